```python
import jax, jax.numpy as jnp
from jax import lax
import numpy as np

D_MODEL = 1024
BATCH = 8
SEQ = 2048
DEPTH = 2

GRID_W = 64
CTX_LEN = 256
HEAD_DIM = 64
RET_HEADS = 4
RET_DK = 64
RET_DV = 128
RET_CHUNK = 128
NA_HEADS = 8
NA_ROWS = 8
NA_COLS = 16
GQA_Q_HEADS = 8
GQA_KV_HEADS = 2
GQA_GROUP = GQA_Q_HEADS // GQA_KV_HEADS
Q_BLOCK = 128
D_FF = 4 * D_MODEL
ROPE_BASE = 10000.0
NORM_EPS = 1e-6
NEG_INF = -1e30

RET_QK_W = RET_HEADS * RET_DK
RET_V_W = RET_HEADS * RET_DV
NA_W = NA_HEADS * HEAD_DIM
GQA_Q_W = GQA_Q_HEADS * HEAD_DIM
GQA_KV_W = GQA_KV_HEADS * HEAD_DIM
SPLIT_SIZES = (RET_QK_W, RET_QK_W, RET_V_W, RET_V_W, NA_W, NA_W, NA_W,
               GQA_Q_W, GQA_KV_W, GQA_KV_W, D_MODEL, D_MODEL, D_MODEL)
SPLIT_POINTS = tuple(int(v) for v in np.cumsum(SPLIT_SIZES)[:-1])
IN_W = int(sum(SPLIT_SIZES))

kernel_name = "hybrid_retention_natten_gqa_dit_block"


def rms_norm(x, gain=None):
    xf = x.astype(jnp.float32)
    y = xf * lax.rsqrt(jnp.mean(xf * xf, axis=-1, keepdims=True) + NORM_EPS)
    if gain is not None:
        y = y * gain.astype(jnp.float32)
    return y.astype(x.dtype)


def heads(t, n_heads):
    return t.reshape(t.shape[:-1] + (n_heads, t.shape[-1] // n_heads))


def axial_rope_angles(n):
    t = jnp.arange(n, dtype=jnp.int32)
    pos = jnp.stack([t // GRID_W, t % GRID_W], axis=-1).astype(jnp.float32)
    n_freq = HEAD_DIM // 4
    inv_freq = ROPE_BASE ** (-jnp.arange(n_freq, dtype=jnp.float32) / n_freq)
    ang = pos[:, :, None] * inv_freq
    return jnp.cos(ang), jnp.sin(ang)


def apply_rope(x, cos, sin):
    b, n, h, d = x.shape
    xf = x.astype(jnp.float32).reshape(b, n, h, 2, 2, d // 4)
    x1, x2 = xf[..., 0, :], xf[..., 1, :]
    cs = cos[None, :, None]
    sn = sin[None, :, None]
    out = jnp.stack([x1 * cs - x2 * sn, x2 * cs + x1 * sn], axis=-2)
    return out.reshape(b, n, h, d).astype(x.dtype)


def attend(q, k, v):
    s = jnp.einsum('bqkgd,bskd->bkgqs', q, k, preferred_element_type=jnp.float32) * (q.shape[-1] ** -0.5)
    p = jax.nn.softmax(s, axis=-1).astype(v.dtype)
    return jnp.einsum('bkgqs,bskd->bqkgd', p, v)


def retention_scan(q, k, v, log_gamma, s0, include_diag, with_output):
    b, L, h, _ = q.shape
    dv = v.shape[-1]
    nc = L // RET_CHUNK
    pos = jnp.arange(RET_CHUNK, dtype=jnp.float32)
    rel = pos[:, None] - pos[None, :]
    keep = (rel >= 0) if include_diag else (rel > 0)
    lg = log_gamma.astype(jnp.float32)
    decay_in = jnp.where(keep[None], jnp.exp(lg[:, None, None] * jnp.maximum(rel, 0.0)[None]), 0.0)
    decay_q = jnp.exp(lg[None, :] * (pos[:, None] + 1.0))
    decay_k = jnp.exp(lg[None, :] * (RET_CHUNK - 1.0 - pos[:, None]))
    decay_s = jnp.exp(lg * RET_CHUNK)

    def chunks(t):
        return t.astype(jnp.float32).reshape(b, nc, RET_CHUNK, h, t.shape[-1]).swapaxes(0, 1)

    def step(s, blk):
        qb, kb, vb = blk
        s_new = s * decay_s[None, :, None, None] + jnp.einsum('bjhd,bjhe->bhde', kb * decay_k[None, :, :, None], vb)
        if not with_output:
            return s_new, None
        a = jnp.einsum('bihd,bjhd->bhij', qb, kb) * decay_in[None]
        o = (jnp.einsum('bhij,bjhe->bihe', a, vb)
             + jnp.einsum('bihd,bhde->bihe', qb, s) * decay_q[None, :, :, None])
        return s_new, o

    s_fin, o = lax.scan(step, s0, (chunks(q), chunks(k), chunks(v)))
    if not with_output:
        return None, s_fin
    return o.swapaxes(0, 1).reshape(b, L, h, dv), s_fin


def retention_bidir(q, k, v, log_gamma, s0_fwd, s0_bwd, with_output):
    o_f, s_f = retention_scan(q, k, v, log_gamma[0], s0_fwd, True, with_output)
    o_b, s_b = retention_scan(q[:, ::-1], k[:, ::-1], v[:, ::-1], log_gamma[1], s0_bwd, False, with_output)
    if not with_output:
        return None, s_f, s_b
    return o_f + o_b[:, ::-1], s_f, s_b


def retention_out(o, gate):
    y = rms_norm(o).reshape(o.shape[:2] + (RET_V_W,))
    return (y * jax.nn.silu(gate.astype(jnp.float32))).astype(gate.dtype)


def neighbourhood_attention(q, k, v, k_ctx, v_ctx, rel_bias):
    b, n, h, d = q.shape
    rows = n // GRID_W
    kr = min(NA_ROWS, rows)
    q = q.reshape(b, rows, GRID_W, h, d)
    k = k.reshape(b, rows, GRID_W, h, d)
    v = v.reshape(b, rows, GRID_W, h, d)
    col = jnp.arange(GRID_W)
    col_start = jnp.clip(col - NA_COLS // 2, 0, GRID_W - NA_COLS)
    in_window = (col[None, :] >= col_start[:, None]) & (col[None, :] < col_start[:, None] + NA_COLS)
    dcol = jnp.clip(col[None, :] - col[:, None], 1 - NA_COLS, NA_COLS - 1) + NA_COLS - 1
    col_bias = jnp.where(in_window[None, None], rel_bias[:, :, dcol].astype(jnp.float32), NEG_INF)
    scale = d ** -0.5

    def row_block(r):
        r0 = jnp.clip(r - kr // 2, 0, rows - kr)
        qr = lax.dynamic_index_in_dim(q, r, axis=1, keepdims=False)
        kb = lax.dynamic_slice_in_dim(k, r0, kr, axis=1)
        vb = lax.dynamic_slice_in_dim(v, r0, kr, axis=1)
        drow = r0 + jnp.arange(kr) - r + NA_ROWS - 1
        bias = jnp.take(col_bias, drow, axis=1).transpose(0, 2, 1, 3)
        s_loc = jnp.einsum('bqhd,bikhd->bhqik', qr, kb, preferred_element_type=jnp.float32) * scale + bias[None]
        s_ctx = jnp.einsum('bqhd,bmhd->bhqm', qr, k_ctx, preferred_element_type=jnp.float32) * scale
        s = jnp.concatenate([s_loc.reshape(b, h, GRID_W, kr * GRID_W), s_ctx], axis=-1)
        p = jax.nn.softmax(s, axis=-1).astype(v.dtype)
        p_loc = p[..., :kr * GRID_W].reshape(b, h, GRID_W, kr, GRID_W)
        p_ctx = p[..., kr * GRID_W:]
        return (jnp.einsum('bhqik,bikhd->bqhd', p_loc, vb)
                + jnp.einsum('bhqm,bmhd->bqhd', p_ctx, v_ctx))

    out = lax.map(row_block, jnp.arange(rows))
    return out.swapaxes(0, 1).reshape(b, n, h * d)


def gqa_latent(q, k, v, k_ctx, v_ctx):
    b, n, _, d = q.shape
    keys = jnp.concatenate([k_ctx, k], axis=1)
    vals = jnp.concatenate([v_ctx, v], axis=1)
    qb = q.reshape(b, n // Q_BLOCK, Q_BLOCK, GQA_KV_HEADS, GQA_GROUP, d).swapaxes(0, 1)
    out = lax.map(lambda blk: attend(blk, keys, vals), qb)
    return out.swapaxes(0, 1).reshape(b, n, GQA_Q_W)


def merge_branches(y_ret, y_na, y_gqa, ga, gb, gc, w_br_ret, w_br_na, w_br_gqa, w_out):
    y = (jax.nn.sigmoid(ga) * (y_ret @ w_br_ret)
         + jax.nn.sigmoid(gb) * (y_na @ w_br_na)
         + jax.nn.sigmoid(gc) * (y_gqa @ w_br_gqa))
    return y @ w_out


def token_mixer(h, hc, w_in, log_gamma, rel_bias, q_gain, k_gain,
                w_br_ret, w_br_na, w_br_gqa, w_out, cos, sin, with_ctx_out):
    b, _, _ = h.shape
    m = hc.shape[1]
    (rq, rk, rv, rg, nq, nk, nv, gq, gk, gv, ga, gb, gc) = jnp.split(h @ w_in, SPLIT_POINTS, axis=-1)
    (crq, crk, crv, crg, cnq, cnk, cnv, cgq, cgk, cgv, cga, cgb, cgc) = jnp.split(hc @ w_in, SPLIT_POINTS, axis=-1)

    ret_scale = RET_DK ** -0.5
    s0 = jnp.zeros((b, RET_HEADS, RET_DK, RET_DV), jnp.float32)
    o_ctx_ret, s_fwd, s_bwd = retention_bidir(
        heads(crq, RET_HEADS), heads(crk, RET_HEADS) * ret_scale, heads(crv, RET_HEADS),
        log_gamma, s0, s0, with_ctx_out)
    o_ret, _, _ = retention_bidir(
        apply_rope(heads(rq, RET_HEADS), cos, sin),
        apply_rope(heads(rk, RET_HEADS), cos, sin) * ret_scale,
        heads(rv, RET_HEADS), log_gamma, s_fwd, s_bwd, True)
    y_ret = retention_out(o_ret, rg)

    cnk_h, cnv_h = heads(cnk, NA_HEADS), heads(cnv, NA_HEADS)
    y_na = neighbourhood_attention(heads(nq, NA_HEADS), heads(nk, NA_HEADS), heads(nv, NA_HEADS),
                                   cnk_h, cnv_h, rel_bias)

    cgk_h = rms_norm(heads(cgk, GQA_KV_HEADS), k_gain)
    cgv_h = heads(cgv, GQA_KV_HEADS)
    q_c = apply_rope(rms_norm(heads(gq, GQA_Q_HEADS), q_gain), cos, sin)
    k_c = apply_rope(rms_norm(heads(gk, GQA_KV_HEADS), k_gain), cos, sin)
    y_gqa = gqa_latent(q_c, k_c, heads(gv, GQA_KV_HEADS), cgk_h, cgv_h)

    y = merge_branches(y_ret, y_na, y_gqa, ga, gb, gc, w_br_ret, w_br_na, w_br_gqa, w_out)
    if not with_ctx_out:
        return y, None

    yc_ret = retention_out(o_ctx_ret, crg)
    yc_na = attend(heads(cnq, NA_HEADS)[:, :, :, None, :], cnk_h, cnv_h).reshape(b, m, NA_W)
    cq = rms_norm(heads(cgq, GQA_Q_HEADS), q_gain).reshape(b, m, GQA_KV_HEADS, GQA_GROUP, HEAD_DIM)
    yc_gqa = attend(cq, cgk_h, cgv_h).reshape(b, m, GQA_Q_W)
    yc = merge_branches(yc_ret, yc_na, yc_gqa, cga, cgb, cgc, w_br_ret, w_br_na, w_br_gqa, w_out)
    return y, yc


def squared_relu_mlp(h, w1, w2):
    return jnp.square(jax.nn.relu(h @ w1)) @ w2


def setup_inputs(seed: int = 0) -> dict:
    key = jax.random.key(seed)
    ks = jax.random.split(key, 24)
    f32 = jnp.float32
    d = D_MODEL

    def nrm(k, shape, scale):
        return jax.random.normal(k, shape, f32) * scale

    decay_base = jnp.log(2.0 ** (5.0 + jnp.arange(RET_HEADS, dtype=f32)) - 1.0)
    return {
        "x": nrm(ks[0], (BATCH, SEQ, d), 1.0),
        "c": nrm(ks[1], (BATCH, d), 1.0),
        "ctx": nrm(ks[2], (BATCH, CTX_LEN, d), 1.0),
        "c_ctx": nrm(ks[3], (d,), 1.0),
        "w_mod": nrm(ks[4], (DEPTH, d, 6 * d), 0.5 * d ** -0.5),
        "b_mod": nrm(ks[5], (DEPTH, 6 * d), 0.02),
        "g_pre_mix": 1.0 + nrm(ks[6], (DEPTH, d), 0.1),
        "g_post_mix": 1.0 + nrm(ks[7], (DEPTH, d), 0.1),
        "g_pre_mlp": 1.0 + nrm(ks[8], (DEPTH, d), 0.1),
        "g_post_mlp": 1.0 + nrm(ks[9], (DEPTH, d), 0.1),
        "w_in": nrm(ks[10], (DEPTH, d, IN_W), d ** -0.5),
        "ret_decay_logit": decay_base + nrm(ks[11], (DEPTH, 2, RET_HEADS), 0.1),
        "na_rel_bias": nrm(ks[12], (DEPTH, NA_HEADS, 2 * NA_ROWS - 1, 2 * NA_COLS - 1), 0.1),
        "gqa_q_norm": 1.0 + nrm(ks[13], (DEPTH, HEAD_DIM), 0.1),
        "gqa_k_norm": 1.0 + nrm(ks[14], (DEPTH, HEAD_DIM), 0.1),
        "w_br_ret": nrm(ks[15], (DEPTH, RET_V_W, d), RET_V_W ** -0.5),
        "w_br_na": nrm(ks[16], (DEPTH, NA_W, d), NA_W ** -0.5),
        "w_br_gqa": nrm(ks[17], (DEPTH, GQA_Q_W, d), GQA_Q_W ** -0.5),
        "w_out": nrm(ks[18], (DEPTH, d, d), d ** -0.5),
        "w_mlp_in": nrm(ks[19], (DEPTH, d, D_FF), d ** -0.5),
        "w_mlp_out": nrm(ks[20], (DEPTH, D_FF, d), D_FF ** -0.5),
    }


def reference(x, c, ctx, c_ctx, w_mod, b_mod, g_pre_mix, g_post_mix, g_pre_mlp, g_post_mlp,
              w_in, ret_decay_logit, na_rel_bias, gqa_q_norm, gqa_k_norm,
              w_br_ret, w_br_na, w_br_gqa, w_out, w_mlp_in, w_mlp_out):
    cos, sin = axial_rope_angles(x.shape[1])
    silu_c = jax.nn.silu(c)
    silu_cc = jax.nn.silu(c_ctx)
    for l in range(DEPTH):
        last = l == DEPTH - 1
        mod_lat = (silu_c @ w_mod[l] + b_mod[l])[:, None, :]
        mod_ctx = silu_cc @ w_mod[l] + b_mod[l]
        sh1, sc1, gt1, sh2, sc2, gt2 = jnp.split(mod_lat, 6, axis=-1)
        csh1, csc1, cgt1, csh2, csc2, cgt2 = jnp.split(mod_ctx, 6, axis=-1)
        log_gamma = jax.nn.log_sigmoid(ret_decay_logit[l].astype(jnp.float32))

        h = rms_norm(x, g_pre_mix[l]) * (1.0 + sc1) + sh1
        hc = rms_norm(ctx, g_pre_mix[l]) * (1.0 + csc1) + csh1
        y, yc = token_mixer(h, hc, w_in[l], log_gamma, na_rel_bias[l], gqa_q_norm[l], gqa_k_norm[l],
                            w_br_ret[l], w_br_na[l], w_br_gqa[l], w_out[l], cos, sin, not last)
        x = x + gt1 * rms_norm(y, g_post_mix[l])
        h = rms_norm(x, g_pre_mlp[l]) * (1.0 + sc2) + sh2
        x = x + gt2 * rms_norm(squared_relu_mlp(h, w_mlp_in[l], w_mlp_out[l]), g_post_mlp[l])

        if not last:
            ctx = ctx + cgt1 * rms_norm(yc, g_post_mix[l])
            hc = rms_norm(ctx, g_pre_mlp[l]) * (1.0 + csc2) + csh2
            ctx = ctx + cgt2 * rms_norm(squared_relu_mlp(hc, w_mlp_in[l], w_mlp_out[l]), g_post_mlp[l])
    return x
```

```python
import functools

import jax
import jax.numpy as jnp
import numpy as np
from jax import lax
from jax.experimental import pallas as pl
from jax.experimental.pallas import tpu as pltpu

F32 = jnp.float32
MXU_DTYPE = jnp.bfloat16

D_MODEL = 1024
GRID_W = 64
CTX_LEN = 256
HEAD_DIM = 64
RET_HEADS = 4
RET_CHUNK = 128
NA_HEADS = 8
NA_ROWS = 8
NA_COLS = 16
GQA_Q_HEADS = 8
D_FF = 4 * D_MODEL
ROPE_BASE = 10000.0
NORM_EPS = 1e-6
NEG_INF = -1e30

LANES = 128
HEAD_W = 512
C_RQ, C_RK, C_RV, C_RG = 0, 256, 512, 1024
C_NQ, C_NK, C_NV = 1536, 2048, 2560
C_GQ, C_GK, C_GV = 3072, 3584, 3712
C_GATE = 3840
IN_W = 6912

TOKEN_TILE = 256
VMEM_LIMIT = 56 * 1024 * 1024


def _rms(x):
    return x * lax.rsqrt(jnp.mean(x * x, axis=-1, keepdims=True) + NORM_EPS)


def _dot(a, b):
    return jnp.dot(a, b, preferred_element_type=F32)


def _dot_nt(a, b):
    return lax.dot_general(a, b, (((1,), (1,)), ((), ())), preferred_element_type=F32)


def _resident(shape):
    return pl.BlockSpec(shape, lambda *_: (0,) * len(shape), pipeline_mode=pl.Buffered(1))


def _params(n_axes):
    return pltpu.CompilerParams(dimension_semantics=("arbitrary",) * n_axes,
                                vmem_limit_bytes=VMEM_LIMIT)


def _mod_kernel(c_ref, w_ref, b_ref, o_ref):
    c = c_ref[...]
    s = (c * jax.nn.sigmoid(c)).astype(MXU_DTYPE)
    o_ref[0] = _dot(s, w_ref[0].astype(MXU_DTYPE)) + b_ref[0]


def _modulation(cs, w_mod, b_mod):
    depth, d, n = w_mod.shape
    r = cs.shape[0]
    tn = 1024
    return pl.pallas_call(
        _mod_kernel,
        grid=(depth, n // tn),
        in_specs=[pl.BlockSpec((r, d), lambda l, j: (0, 0)),
                  pl.BlockSpec((1, d, tn), lambda l, j: (l, 0, j)),
                  pl.BlockSpec((1, 1, tn), lambda l, j: (l, 0, j))],
        out_specs=pl.BlockSpec((1, r, tn), lambda l, j: (l, 0, j)),
        out_shape=jax.ShapeDtypeStruct((depth, r, n), F32),
        compiler_params=_params(2),
        name="adaln_modulation",
    )(cs, w_mod, b_mod.reshape(depth, 1, n))


def _inproj_kernel(s_ref, sh_ref, sc_ref, g_ref, w_ref, cos_ref, sin_ref, qg_ref, kg_ref, gsum_ref,
                   rqk_ref, rv_ref, rg_ref, nq_ref, nk_ref, nv_ref, gq_ref, gkv_ref, gate_ref):
    x = s_ref[0]
    h = _rms(x) * g_ref[...]
    h = h * (1.0 + sc_ref[0]) + sh_ref[0]
    hb = h.astype(MXU_DTYPE)
    cos = cos_ref[...]
    sin = sin_ref[...]
    lane = lax.broadcasted_iota(jnp.int32, cos.shape, 1)
    first_half = (lane % 32) < 16
    gsum = gsum_ref[...]

    def proj(col, width):
        return _dot(hb, w_ref[:, col:col + width])

    def rope(v):
        partner = jnp.where(first_half, pltpu.roll(v, LANES - 16, 1), pltpu.roll(v, 16, 1))
        return v * cos + partner * sin

    def head_norm(v, gain):
        sq = v * v
        hi = sq.astype(MXU_DTYPE)
        lo = (sq - hi.astype(F32)).astype(MXU_DTYPE)
        ss = _dot(hi, gsum) + _dot(lo, gsum)
        return (v * lax.rsqrt(ss * (1.0 / HEAD_DIM) + NORM_EPS)) * gain

    odt = rqk_ref.dtype
    ret_scale = HEAD_DIM ** -0.5
    att_scale = HEAD_DIM ** -0.5

    rq = proj(C_RQ, 256)
    rk = proj(C_RK, 256)
    for j in range(2):
        sl = slice(j * LANES, (j + 1) * LANES)
        rqk_ref[0, :, j * LANES:(j + 1) * LANES] = rope(rq[:, sl]).astype(odt)
        rqk_ref[0, :, 256 + j * LANES:256 + (j + 1) * LANES] = (rope(rk[:, sl]) * ret_scale).astype(odt)
    rv_ref[0] = proj(C_RV, HEAD_W).astype(odt)
    rg = proj(C_RG, HEAD_W)
    rg_ref[0] = (rg * jax.nn.sigmoid(rg)).astype(odt)
    nq_ref[0] = (proj(C_NQ, HEAD_W) * att_scale).astype(odt)
    nk_ref[0] = proj(C_NK, HEAD_W).astype(odt)
    nv_ref[0] = proj(C_NV, HEAD_W).astype(odt)

    gq = proj(C_GQ, HEAD_W)
    qg = qg_ref[...]
    for j in range(4):
        sl = slice(j * LANES, (j + 1) * LANES)
        gq_ref[0, :, sl] = (rope(head_norm(gq[:, sl], qg)) * att_scale).astype(odt)
    gk = rope(head_norm(proj(C_GK, LANES), kg_ref[...]))
    gv = proj(C_GV, LANES)
    gkv_ref[0, :, 0:LANES] = gk.astype(odt)
    gkv_ref[0, :, LANES:2 * LANES] = pltpu.roll(gk, HEAD_DIM, 1).astype(odt)
    gkv_ref[0, :, 2 * LANES:3 * LANES] = gv.astype(odt)
    gkv_ref[0, :, 3 * LANES:4 * LANES] = pltpu.roll(gv, HEAD_DIM, 1).astype(odt)

    for j in range(6):
        sl = slice(j * HEAD_W, (j + 1) * HEAD_W)
        gate_ref[0, :, sl] = jax.nn.sigmoid(proj(C_GATE + j * HEAD_W, HEAD_W)).astype(odt)


def _in_projection(stream, mod, layer_gain, w_in, cos_t, sin_t, q_gain, k_gain, gsum):
    b, t, d = stream.shape
    tm = TOKEN_TILE
    n_ctx = CTX_LEN // tm
    rows = mod.shape[0]
    mod3 = mod.reshape(rows, 1, 6 * d)

    def mod_spec(chunk):
        return pl.BlockSpec((1, 1, d), lambda i, j: (jnp.where(j < n_ctx, rows - 1, i), 0, chunk))

    def tok(width):
        return pl.BlockSpec((1, tm, width), lambda i, j: (i, j, 0))

    def out(width):
        return jax.ShapeDtypeStruct((b, t, width), MXU_DTYPE)

    return pl.pallas_call(
        _inproj_kernel,
        grid=(b, t // tm),
        in_specs=[tok(d), mod_spec(0), mod_spec(1), _resident((1, d)), _resident((d, IN_W)),
                  pl.BlockSpec((tm, LANES), lambda i, j: (j, 0)),
                  pl.BlockSpec((tm, LANES), lambda i, j: (j, 0)),
                  _resident((1, LANES)), _resident((1, LANES)), _resident((LANES, LANES))],
        out_specs=[tok(HEAD_W)] * 8 + [tok(3 * d)],
        out_shape=[out(HEAD_W)] * 8 + [out(3 * d)],
        compiler_params=_params(2),
        name="in_projection",
    )(stream, mod3, mod3, layer_gain, w_in, cos_t, sin_t, q_gain, k_gain, gsum)


def _retention_kernel(logit_ref, qk_ref, v_ref, g_ref, o_ref, dec_ref, sf_ref, sb_ref, acc_ref, *, n_tok):
    c_len = RET_CHUNK
    n_chunks = n_tok // c_len
    n_ctx_chunks = CTX_LEN // c_len
    heads = RET_HEADS
    t_mask, t_dkf, t_dkb, t_dsf, t_dsb = (k * heads for k in range(5))
    t_dqf, t_dqb = 5 * heads, 5 * heads + heads // 2

    row = lax.broadcasted_iota(jnp.int32, (c_len, LANES), 0)
    lane = lax.broadcasted_iota(jnp.int32, (c_len, LANES), 1)
    lane_lo = lane < HEAD_DIM

    @pl.when(pl.program_id(0) == 0)
    def _build_tables():
        rowf = row.astype(F32)
        lanef = lane.astype(F32)
        rel = rowf - lanef
        lg = [[jnp.broadcast_to(jax.nn.log_sigmoid(logit_ref[dd, hh])[0:1, :], (c_len, LANES))
               for hh in range(heads)] for dd in range(2)]
        for hh in range(heads):
            lf, lb = lg[0][hh], lg[1][hh]
            fwd = jnp.where(rel >= 0, jnp.exp(lf * jnp.maximum(rel, 0.0)), 0.0)
            bwd = jnp.where(rel < 0, jnp.exp(lb * jnp.maximum(-rel, 0.0)), 0.0)
            dec_ref[t_mask + hh] = fwd + bwd
            dec_ref[t_dkf + hh] = jnp.exp(lf * (c_len - 1.0 - lanef))
            dec_ref[t_dkb + hh] = jnp.exp(lb * lanef)
            dec_ref[t_dsf + hh] = jnp.exp(lf * float(c_len))
            dec_ref[t_dsb + hh] = jnp.exp(lb * float(c_len))
        for p in range(heads // 2):
            lf = jnp.where(lane_lo, lg[0][2 * p], lg[0][2 * p + 1])
            lb = jnp.where(lane_lo, lg[1][2 * p], lg[1][2 * p + 1])
            dec_ref[t_dqf + p] = jnp.exp(lf * (rowf + 1.0))
            dec_ref[t_dqb + p] = jnp.exp(lb * (float(c_len) - rowf))

    sf_ref[...] = jnp.zeros_like(sf_ref)
    sb_ref[...] = jnp.zeros_like(sb_ref)

    def load(c):
        r0 = pl.multiple_of(c * c_len, c_len)
        return r0, pl.ds(r0, c_len)

    def masked_kt(k, half):
        keep = lane_lo if half == 0 else jnp.logical_not(lane_lo)
        return jnp.where(keep, k, 0.0).T

    def fwd_chunk(c, carry):
        _, rows = load(c)
        for p in range(heads // 2):
            q = qk_ref[0, rows, p * LANES:(p + 1) * LANES].astype(F32)
            k = qk_ref[0, rows, 256 + p * LANES:256 + (p + 1) * LANES].astype(F32)
            qf = (q * dec_ref[t_dqf + p]).astype(MXU_DTYPE)
            qb = q.astype(MXU_DTYPE)
            for half in range(2):
                hh = 2 * p + half
                v = v_ref[0, rows, hh * LANES:(hh + 1) * LANES]
                keep = lane_lo if half == 0 else jnp.logical_not(lane_lo)
                km = jnp.where(keep, k, 0.0).astype(MXU_DTYPE)
                a = _dot_nt(qb, km) * dec_ref[t_mask + hh]
                o = _dot(a.astype(MXU_DTYPE), v) + _dot(qf, sf_ref[hh].astype(MXU_DTYPE))
                acc_ref[rows, hh * LANES:(hh + 1) * LANES] = o
                kt = (masked_kt(k, half) * dec_ref[t_dkf + hh]).astype(MXU_DTYPE)
                sf_ref[hh] = sf_ref[hh] * dec_ref[t_dsf + hh] + _dot(kt, v)
        return carry

    def bwd_chunk(c):
        _, rows = load(c)
        for p in range(heads // 2):
            q = qk_ref[0, rows, p * LANES:(p + 1) * LANES].astype(F32)
            k = qk_ref[0, rows, 256 + p * LANES:256 + (p + 1) * LANES].astype(F32)
            qb = (q * dec_ref[t_dqb + p]).astype(MXU_DTYPE)
            for half in range(2):
                hh = 2 * p + half
                sl = slice(hh * LANES, (hh + 1) * LANES)
                v = v_ref[0, rows, sl]
                o = acc_ref[rows, sl] + _dot(qb, sb_ref[hh].astype(MXU_DTYPE))
                y = _rms(o) * g_ref[0, rows, sl].astype(F32)
                o_ref[0, rows, sl] = y.astype(o_ref.dtype)
                kt = (masked_kt(k, half) * dec_ref[t_dkb + hh]).astype(MXU_DTYPE)
                sb_ref[hh] = sb_ref[hh] * dec_ref[t_dsb + hh] + _dot(kt, v)

    lax.fori_loop(0, n_chunks, fwd_chunk, 0)

    def bwd_ctx(i, carry):
        bwd_chunk(n_ctx_chunks - 1 - i)
        return carry

    def bwd_lat(i, carry):
        bwd_chunk(n_chunks - 1 - i)
        return carry

    lax.fori_loop(0, n_ctx_chunks, bwd_ctx, 0)
    lax.fori_loop(0, n_chunks - n_ctx_chunks, bwd_lat, 0)


def _retention(logit_tile, rqk, rv, rg):
    b, t, _ = rqk.shape
    n_tables = 5 * RET_HEADS + RET_HEADS
    blk = pl.BlockSpec((1, t, HEAD_W), lambda i: (i, 0, 0))
    return pl.pallas_call(
        functools.partial(_retention_kernel, n_tok=t),
        grid=(b,),
        in_specs=[_resident(logit_tile.shape), blk, blk, blk],
        out_specs=blk,
        out_shape=jax.ShapeDtypeStruct((b, t, HEAD_W), MXU_DTYPE),
        scratch_shapes=[pltpu.VMEM((n_tables, RET_CHUNK, LANES), F32),
                        pltpu.VMEM((RET_HEADS, LANES, LANES), F32),
                        pltpu.VMEM((RET_HEADS, LANES, LANES), F32),
                        pltpu.VMEM((t, HEAD_W), F32)],
        compiler_params=_params(1),
        name="retention",
    )(logit_tile, rqk, rv, rg)


def _softmax_pv(scores, values):
    m = scores[0].max(axis=-1, keepdims=True)
    for s in scores[1:]:
        m = jnp.maximum(m, s.max(axis=-1, keepdims=True))
    den = None
    acc = None
    for s, v in zip(scores, values):
        e = jnp.exp(s - m)
        part = e.sum(axis=-1, keepdims=True)
        pv = _dot(e.astype(MXU_DTYPE), v)
        den = part if den is None else den + part
        acc = pv if acc is None else acc + pv
    return acc / den


def _na_kernel(q_ref, k_ref, v_ref, bias_ref, o_ref, *, n_tok, with_ctx_out):
    n_rows = (n_tok - CTX_LEN) // GRID_W
    win = NA_ROWS * GRID_W
    lane_q = lax.broadcasted_iota(jnp.int32, (GRID_W, LANES), 1) < HEAD_DIM
    zero = jnp.zeros((), MXU_DTYPE)

    def row_body(r, carry):
        r0 = jnp.clip(r - NA_ROWS // 2, 0, n_rows - NA_ROWS)
        off = r0 - r + NA_ROWS - 1
        q_rows = pl.ds(pl.multiple_of(CTX_LEN + r * GRID_W, GRID_W), GRID_W)
        k_rows = pl.ds(pl.multiple_of(CTX_LEN + r0 * GRID_W, GRID_W), win)
        for p in range(NA_HEADS // 2):
            sl = slice(p * LANES, (p + 1) * LANES)
            q = q_ref[0, q_rows, sl]
            k_loc, v_loc = k_ref[0, k_rows, sl], v_ref[0, k_rows, sl]
            k_ctx, v_ctx = k_ref[0, 0:CTX_LEN, sl], v_ref[0, 0:CTX_LEN, sl]
            outs = []
            for half in range(2):
                qm = jnp.where(lane_q if half == 0 else jnp.logical_not(lane_q), q, zero)
                s_loc = _dot_nt(qm, k_loc) + bias_ref[off, 2 * p + half]
                s_ctx = _dot_nt(qm, k_ctx)
                outs.append(_softmax_pv([s_loc, s_ctx], [v_loc, v_ctx]))
            o_ref[0, q_rows, sl] = jnp.where(lane_q, outs[0], outs[1]).astype(o_ref.dtype)
        return carry

    lax.fori_loop(0, n_rows, row_body, 0)

    lane_c = lax.broadcasted_iota(jnp.int32, (CTX_LEN, LANES), 1) < HEAD_DIM
    for p in range(NA_HEADS // 2):
        sl = slice(p * LANES, (p + 1) * LANES)
        if with_ctx_out:
            q = q_ref[0, 0:CTX_LEN, sl]
            k_ctx, v_ctx = k_ref[0, 0:CTX_LEN, sl], v_ref[0, 0:CTX_LEN, sl]
            outs = []
            for half in range(2):
                qm = jnp.where(lane_c if half == 0 else jnp.logical_not(lane_c), q, zero)
                outs.append(_softmax_pv([_dot_nt(qm, k_ctx)], [v_ctx]))
            o_ref[0, 0:CTX_LEN, sl] = jnp.where(lane_c, outs[0], outs[1]).astype(o_ref.dtype)
        else:
            o_ref[0, 0:CTX_LEN, sl] = jnp.zeros((CTX_LEN, LANES), o_ref.dtype)


def _neighbourhood(nq, nk, nv, bias, with_ctx_out):
    b, t, _ = nq.shape
    blk = pl.BlockSpec((1, t, HEAD_W), lambda i: (i, 0, 0))
    return pl.pallas_call(
        functools.partial(_na_kernel, n_tok=t, with_ctx_out=with_ctx_out),
        grid=(b,),
        in_specs=[blk, blk, blk, _resident(bias.shape)],
        out_specs=blk,
        out_shape=jax.ShapeDtypeStruct((b, t, HEAD_W), MXU_DTYPE),
        compiler_params=_params(1),
        name="neighbourhood_attention",
    )(nq, nk, nv, bias)


def _na_bias_table(rel_bias):
    col = jnp.arange(GRID_W)
    col_start = jnp.clip(col - NA_COLS // 2, 0, GRID_W - NA_COLS)
    in_window = (col[None, :] >= col_start[:, None]) & (col[None, :] < col_start[:, None] + NA_COLS)
    dcol = jnp.clip(col[None, :] - col[:, None], 1 - NA_COLS, NA_COLS - 1) + NA_COLS - 1
    col_bias = jnp.where(in_window[None, None], rel_bias[:, :, dcol].astype(F32), NEG_INF)
    drow = jnp.arange(NA_ROWS)[:, None] + jnp.arange(NA_ROWS)[None, :]
    tbl = col_bias[:, drow]
    tbl = tbl.transpose(1, 0, 3, 2, 4)
    return tbl.reshape(NA_ROWS, NA_HEADS, GRID_W, NA_ROWS * GRID_W)


def _gqa_kernel(q_ref, kv_ref, o_ref, *, n_tok, tile_offset):
    tq = q_ref.shape[1]
    lane_lo = lax.broadcasted_iota(jnp.int32, (tq, LANES), 1) < HEAD_DIM
    zero = jnp.zeros((), MXU_DTYPE)
    n_ctx_tiles = CTX_LEN // tq

    def run(n_keys):
        for p in range(GQA_Q_HEADS // 2):
            kv_head = p // 2
            sl = slice(p * LANES, (p + 1) * LANES)
            q = q_ref[0, :, sl]
            outs = []
            for half in range(2):
                order = 0 if half == kv_head else 1
                k = kv_ref[0, 0:n_keys, order * LANES:(order + 1) * LANES]
                v = kv_ref[0, 0:n_keys, (2 + order) * LANES:(3 + order) * LANES]
                qm = jnp.where(lane_lo if half == 0 else jnp.logical_not(lane_lo), q, zero)
                outs.append(_softmax_pv([_dot_nt(qm, k)], [v]))
            o_ref[0, :, sl] = jnp.where(lane_lo, outs[0], outs[1]).astype(o_ref.dtype)

    if tile_offset >= n_ctx_tiles:
        run(n_tok)
    else:
        is_ctx = pl.program_id(1) + tile_offset < n_ctx_tiles
        pl.when(is_ctx)(lambda: run(CTX_LEN))
        pl.when(jnp.logical_not(is_ctx))(lambda: run(n_tok))


def _gqa(gq, gkv, with_ctx_out):
    b, t, _ = gq.shape
    tq = TOKEN_TILE
    off = 0 if with_ctx_out else CTX_LEN // tq
    return pl.pallas_call(
        functools.partial(_gqa_kernel, n_tok=t, tile_offset=off),
        grid=(b, t // tq - off),
        in_specs=[pl.BlockSpec((1, tq, HEAD_W), lambda i, j: (i, j + off, 0)),
                  pl.BlockSpec((1, t, HEAD_W), lambda i, j: (i, 0, 0))],
        out_specs=pl.BlockSpec((1, tq, HEAD_W), lambda i, j: (i, j, 0)),
        out_shape=jax.ShapeDtypeStruct((b, t - off * tq, HEAD_W), MXU_DTYPE),
        compiler_params=_params(2),
        name="gqa_attention",
    )(gq, gkv)


def _post_kernel(yr_ref, yn_ref, yg_ref, gate_ref, s_ref, gt1_ref, sh2_ref, sc2_ref, gt2_ref,
                 gpost_ref, gpre_ref, gpost2_ref, wbr_ref, wout_ref, w1_ref, w2_ref, o_ref):
    d = s_ref.shape[-1]
    y = None
    for i, br in enumerate((yr_ref, yn_ref, yg_ref)):
        z = _dot(br[0], wbr_ref[i]) * gate_ref[0, :, i * d:(i + 1) * d].astype(F32)
        y = z if y is None else y + z
    y = _dot(y.astype(MXU_DTYPE), wout_ref[...])
    x = s_ref[0] + gt1_ref[0] * (_rms(y) * gpost_ref[...])

    h = _rms(x) * gpre_ref[...]
    h = (h * (1.0 + sc2_ref[0]) + sh2_ref[0]).astype(MXU_DTYPE)
    acc = None
    for j in range(D_FF // d):
        u = jnp.maximum(_dot(h, w1_ref[:, j * d:(j + 1) * d]), 0.0)
        part = _dot((u * u).astype(MXU_DTYPE), w2_ref[j * d:(j + 1) * d, :])
        acc = part if acc is None else acc + part
    o_ref[0] = x + gt2_ref[0] * (_rms(acc) * gpost2_ref[...])


def _post(y_ret, y_na, y_gqa, gates, stream, mod, g_post_mix, g_pre_mlp, g_post_mlp,
          w_br, w_out, w1, w2, latent_only):
    b, t, d = stream.shape
    tm = TOKEN_TILE
    n_ctx = CTX_LEN // tm
    off = n_ctx if latent_only else 0
    rows = mod.shape[0]
    mod3 = mod.reshape(rows, 1, 6 * d)

    def mod_spec(chunk):
        return pl.BlockSpec((1, 1, d), lambda i, j: (jnp.where(j + off < n_ctx, rows - 1, i), 0, chunk))

    def tok(width, shift=off):
        return pl.BlockSpec((1, tm, width), lambda i, j: (i, j + shift, 0))

    return pl.pallas_call(
        _post_kernel,
        grid=(b, t // tm - off),
        in_specs=[tok(HEAD_W), tok(HEAD_W), tok(HEAD_W, 0), tok(3 * d), tok(d),
                  mod_spec(2), mod_spec(3), mod_spec(4), mod_spec(5),
                  _resident((1, d)), _resident((1, d)), _resident((1, d)),
                  _resident(w_br.shape), _resident(w_out.shape), _resident(w1.shape), _resident(w2.shape)],
        out_specs=pl.BlockSpec((1, tm, d), lambda i, j: (i, j, 0)),
        out_shape=jax.ShapeDtypeStruct((b, t - off * tm, d), F32),
        compiler_params=_params(2),
        name="merge_mlp",
    )(y_ret, y_na, y_gqa, gates, stream, mod3, mod3, mod3, mod3,
      g_post_mix, g_pre_mlp, g_post_mlp, w_br, w_out, w1, w2)


def _rope_tables(n_latent):
    t = jnp.arange(n_latent, dtype=jnp.int32)
    pos = jnp.stack([t // GRID_W, t % GRID_W], axis=-1).astype(F32)
    n_freq = HEAD_DIM // 4
    inv_freq = ROPE_BASE ** (-jnp.arange(n_freq, dtype=F32) / n_freq)
    lane = np.arange(LANES) % HEAD_DIM
    axis, second, freq = lane // 32, (lane % 32) // 16, lane % 16
    ang = pos[:, axis] * inv_freq[freq][None, :]
    cos = jnp.cos(ang)
    sin = jnp.sin(ang) * jnp.asarray(np.where(second == 1, 1.0, -1.0), F32)
    cos = jnp.concatenate([jnp.ones((CTX_LEN, LANES), F32), cos], axis=0)
    sin = jnp.concatenate([jnp.zeros((CTX_LEN, LANES), F32), sin], axis=0)
    return cos, sin


def kernel(x, c, ctx, c_ctx, w_mod, b_mod, g_pre_mix, g_post_mix, g_pre_mlp, g_post_mlp, w_in, ret_decay_logit, na_rel_bias, gqa_q_norm, gqa_k_norm, w_br_ret, w_br_na, w_br_gqa, w_out, w_mlp_in, w_mlp_out):
    b, n, d = x.shape
    depth = w_mod.shape[0]
    cdt = MXU_DTYPE

    cos_t, sin_t = _rope_tables(n)
    lane_head = np.arange(LANES) // HEAD_DIM
    gsum = jnp.asarray(lane_head[:, None] == lane_head[None, :], cdt)

    n_rows = -(-(b + 1) // 8) * 8
    cs = jnp.concatenate([c, jnp.zeros((n_rows - b - 1, d), F32), c_ctx[None, :]], axis=0)
    mod = _modulation(cs, w_mod, b_mod)

    stream = jnp.concatenate([ctx, x], axis=1)
    for l in range(depth):
        last = l == depth - 1
        proj = _in_projection(stream, mod[l], g_pre_mix[l][None, :], w_in[l].astype(cdt), cos_t, sin_t,
                              jnp.tile(gqa_q_norm[l], 2)[None, :], jnp.tile(gqa_k_norm[l], 2)[None, :], gsum)
        rqk, rv, rg, nq, nk, nv, gq, gkv, gates = proj
        logit_tile = jnp.broadcast_to(ret_decay_logit[l].astype(F32)[:, :, None, None], (2, RET_HEADS, 8, LANES))
        y_ret = _retention(logit_tile, rqk, rv, rg)
        y_na = _neighbourhood(nq, nk, nv, _na_bias_table(na_rel_bias[l]), not last)
        y_gqa = _gqa(gq, gkv, not last)
        w_br = jnp.stack([w_br_ret[l], w_br_na[l], w_br_gqa[l]]).astype(cdt)
        stream = _post(y_ret, y_na, y_gqa, gates, stream, mod[l], g_post_mix[l][None, :], g_pre_mlp[l][None, :],
                       g_post_mlp[l][None, :], w_br, w_out[l].astype(cdt), w_mlp_in[l].astype(cdt),
                       w_mlp_out[l].astype(cdt), last)
    return stream
```

```python
import functools

import jax
import jax.numpy as jnp
import numpy as np
from jax import lax
from jax.experimental import pallas as pl
from jax.experimental.pallas import tpu as pltpu

F32 = jnp.float32
MXU_DTYPE = jnp.bfloat16

D_MODEL = 1024
GRID_W = 64
CTX_LEN = 256
HEAD_DIM = 64
RET_HEADS = 4
RET_CHUNK = 128
NA_HEADS = 8
NA_ROWS = 8
NA_COLS = 16
NA_BLOCK_ROWS = 4
NA_UNION_ROWS = NA_BLOCK_ROWS + NA_ROWS
GQA_Q_HEADS = 8
GQA_KV_HEADS = 2
GQA_KEY_CHUNK = 768
D_FF = 4 * D_MODEL
ROPE_BASE = 10000.0
NORM_EPS = 1e-6
NEG_INF = -1e30

LANES = 128
HEAD_W = 512
C_RQ, C_RK, C_RV, C_RG = 0, 256, 512, 1024
C_NQ, C_NK, C_NV = 1536, 2048, 2560
C_GQ, C_GK, C_GV = 3072, 3584, 3712
C_GATE = 3840
IN_W = 6912

TOKEN_TILE = 256
VMEM_LIMIT = 56 * 1024 * 1024


def _rms(x):
    return x * lax.rsqrt(jnp.mean(x * x, axis=-1, keepdims=True) + NORM_EPS)


def _dot(a, b):
    return jnp.dot(a, b, preferred_element_type=F32)


def _dot_nt(a, b):
    return lax.dot_general(a, b, (((1,), (1,)), ((), ())), preferred_element_type=F32)


def _resident(shape):
    return pl.BlockSpec(shape, lambda *_: (0,) * len(shape), pipeline_mode=pl.Buffered(1))


def _params(n_axes):
    return pltpu.CompilerParams(dimension_semantics=("arbitrary",) * n_axes,
                                vmem_limit_bytes=VMEM_LIMIT)


def _mod_kernel(c_ref, w_ref, b_ref, o_ref):
    c = c_ref[...]
    s = (c * jax.nn.sigmoid(c)).astype(MXU_DTYPE)
    o_ref[0] = _dot(s, w_ref[0].astype(MXU_DTYPE)) + b_ref[0]


def _modulation(cs, w_mod, b_mod):
    depth, d, n = w_mod.shape
    r = cs.shape[0]
    tn = 1024
    return pl.pallas_call(
        _mod_kernel,
        grid=(depth, n // tn),
        in_specs=[pl.BlockSpec((r, d), lambda l, j: (0, 0)),
                  pl.BlockSpec((1, d, tn), lambda l, j: (l, 0, j)),
                  pl.BlockSpec((1, 1, tn), lambda l, j: (l, 0, j))],
        out_specs=pl.BlockSpec((1, r, tn), lambda l, j: (l, 0, j)),
        out_shape=jax.ShapeDtypeStruct((depth, r, n), F32),
        compiler_params=_params(2),
        name="adaln_modulation",
    )(cs, w_mod, b_mod.reshape(depth, 1, n))


def _pick_source(ctx_ref, lat_ref, tile):
    n_ctx = CTX_LEN // ctx_ref.shape[1]
    return jnp.where(tile < n_ctx, ctx_ref[0], lat_ref[0])


def _source_specs(sources, tm, d, off):
    n_ctx = CTX_LEN // tm
    if len(sources) == 1:
        lat_base = 0
        arrays = (sources[0], sources[0])
    else:
        lat_base = n_ctx
        arrays = tuple(sources)
    specs = [pl.BlockSpec((1, tm, d), lambda i, j: (i, jnp.minimum(j + off, n_ctx - 1), 0)),
             pl.BlockSpec((1, tm, d), lambda i, j: (i, jnp.maximum(j + off, n_ctx) - lat_base, 0))]
    return arrays, specs


def _inproj_kernel(sc_ctx_ref, sc_lat_ref, sh_ref, sc_ref, g_ref, w_ref, cos_ref, sin_ref, qg_ref, kg_ref, gsum_ref,
                   rqk_ref, rv_ref, rg_ref, nq_ref, nk_ref, nv_ref, gq_ref, gkv_ref, gate_ref):
    x = _pick_source(sc_ctx_ref, sc_lat_ref, pl.program_id(1))
    h = _rms(x) * g_ref[...]
    h = h * (1.0 + sc_ref[0]) + sh_ref[0]
    hb = h.astype(MXU_DTYPE)
    cos = cos_ref[...]
    sin = sin_ref[...]
    lane = lax.broadcasted_iota(jnp.int32, cos.shape, 1)
    first_half = (lane % 32) < 16
    gsum = gsum_ref[...]

    def proj(col, width):
        return _dot(hb, w_ref[:, col:col + width])

    def rope(v):
        partner = jnp.where(first_half, pltpu.roll(v, LANES - 16, 1), pltpu.roll(v, 16, 1))
        return v * cos + partner * sin

    def head_norm(v, gain):
        sq = v * v
        hi = sq.astype(MXU_DTYPE)
        lo = (sq - hi.astype(F32)).astype(MXU_DTYPE)
        ss = _dot(hi, gsum) + _dot(lo, gsum)
        return (v * lax.rsqrt(ss * (1.0 / HEAD_DIM) + NORM_EPS)) * gain

    odt = rqk_ref.dtype
    ret_scale = HEAD_DIM ** -0.5
    att_scale = HEAD_DIM ** -0.5

    rq = proj(C_RQ, 256)
    rk = proj(C_RK, 256)
    for j in range(2):
        sl = slice(j * LANES, (j + 1) * LANES)
        rqk_ref[0, :, j * LANES:(j + 1) * LANES] = rope(rq[:, sl]).astype(odt)
        rqk_ref[0, :, 256 + j * LANES:256 + (j + 1) * LANES] = (rope(rk[:, sl]) * ret_scale).astype(odt)
    rv_ref[0] = proj(C_RV, HEAD_W).astype(odt)
    rg = proj(C_RG, HEAD_W)
    rg_ref[0] = (rg * jax.nn.sigmoid(rg)).astype(odt)
    nq_ref[0] = (proj(C_NQ, HEAD_W) * att_scale).astype(odt)
    nk_ref[0] = proj(C_NK, HEAD_W).astype(odt)
    nv_ref[0] = proj(C_NV, HEAD_W).astype(odt)

    gq = proj(C_GQ, HEAD_W)
    qg = qg_ref[...]
    for j in range(4):
        sl = slice(j * LANES, (j + 1) * LANES)
        gq_ref[0, :, sl] = (rope(head_norm(gq[:, sl], qg)) * att_scale).astype(odt)
    gk = rope(head_norm(proj(C_GK, LANES), kg_ref[...]))
    gv = proj(C_GV, LANES)
    low_half = lane < HEAD_DIM
    for j, t in enumerate((gk, gv)):
        swapped = pltpu.roll(t, HEAD_DIM, 1)
        gkv_ref[0, :, (2 * j) * LANES:(2 * j + 1) * LANES] = jnp.where(low_half, t, swapped).astype(odt)
        gkv_ref[0, :, (2 * j + 1) * LANES:(2 * j + 2) * LANES] = jnp.where(low_half, swapped, t).astype(odt)

    for j in range(6):
        sl = slice(j * HEAD_W, (j + 1) * HEAD_W)
        gate_ref[0, :, sl] = jax.nn.sigmoid(proj(C_GATE + j * HEAD_W, HEAD_W)).astype(odt)


def _in_projection(sources, mod, layer_gain, w_in, cos_t, sin_t, q_gain, k_gain, gsum):
    b, _, d = sources[-1].shape
    t = cos_t.shape[0]
    tm = TOKEN_TILE
    src_arrays, src_specs = _source_specs(sources, tm, d, 0)
    n_ctx = CTX_LEN // tm
    rows = mod.shape[0]
    mod3 = mod.reshape(rows, 1, 6 * d)

    def mod_spec(chunk):
        return pl.BlockSpec((1, 1, d), lambda i, j: (jnp.where(j < n_ctx, rows - 1, i), 0, chunk))

    def tok(width):
        return pl.BlockSpec((1, tm, width), lambda i, j: (i, j, 0))

    def out(width):
        return jax.ShapeDtypeStruct((b, t, width), MXU_DTYPE)

    return pl.pallas_call(
        _inproj_kernel,
        grid=(b, t // tm),
        in_specs=src_specs + [mod_spec(0), mod_spec(1), _resident((1, d)), _resident((d, IN_W)),
                              pl.BlockSpec((tm, LANES), lambda i, j: (j, 0)),
                              pl.BlockSpec((tm, LANES), lambda i, j: (j, 0)),
                              _resident((1, LANES)), _resident((1, LANES)), _resident((LANES, LANES))],
        out_specs=[tok(HEAD_W)] * 8 + [tok(3 * d)],
        out_shape=[out(HEAD_W)] * 8 + [out(3 * d)],
        compiler_params=_params(2),
        name="in_projection",
    )(*src_arrays, mod3, mod3, layer_gain, w_in, cos_t, sin_t, q_gain, k_gain, gsum)


def _retention_kernel(logit_ref, qk_ref, v_ref, g_ref, o_ref, dec_ref, sf_ref, sb_ref, acc_ref, *, n_tok):
    c_len = RET_CHUNK
    n_chunks = n_tok // c_len
    n_ctx_chunks = CTX_LEN // c_len
    heads = RET_HEADS
    t_mask, t_dkf, t_dkb, t_dsf, t_dsb = (k * heads for k in range(5))
    t_dqf, t_dqb = 5 * heads, 5 * heads + heads // 2

    row = lax.broadcasted_iota(jnp.int32, (c_len, LANES), 0)
    lane = lax.broadcasted_iota(jnp.int32, (c_len, LANES), 1)
    lane_lo = lane < HEAD_DIM

    @pl.when(pl.program_id(0) == 0)
    def _build_tables():
        rowf = row.astype(F32)
        lanef = lane.astype(F32)
        rel = rowf - lanef
        lg = [[jnp.broadcast_to(jax.nn.log_sigmoid(logit_ref[dd, hh])[0:1, :], (c_len, LANES))
               for hh in range(heads)] for dd in range(2)]
        for hh in range(heads):
            lf, lb = lg[0][hh], lg[1][hh]
            fwd = jnp.where(rel >= 0, jnp.exp(lf * jnp.maximum(rel, 0.0)), 0.0)
            bwd = jnp.where(rel < 0, jnp.exp(lb * jnp.maximum(-rel, 0.0)), 0.0)
            dec_ref[t_mask + hh] = fwd + bwd
            dec_ref[t_dkf + hh] = jnp.exp(lf * (c_len - 1.0 - lanef))
            dec_ref[t_dkb + hh] = jnp.exp(lb * lanef)
            dec_ref[t_dsf + hh] = jnp.exp(lf * float(c_len))
            dec_ref[t_dsb + hh] = jnp.exp(lb * float(c_len))
        for p in range(heads // 2):
            lf = jnp.where(lane_lo, lg[0][2 * p], lg[0][2 * p + 1])
            lb = jnp.where(lane_lo, lg[1][2 * p], lg[1][2 * p + 1])
            dec_ref[t_dqf + p] = jnp.exp(lf * (rowf + 1.0))
            dec_ref[t_dqb + p] = jnp.exp(lb * (float(c_len) - rowf))

    sf_ref[...] = jnp.zeros_like(sf_ref)
    sb_ref[...] = jnp.zeros_like(sb_ref)

    def load(c):
        r0 = pl.multiple_of(c * c_len, c_len)
        return r0, pl.ds(r0, c_len)

    def masked_kt(k, half):
        keep = lane_lo if half == 0 else jnp.logical_not(lane_lo)
        return jnp.where(keep, k, 0.0).T

    def fwd_chunk(c, carry):
        _, rows = load(c)
        for p in range(heads // 2):
            q = qk_ref[0, rows, p * LANES:(p + 1) * LANES].astype(F32)
            k = qk_ref[0, rows, 256 + p * LANES:256 + (p + 1) * LANES].astype(F32)
            qf = (q * dec_ref[t_dqf + p]).astype(MXU_DTYPE)
            qb = q.astype(MXU_DTYPE)
            for half in range(2):
                hh = 2 * p + half
                v = v_ref[0, rows, hh * LANES:(hh + 1) * LANES]
                keep = lane_lo if half == 0 else jnp.logical_not(lane_lo)
                km = jnp.where(keep, k, 0.0).astype(MXU_DTYPE)
                a = _dot_nt(qb, km) * dec_ref[t_mask + hh]
                o = _dot(a.astype(MXU_DTYPE), v) + _dot(qf, sf_ref[hh].astype(MXU_DTYPE))
                acc_ref[rows, hh * LANES:(hh + 1) * LANES] = o
                kt = (masked_kt(k, half) * dec_ref[t_dkf + hh]).astype(MXU_DTYPE)
                sf_ref[hh] = sf_ref[hh] * dec_ref[t_dsf + hh] + _dot(kt, v)
        return carry

    def bwd_chunk(c):
        _, rows = load(c)
        for p in range(heads // 2):
            q = qk_ref[0, rows, p * LANES:(p + 1) * LANES].astype(F32)
            k = qk_ref[0, rows, 256 + p * LANES:256 + (p + 1) * LANES].astype(F32)
            qb = (q * dec_ref[t_dqb + p]).astype(MXU_DTYPE)
            for half in range(2):
                hh = 2 * p + half
                sl = slice(hh * LANES, (hh + 1) * LANES)
                v = v_ref[0, rows, sl]
                o = acc_ref[rows, sl] + _dot(qb, sb_ref[hh].astype(MXU_DTYPE))
                y = _rms(o) * g_ref[0, rows, sl].astype(F32)
                o_ref[0, rows, sl] = y.astype(o_ref.dtype)
                kt = (masked_kt(k, half) * dec_ref[t_dkb + hh]).astype(MXU_DTYPE)
                sb_ref[hh] = sb_ref[hh] * dec_ref[t_dsb + hh] + _dot(kt, v)

    lax.fori_loop(0, n_chunks, fwd_chunk, 0)

    def bwd_ctx(i, carry):
        bwd_chunk(n_ctx_chunks - 1 - i)
        return carry

    def bwd_lat(i, carry):
        bwd_chunk(n_chunks - 1 - i)
        return carry

    lax.fori_loop(0, n_ctx_chunks, bwd_ctx, 0)
    lax.fori_loop(0, n_chunks - n_ctx_chunks, bwd_lat, 0)


def _retention(logit_tile, rqk, rv, rg):
    b, t, _ = rqk.shape
    n_tables = 5 * RET_HEADS + RET_HEADS
    blk = pl.BlockSpec((1, t, HEAD_W), lambda i: (i, 0, 0))
    return pl.pallas_call(
        functools.partial(_retention_kernel, n_tok=t),
        grid=(b,),
        in_specs=[_resident(logit_tile.shape), blk, blk, blk],
        out_specs=blk,
        out_shape=jax.ShapeDtypeStruct((b, t, HEAD_W), MXU_DTYPE),
        scratch_shapes=[pltpu.VMEM((n_tables, RET_CHUNK, LANES), F32),
                        pltpu.VMEM((RET_HEADS, LANES, LANES), F32),
                        pltpu.VMEM((RET_HEADS, LANES, LANES), F32),
                        pltpu.VMEM((t, HEAD_W), F32)],
        compiler_params=_params(1),
        name="retention",
    )(logit_tile, rqk, rv, rg)


def _na_kernel(q_ref, k_ref, v_ref, bias_ref, bias_pad_ref, o_ref, p_ref, *, n_tok, with_ctx_out):
    n_rows = (n_tok - CTX_LEN) // GRID_W
    n_blocks = n_rows // NA_BLOCK_ROWS
    qb = NA_BLOCK_ROWS * GRID_W
    un = NA_UNION_ROWS * GRID_W
    win = NA_ROWS * GRID_W
    half_win = NA_ROWS // 2
    cdt = p_ref.dtype

    def stacked(q):
        lo = lax.broadcasted_iota(jnp.int32, q.shape, 1) < HEAD_DIM
        zero = jnp.zeros((), q.dtype)
        return jnp.concatenate([jnp.where(lo, q, zero), jnp.where(lo, zero, q)], axis=0)

    def unstack(o):
        m = o.shape[0] // 2
        lo = lax.broadcasted_iota(jnp.int32, (m, LANES), 1) < HEAD_DIM
        return jnp.where(lo, o[0:m], o[m:])

    def softmax(parts):
        m = parts[0].max(axis=-1, keepdims=True)
        for s in parts[1:]:
            m = jnp.maximum(m, s.max(axis=-1, keepdims=True))
        e = [jnp.exp(s - m) for s in parts]
        den = e[0].sum(axis=-1, keepdims=True)
        for x in e[1:]:
            den = den + x.sum(axis=-1, keepdims=True)
        inv = 1.0 / den
        return [x * inv for x in e]

    def block(g, config):
        if config == "first":
            u0 = 0
        elif config == "last":
            u0 = n_rows - NA_UNION_ROWS
        else:
            u0 = g * NA_BLOCK_ROWS - half_win
        q_rows = pl.ds(pl.multiple_of(CTX_LEN + g * qb, qb), qb)
        k_rows = pl.ds(pl.multiple_of(CTX_LEN + u0 * GRID_W, GRID_W), un)
        for p in range(NA_HEADS // 2):
            sl = slice(p * LANES, (p + 1) * LANES)
            pbuf = p_ref.at[p % 2]
            qs = stacked(q_ref[0, q_rows, sl])
            s_un = _dot_nt(qs, k_ref[0, k_rows, sl])
            s_cx = _dot_nt(qs, k_ref[0, 0:CTX_LEN, sl])
            for half in range(2):
                hh = 2 * p + half
                for a in range(NA_BLOCK_ROWS):
                    rows = slice(half * qb + a * GRID_W, half * qb + (a + 1) * GRID_W)
                    if config == "first":
                        w, off = 0, NA_ROWS - 1 - a
                    elif config == "last":
                        w, off = (n_rows - NA_ROWS - u0) * GRID_W, half_win - 1 - a
                    else:
                        w, off = a * GRID_W, half_win - 1
                    if w % LANES:
                        lo, width, bias = w - GRID_W, win + 2 * GRID_W, bias_pad_ref[hh]
                    else:
                        lo, width, bias = w, win, bias_ref[off, hh]
                    p_loc, p_cx = softmax([s_un[rows, lo:lo + width] + bias, s_cx[rows]])
                    pieces = [p_loc, jnp.zeros((GRID_W, un - lo - width), F32), p_cx]
                    if lo:
                        pieces = [jnp.zeros((GRID_W, lo), F32)] + pieces
                    pbuf[rows, :] = jnp.concatenate([x for x in pieces if x.shape[1]], axis=1).astype(cdt)
            o = _dot(pbuf[:, 0:un], v_ref[0, k_rows, sl]) + _dot(pbuf[:, un:], v_ref[0, 0:CTX_LEN, sl])
            o_ref[0, q_rows, sl] = unstack(o).astype(o_ref.dtype)

    block(0, "first")

    def middle(g, carry):
        block(g, "middle")
        return carry

    lax.fori_loop(1, n_blocks - 1, middle, 0)
    block(n_blocks - 1, "last")

    for p in range(NA_HEADS // 2):
        sl = slice(p * LANES, (p + 1) * LANES)
        if with_ctx_out:
            qs = stacked(q_ref[0, 0:CTX_LEN, sl])
            (pn,) = softmax([_dot_nt(qs, k_ref[0, 0:CTX_LEN, sl])])
            o = _dot(pn.astype(cdt), v_ref[0, 0:CTX_LEN, sl])
            o_ref[0, 0:CTX_LEN, sl] = unstack(o).astype(o_ref.dtype)
        else:
            o_ref[0, 0:CTX_LEN, sl] = jnp.zeros((CTX_LEN, LANES), o_ref.dtype)


def _neighbourhood(nq, nk, nv, bias, bias_pad, with_ctx_out):
    b, t, _ = nq.shape
    blk = pl.BlockSpec((1, t, HEAD_W), lambda i: (i, 0, 0))
    p_cols = NA_UNION_ROWS * GRID_W + CTX_LEN
    return pl.pallas_call(
        functools.partial(_na_kernel, n_tok=t, with_ctx_out=with_ctx_out),
        grid=(b,),
        in_specs=[blk, blk, blk, _resident(bias.shape), _resident(bias_pad.shape)],
        out_specs=blk,
        out_shape=jax.ShapeDtypeStruct((b, t, HEAD_W), MXU_DTYPE),
        scratch_shapes=[pltpu.VMEM((2, 2 * NA_BLOCK_ROWS * GRID_W, p_cols), MXU_DTYPE)],
        compiler_params=_params(1),
        name="neighbourhood_attention",
    )(nq, nk, nv, bias, bias_pad)


def _na_bias_table(rel_bias):
    col = np.arange(GRID_W)
    col_start = np.clip(col - NA_COLS // 2, 0, GRID_W - NA_COLS)
    in_window = (col[None, :] >= col_start[:, None]) & (col[None, :] < col_start[:, None] + NA_COLS)
    dcol = np.clip(col[None, :] - col[:, None], 1 - NA_COLS, NA_COLS - 1) + NA_COLS - 1
    rb = rel_bias.astype(F32)[:, None, :, None, :]
    cb = jnp.full((NA_HEADS, GRID_W, 2 * NA_ROWS - 1, GRID_W), NEG_INF, F32)
    for c in range(2 * NA_COLS - 1):
        hit = jnp.asarray((dcol == c) & in_window)[None, :, None, :]
        cb = jnp.where(hit, rb[..., c], cb)
    tbl = jnp.stack([cb[:, :, o:o + NA_ROWS, :].reshape(NA_HEADS, GRID_W, NA_ROWS * GRID_W) for o in range(NA_ROWS)])
    side = jnp.full((NA_HEADS, GRID_W, GRID_W), NEG_INF, F32)
    padded = jnp.concatenate([side, tbl[NA_ROWS // 2 - 1], side], axis=-1)
    return tbl, padded


def _gqa_kernel(q_ref, k_ref, v_ref, o_ref, *, n_tok, tile_offset):
    tq = q_ref.shape[1]
    n_ctx_tiles = CTX_LEN // tq
    lo = lax.broadcasted_iota(jnp.int32, (tq, LANES), 1) < HEAD_DIM
    zero = jnp.zeros((), q_ref.dtype)
    parts = []
    for j in range(2):
        q = q_ref[0, :, j * LANES:(j + 1) * LANES]
        parts += [jnp.where(lo, q, zero), jnp.where(lo, zero, q)]
    qs = jnp.concatenate(parts, axis=0)

    def run(n_keys):
        m = l = acc = None
        for c0 in range(0, n_keys, GQA_KEY_CHUNK):
            c1 = min(c0 + GQA_KEY_CHUNK, n_keys)
            s = _dot_nt(qs, k_ref[0, c0:c1, :])
            s_max = s.max(axis=-1, keepdims=True)
            if m is None:
                m = s_max
                p = jnp.exp(s - m)
                l = p.sum(axis=-1, keepdims=True)
                acc = _dot(p.astype(v_ref.dtype), v_ref[0, c0:c1, :])
            else:
                m_new = jnp.maximum(m, s_max)
                alpha = jnp.exp(m - m_new)
                p = jnp.exp(s - m_new)
                l = alpha * l + p.sum(axis=-1, keepdims=True)
                acc = alpha * acc + _dot(p.astype(v_ref.dtype), v_ref[0, c0:c1, :])
                m = m_new
        o = acc / l
        for j in range(2):
            o_ref[0, :, j * LANES:(j + 1) * LANES] = jnp.where(
                lo, o[2 * j * tq:(2 * j + 1) * tq], o[(2 * j + 1) * tq:(2 * j + 2) * tq]).astype(o_ref.dtype)

    if tile_offset >= n_ctx_tiles:
        run(n_tok)
    else:
        is_ctx = pl.program_id(2) + tile_offset < n_ctx_tiles
        pl.when(is_ctx)(lambda: run(CTX_LEN))
        pl.when(jnp.logical_not(is_ctx))(lambda: run(n_tok))


def _gqa(gq, gkv, with_ctx_out):
    b, t, _ = gq.shape
    tq = TOKEN_TILE
    off = 0 if with_ctx_out else CTX_LEN // tq
    n_kv = GQA_KV_HEADS
    return pl.pallas_call(
        functools.partial(_gqa_kernel, n_tok=t, tile_offset=off),
        grid=(b, n_kv, t // tq - off),
        in_specs=[pl.BlockSpec((1, tq, 2 * LANES), lambda i, g, j: (i, j + off, g)),
                  pl.BlockSpec((1, t, LANES), lambda i, g, j: (i, 0, g)),
                  pl.BlockSpec((1, t, LANES), lambda i, g, j: (i, 0, n_kv + g))],
        out_specs=pl.BlockSpec((1, tq, 2 * LANES), lambda i, g, j: (i, j, g)),
        out_shape=jax.ShapeDtypeStruct((b, t - off * tq, HEAD_W), MXU_DTYPE),
        compiler_params=_params(3),
        name="gqa_attention",
    )(gq, gkv, gkv)


def _post_kernel(yr_ref, yn_ref, yg_ref, gate_ref, s_ctx_ref, s_lat_ref, gt1_ref, sh2_ref, sc2_ref, gt2_ref,
                 gpost_ref, gpre_ref, gpost2_ref, wbr_ref, wout_ref, w1_ref, w2_ref, o_ref, *, tile_offset):
    d = o_ref.shape[-1]
    resid = _pick_source(s_ctx_ref, s_lat_ref, pl.program_id(1) + tile_offset)
    y = None
    for i, br in enumerate((yr_ref, yn_ref, yg_ref)):
        z = _dot(br[0], wbr_ref[i]) * gate_ref[0, :, i * d:(i + 1) * d].astype(F32)
        y = z if y is None else y + z
    y = _dot(y.astype(MXU_DTYPE), wout_ref[...])
    x = resid + gt1_ref[0] * (_rms(y) * gpost_ref[...])

    h = _rms(x) * gpre_ref[...]
    h = (h * (1.0 + sc2_ref[0]) + sh2_ref[0]).astype(MXU_DTYPE)
    acc = None
    for j in range(D_FF // d):
        u = jnp.maximum(_dot(h, w1_ref[:, j * d:(j + 1) * d]), 0.0)
        part = _dot((u * u).astype(MXU_DTYPE), w2_ref[j * d:(j + 1) * d, :])
        acc = part if acc is None else acc + part
    o_ref[0] = x + gt2_ref[0] * (_rms(acc) * gpost2_ref[...])


def _post(y_ret, y_na, y_gqa, gates, sources, mod, g_post_mix, g_pre_mlp, g_post_mlp,
          w_br, w_out, w1, w2, latent_only):
    b, t, _ = y_ret.shape
    d = sources[-1].shape[-1]
    tm = TOKEN_TILE
    n_ctx = CTX_LEN // tm
    off = n_ctx if latent_only else 0
    src_arrays, src_specs = _source_specs(sources, tm, d, off)
    rows = mod.shape[0]
    mod3 = mod.reshape(rows, 1, 6 * d)

    def mod_spec(chunk):
        return pl.BlockSpec((1, 1, d), lambda i, j: (jnp.where(j + off < n_ctx, rows - 1, i), 0, chunk))

    def tok(width, shift=off):
        return pl.BlockSpec((1, tm, width), lambda i, j: (i, j + shift, 0))

    return pl.pallas_call(
        functools.partial(_post_kernel, tile_offset=off),
        grid=(b, t // tm - off),
        in_specs=[tok(HEAD_W), tok(HEAD_W), tok(HEAD_W, 0), tok(3 * d)] + src_specs + [
            mod_spec(2), mod_spec(3), mod_spec(4), mod_spec(5),
            _resident((1, d)), _resident((1, d)), _resident((1, d)),
            _resident(w_br.shape), _resident(w_out.shape), _resident(w1.shape), _resident(w2.shape)],
        out_specs=pl.BlockSpec((1, tm, d), lambda i, j: (i, j, 0)),
        out_shape=jax.ShapeDtypeStruct((b, t - off * tm, d), F32),
        compiler_params=_params(2),
        name="merge_mlp",
    )(y_ret, y_na, y_gqa, gates, *src_arrays, mod3, mod3, mod3, mod3,
      g_post_mix, g_pre_mlp, g_post_mlp, w_br, w_out, w1, w2)


def _rope_tables(n_latent):
    t = jnp.arange(n_latent, dtype=jnp.int32)
    pos = jnp.stack([t // GRID_W, t % GRID_W], axis=-1).astype(F32)
    n_freq = HEAD_DIM // 4
    inv_freq = ROPE_BASE ** (-jnp.arange(n_freq, dtype=F32) / n_freq)
    lane = np.arange(LANES) % HEAD_DIM
    axis, second, freq = lane // 32, (lane % 32) // 16, lane % 16
    ang = pos[:, axis] * inv_freq[freq][None, :]
    cos = jnp.cos(ang)
    sin = jnp.sin(ang) * jnp.asarray(np.where(second == 1, 1.0, -1.0), F32)
    cos = jnp.concatenate([jnp.ones((CTX_LEN, LANES), F32), cos], axis=0)
    sin = jnp.concatenate([jnp.zeros((CTX_LEN, LANES), F32), sin], axis=0)
    return cos, sin


def kernel(x, c, ctx, c_ctx, w_mod, b_mod, g_pre_mix, g_post_mix, g_pre_mlp, g_post_mlp, w_in, ret_decay_logit, na_rel_bias, gqa_q_norm, gqa_k_norm, w_br_ret, w_br_na, w_br_gqa, w_out, w_mlp_in, w_mlp_out):
    b, n, d = x.shape
    depth = w_mod.shape[0]
    cdt = MXU_DTYPE

    cos_t, sin_t = _rope_tables(n)
    lane_head = np.arange(LANES) // HEAD_DIM
    gsum = jnp.asarray(lane_head[:, None] == lane_head[None, :], cdt)

    n_rows = -(-(b + 1) // 8) * 8
    cs = jnp.concatenate([c, jnp.zeros((n_rows - b - 1, d), F32), c_ctx[None, :]], axis=0)
    mod = _modulation(cs, w_mod, b_mod)

    sources = (ctx, x)
    for l in range(depth):
        last = l == depth - 1
        proj = _in_projection(sources, mod[l], g_pre_mix[l][None, :], w_in[l].astype(cdt), cos_t, sin_t,
                              jnp.tile(gqa_q_norm[l], 2)[None, :], jnp.tile(gqa_k_norm[l], 2)[None, :], gsum)
        rqk, rv, rg, nq, nk, nv, gq, gkv, gates = proj
        logit_tile = jnp.broadcast_to(ret_decay_logit[l].astype(F32)[:, :, None, None], (2, RET_HEADS, 8, LANES))
        y_ret = _retention(logit_tile, rqk, rv, rg)
        na_bias, na_bias_pad = _na_bias_table(na_rel_bias[l])
        y_na = _neighbourhood(nq, nk, nv, na_bias, na_bias_pad, not last)
        y_gqa = _gqa(gq, gkv, not last)
        w_br = jnp.stack([w_br_ret[l], w_br_na[l], w_br_gqa[l]]).astype(cdt)
        stream = _post(y_ret, y_na, y_gqa, gates, sources, mod[l], g_post_mix[l][None, :], g_pre_mlp[l][None, :],
                       g_post_mlp[l][None, :], w_br, w_out[l].astype(cdt), w_mlp_in[l].astype(cdt),
                       w_mlp_out[l].astype(cdt), last)
        sources = (stream,)
    return stream
```

```python
import functools

import jax
import jax.numpy as jnp
import numpy as np
from jax import lax
from jax.experimental import pallas as pl
from jax.experimental.pallas import tpu as pltpu

F32 = jnp.float32
MXU_DTYPE = jnp.bfloat16

D_MODEL = 1024
GRID_W = 64
CTX_LEN = 256
HEAD_DIM = 64
RET_HEADS = 4
RET_CHUNK = 128
NA_HEADS = 8
NA_ROWS = 8
NA_COLS = 16
NA_BLOCK_ROWS = 4
NA_UNION_ROWS = NA_BLOCK_ROWS + NA_ROWS
GQA_Q_HEADS = 8
GQA_KV_HEADS = 2
GQA_KEY_CHUNK = 768
D_FF = 4 * D_MODEL
ROPE_BASE = 10000.0
NORM_EPS = 1e-6
NEG_INF = -1e30
LOG2_E = 1.4426950408889634

LANES = 128
HEAD_W = 512
C_RQ, C_RK, C_RV, C_RG = 0, 256, 512, 1024
C_NQ, C_NK, C_NV = 1536, 2048, 2560
C_GQ, C_GK, C_GV = 3072, 3584, 3712
C_GATE = 3840
IN_W = 6912

TOKEN_TILE = 256
VMEM_LIMIT = 56 * 1024 * 1024


def _rms(x):
    return x * lax.rsqrt(jnp.mean(x * x, axis=-1, keepdims=True) + NORM_EPS)


def _dot(a, b):
    return jnp.dot(a, b, preferred_element_type=F32)


def _dot_nt(a, b):
    return lax.dot_general(a, b, (((1,), (1,)), ((), ())), preferred_element_type=F32)


def _resident(shape):
    return pl.BlockSpec(shape, lambda *_: (0,) * len(shape), pipeline_mode=pl.Buffered(1))


def _layer(arr, l):
    return pl.BlockSpec((None,) + arr.shape[1:], lambda *_: (l,) + (0,) * (arr.ndim - 1),
                        pipeline_mode=pl.Buffered(1))


def _params(n_axes):
    return pltpu.CompilerParams(dimension_semantics=("arbitrary",) * n_axes,
                                vmem_limit_bytes=VMEM_LIMIT)


def _mod_kernel(c_ref, w_ref, b_ref, o_ref):
    c = c_ref[...]
    s = (c * jax.nn.sigmoid(c)).astype(MXU_DTYPE)
    o_ref[0] = _dot(s, w_ref[0].astype(MXU_DTYPE)) + b_ref[0]


def _modulation(cs, w_mod, b_mod):
    depth, d, n = w_mod.shape
    r = cs.shape[0]
    tn = 1024
    return pl.pallas_call(
        _mod_kernel,
        grid=(depth, n // tn),
        in_specs=[pl.BlockSpec((r, d), lambda l, j: (0, 0)),
                  pl.BlockSpec((1, d, tn), lambda l, j: (l, 0, j)),
                  pl.BlockSpec((1, 1, tn), lambda l, j: (l, 0, j))],
        out_specs=pl.BlockSpec((1, r, tn), lambda l, j: (l, 0, j)),
        out_shape=jax.ShapeDtypeStruct((depth, r, n), F32),
        compiler_params=_params(2),
        name="adaln_modulation",
    )(cs, w_mod, b_mod.reshape(depth, 1, n))


def _pick_source(ctx_ref, lat_ref, tile):
    n_ctx = CTX_LEN // ctx_ref.shape[1]
    return jnp.where(tile < n_ctx, ctx_ref[0], lat_ref[0])


def _source_specs(sources, tm, d, off):
    n_ctx = CTX_LEN // tm
    if len(sources) == 1:
        lat_base = 0
        arrays = (sources[0], sources[0])
    else:
        lat_base = n_ctx
        arrays = tuple(sources)
    specs = [pl.BlockSpec((1, tm, d), lambda i, j: (i, jnp.minimum(j + off, n_ctx - 1), 0)),
             pl.BlockSpec((1, tm, d), lambda i, j: (i, jnp.maximum(j + off, n_ctx) - lat_base, 0))]
    return arrays, specs


def _inproj_kernel(sc_ctx_ref, sc_lat_ref, sh_ref, sc_ref, g_ref, w_ref, cos_ref, sin_ref, qg_ref, kg_ref, gsum_ref,
                   rqk_ref, rv_ref, rg_ref, nq_ref, nk_ref, nv_ref, gq_ref, gkv_ref, gate_ref):
    x = _pick_source(sc_ctx_ref, sc_lat_ref, pl.program_id(1))
    h = _rms(x) * g_ref[...]
    h = h * (1.0 + sc_ref[0]) + sh_ref[0]
    hb = h.astype(MXU_DTYPE)
    cos = cos_ref[...]
    sin = sin_ref[...]
    lane = lax.broadcasted_iota(jnp.int32, cos.shape, 1)
    first_half = (lane % 32) < 16
    gsum = gsum_ref[...]

    def proj(col, width):
        return _dot(hb, w_ref[:, col:col + width])

    def rope(v):
        partner = jnp.where(first_half, pltpu.roll(v, LANES - 16, 1), pltpu.roll(v, 16, 1))
        return v * cos + partner * sin

    def head_norm(v, gain):
        sq = v * v
        hi = sq.astype(MXU_DTYPE)
        lo = (sq - hi.astype(F32)).astype(MXU_DTYPE)
        ss = _dot(hi, gsum) + _dot(lo, gsum)
        return (v * lax.rsqrt(ss * (1.0 / HEAD_DIM) + NORM_EPS)) * gain

    odt = rqk_ref.dtype
    ret_scale = HEAD_DIM ** -0.5
    att_scale = HEAD_DIM ** -0.5 * LOG2_E

    rq = proj(C_RQ, 256)
    rk = proj(C_RK, 256)
    for j in range(2):
        sl = slice(j * LANES, (j + 1) * LANES)
        rqk_ref[0, :, j * LANES:(j + 1) * LANES] = rope(rq[:, sl]).astype(odt)
        rqk_ref[0, :, 256 + j * LANES:256 + (j + 1) * LANES] = (rope(rk[:, sl]) * ret_scale).astype(odt)
    rv_ref[0] = proj(C_RV, HEAD_W).astype(odt)
    rg = proj(C_RG, HEAD_W)
    rg_ref[0] = (rg * jax.nn.sigmoid(rg)).astype(odt)
    nq_ref[0] = (proj(C_NQ, HEAD_W) * att_scale).astype(odt)
    nk_ref[0] = proj(C_NK, HEAD_W).astype(odt)
    nv_ref[0] = proj(C_NV, HEAD_W).astype(odt)

    gq = proj(C_GQ, HEAD_W)
    qg = qg_ref[...]
    for j in range(4):
        sl = slice(j * LANES, (j + 1) * LANES)
        gq_ref[0, :, sl] = (rope(head_norm(gq[:, sl], qg)) * att_scale).astype(odt)
    gk = rope(head_norm(proj(C_GK, LANES), kg_ref[...]))
    gv = proj(C_GV, LANES)
    low_half = lane < HEAD_DIM
    for j, t in enumerate((gk, gv)):
        swapped = pltpu.roll(t, HEAD_DIM, 1)
        gkv_ref[0, :, (2 * j) * LANES:(2 * j + 1) * LANES] = jnp.where(low_half, t, swapped).astype(odt)
        gkv_ref[0, :, (2 * j + 1) * LANES:(2 * j + 2) * LANES] = jnp.where(low_half, swapped, t).astype(odt)

    for j in range(6):
        sl = slice(j * HEAD_W, (j + 1) * HEAD_W)
        gate_ref[0, :, sl] = jax.nn.sigmoid(proj(C_GATE + j * HEAD_W, HEAD_W)).astype(odt)


def _mod_spec(l, rows, d, chunk, off):
    n_ctx = CTX_LEN // TOKEN_TILE
    return pl.BlockSpec((1, 1, d), lambda i, j: (l * rows + jnp.where(j + off < n_ctx, rows - 1, i), 0, chunk))


def _in_projection(l, sources, mod3, mod_rows, layer_gain, w_in, cos_t, sin_t, q_gain, k_gain, gsum):
    b, _, d = sources[-1].shape
    t = cos_t.shape[0]
    tm = TOKEN_TILE
    src_arrays, src_specs = _source_specs(sources, tm, d, 0)

    def tok(width):
        return pl.BlockSpec((1, tm, width), lambda i, j: (i, j, 0))

    def out(width):
        return jax.ShapeDtypeStruct((b, t, width), MXU_DTYPE)

    return pl.pallas_call(
        _inproj_kernel,
        grid=(b, t // tm),
        in_specs=src_specs + [_mod_spec(l, mod_rows, d, 0, 0), _mod_spec(l, mod_rows, d, 1, 0),
                              _layer(layer_gain, l), _layer(w_in, l),
                              pl.BlockSpec((tm, LANES), lambda i, j: (j, 0)),
                              pl.BlockSpec((tm, LANES), lambda i, j: (j, 0)),
                              _layer(q_gain, l), _layer(k_gain, l), _resident((LANES, LANES))],
        out_specs=[tok(HEAD_W)] * 8 + [tok(3 * d)],
        out_shape=[out(HEAD_W)] * 8 + [out(3 * d)],
        compiler_params=_params(2),
        name="in_projection",
    )(*src_arrays, mod3, mod3, layer_gain, w_in, cos_t, sin_t, q_gain, k_gain, gsum)


def _retention_kernel(logit_ref, qk_ref, v_ref, g_ref, o_ref, dec_ref, st_ref, *, n_tok):
    c_len = RET_CHUNK
    n_chunks = n_tok // c_len
    n_ctx_chunks = CTX_LEN // c_len
    heads = RET_HEADS
    pairs = heads // 2
    t_mask, t_dkf, t_dkb, t_dsf, t_dsb = (k * heads for k in range(5))
    t_dqf, t_dqb = 5 * heads, 5 * heads + heads // 2
    cdt = qk_ref.dtype

    row = lax.broadcasted_iota(jnp.int32, (c_len, LANES), 0)
    lane = lax.broadcasted_iota(jnp.int32, (c_len, LANES), 1)
    lane_lo = lane < HEAD_DIM
    row_lo = row < HEAD_DIM

    @pl.when(pl.program_id(0) == 0)
    def _build_tables():
        rowf = row.astype(F32)
        lanef = lane.astype(F32)
        rel = rowf - lanef
        lg = [[jnp.broadcast_to(jax.nn.log_sigmoid(logit_ref[dd, hh])[0:1, :], (c_len, LANES))
               for hh in range(heads)] for dd in range(2)]
        for hh in range(heads):
            lf, lb = lg[0][hh], lg[1][hh]
            fwd = jnp.where(rel >= 0, jnp.exp(lf * jnp.maximum(rel, 0.0)), 0.0)
            bwd = jnp.where(rel < 0, jnp.exp(lb * jnp.maximum(-rel, 0.0)), 0.0)
            dec_ref[t_mask + hh] = fwd + bwd
            dec_ref[t_dkf + hh] = jnp.exp(lf * (c_len - 1.0 - lanef))
            dec_ref[t_dkb + hh] = jnp.exp(lb * lanef)
            dec_ref[t_dsf + hh] = jnp.exp(lf * float(c_len))
            dec_ref[t_dsb + hh] = jnp.exp(lb * float(c_len))
        for p in range(heads // 2):
            lf = jnp.where(lane_lo, lg[0][2 * p], lg[0][2 * p + 1])
            lb = jnp.where(lane_lo, lg[1][2 * p], lg[1][2 * p + 1])
            dec_ref[t_dqf + p] = jnp.exp(lf * (rowf + 1.0))
            dec_ref[t_dqb + p] = jnp.exp(lb * (float(c_len) - rowf))

    def chunk_rows(c):
        return pl.ds(pl.multiple_of(c * c_len, c_len), c_len)

    def summaries(c, carry):
        rows = chunk_rows(c)
        for p in range(pairs):
            kt = qk_ref[0, rows, 256 + p * LANES:256 + (p + 1) * LANES].astype(F32).T
            for half in range(2):
                hh = 2 * p + half
                ktm = jnp.where(row_lo if half == 0 else jnp.logical_not(row_lo), kt, 0.0)
                lhs = jnp.concatenate([ktm * dec_ref[t_dkf + hh], ktm * dec_ref[t_dkb + hh]], axis=0)
                st_ref[c, hh] = _dot(lhs.astype(cdt), v_ref[0, rows, hh * LANES:(hh + 1) * LANES])
        return carry

    lax.fori_loop(0, n_chunks, summaries, 0, unroll=6)

    fwd_order = list(range(n_chunks))
    bwd_order = list(range(n_ctx_chunks - 1, -1, -1)) + list(range(n_chunks - 1, n_ctx_chunks - 1, -1))
    for hh in range(heads):
        for order, lo, t_ds in ((fwd_order, 0, t_dsf), (bwd_order, c_len, t_dsb)):
            decay = dec_ref[t_ds + hh]
            state = jnp.zeros((c_len, LANES), F32)
            for c in order:
                z = st_ref[c, hh, lo:lo + c_len, :]
                st_ref[c, hh, lo:lo + c_len, :] = state
                state = state * decay + z

    def outputs(c, carry):
        rows = chunk_rows(c)
        zero = jnp.zeros((), cdt)
        for p in range(pairs):
            q = qk_ref[0, rows, p * LANES:(p + 1) * LANES]
            k = qk_ref[0, rows, 256 + p * LANES:256 + (p + 1) * LANES]
            qs = jnp.concatenate([jnp.where(lane_lo, q, zero), jnp.where(lane_lo, zero, q)], axis=0)
            a = _dot_nt(qs, k)
            q32 = q.astype(F32)
            qf = (q32 * dec_ref[t_dqf + p]).astype(cdt)
            qb = (q32 * dec_ref[t_dqb + p]).astype(cdt)
            for half in range(2):
                hh = 2 * p + half
                sl = slice(hh * LANES, (hh + 1) * LANES)
                pm = (a[half * c_len:(half + 1) * c_len] * dec_ref[t_mask + hh]).astype(cdt)
                lhs = jnp.concatenate([pm, qf, qb], axis=1)
                rhs = jnp.concatenate([v_ref[0, rows, sl], st_ref[c, hh].astype(cdt)], axis=0)
                y = _rms(_dot(lhs, rhs)) * g_ref[0, rows, sl].astype(F32)
                o_ref[0, rows, sl] = y.astype(o_ref.dtype)
        return carry

    lax.fori_loop(0, n_chunks, outputs, 0, unroll=6)


def _retention(l, logit_tile, rqk, rv, rg):
    b, t, _ = rqk.shape
    n_tables = 5 * RET_HEADS + RET_HEADS
    blk = pl.BlockSpec((1, t, HEAD_W), lambda i: (i, 0, 0))
    return pl.pallas_call(
        functools.partial(_retention_kernel, n_tok=t),
        grid=(b,),
        in_specs=[_layer(logit_tile, l), blk, blk, blk],
        out_specs=blk,
        out_shape=jax.ShapeDtypeStruct((b, t, HEAD_W), MXU_DTYPE),
        scratch_shapes=[pltpu.VMEM((n_tables, RET_CHUNK, LANES), F32),
                        pltpu.VMEM((t // RET_CHUNK, RET_HEADS, 2 * RET_CHUNK, LANES), F32)],
        compiler_params=_params(1),
        name="retention",
    )(logit_tile, rqk, rv, rg)


def _na_kernel(q_ref, k_ref, v_ref, bias_ref, bias_shift_ref, bias_pad_ref, o_ref, p_ref, *, n_tok, with_ctx_out):
    n_rows = (n_tok - CTX_LEN) // GRID_W
    n_blocks = n_rows // NA_BLOCK_ROWS
    qb = NA_BLOCK_ROWS * GRID_W
    un = NA_UNION_ROWS * GRID_W
    win = NA_ROWS * GRID_W
    half_win = NA_ROWS // 2
    cdt = p_ref.dtype

    def stacked(q):
        lo = lax.broadcasted_iota(jnp.int32, q.shape, 1) < HEAD_DIM
        zero = jnp.zeros((), q.dtype)
        return jnp.concatenate([jnp.where(lo, q, zero), jnp.where(lo, zero, q)], axis=0)

    def unstack(o):
        m = o.shape[0] // 2
        lo = lax.broadcasted_iota(jnp.int32, (m, LANES), 1) < HEAD_DIM
        return jnp.where(lo, o[0:m], o[m:])

    def softmax(parts):
        m = parts[0].max(axis=-1, keepdims=True)
        for s in parts[1:]:
            m = jnp.maximum(m, s.max(axis=-1, keepdims=True))
        e = [jnp.exp2(s - m) for s in parts]
        den = e[0].sum(axis=-1, keepdims=True)
        for x in e[1:]:
            den = den + x.sum(axis=-1, keepdims=True)
        inv = 1.0 / den
        return [x * inv for x in e]

    def block(g, config):
        if config == "first":
            u0 = 0
        elif config == "last":
            u0 = n_rows - NA_UNION_ROWS
        else:
            u0 = g * NA_BLOCK_ROWS - half_win
        q_rows = pl.ds(pl.multiple_of(CTX_LEN + g * qb, qb), qb)
        k_rows = pl.ds(pl.multiple_of(CTX_LEN + u0 * GRID_W, GRID_W), un)
        for p in range(NA_HEADS // 2):
            sl = slice(p * LANES, (p + 1) * LANES)
            pbuf = p_ref.at[p % 2]
            qs = stacked(q_ref[0, q_rows, sl])
            s_un = _dot_nt(qs, k_ref[0, k_rows, sl])
            s_cx = _dot_nt(qs, k_ref[0, 0:CTX_LEN, sl])
            for half in range(2):
                hh = 2 * p + half
                for a in range(NA_BLOCK_ROWS):
                    rows = slice(half * qb + a * GRID_W, half * qb + (a + 1) * GRID_W)
                    if config == "first":
                        w, off = 0, NA_ROWS - 1 - a
                    elif config == "last":
                        w, off = (n_rows - NA_ROWS - u0) * GRID_W, half_win - 1 - a
                    else:
                        w, off = a * GRID_W, half_win - 1
                    if w % LANES:
                        lo, width, bias = w - GRID_W, win + 2 * GRID_W, bias_pad_ref[hh]
                    elif off % 2:
                        lo, width, bias = w, win, bias_shift_ref[hh, :, (off - 1) * GRID_W:(off - 1) * GRID_W + win]
                    else:
                        lo, width, bias = w, win, bias_ref[hh, :, off * GRID_W:off * GRID_W + win]
                    p_loc, p_cx = softmax([s_un[rows, lo:lo + width] + bias, s_cx[rows]])
                    pieces = [p_loc, jnp.zeros((GRID_W, un - lo - width), F32), p_cx]
                    if lo:
                        pieces = [jnp.zeros((GRID_W, lo), F32)] + pieces
                    pbuf[rows, :] = jnp.concatenate([x for x in pieces if x.shape[1]], axis=1).astype(cdt)
            o = _dot(pbuf[:, 0:un], v_ref[0, k_rows, sl]) + _dot(pbuf[:, un:], v_ref[0, 0:CTX_LEN, sl])
            o_ref[0, q_rows, sl] = unstack(o).astype(o_ref.dtype)

    block(0, "first")

    def middle(g, carry):
        block(g, "middle")
        return carry

    lax.fori_loop(1, n_blocks - 1, middle, 0)
    block(n_blocks - 1, "last")

    for p in range(NA_HEADS // 2):
        sl = slice(p * LANES, (p + 1) * LANES)
        if with_ctx_out:
            qs = stacked(q_ref[0, 0:CTX_LEN, sl])
            (pn,) = softmax([_dot_nt(qs, k_ref[0, 0:CTX_LEN, sl])])
            o = _dot(pn.astype(cdt), v_ref[0, 0:CTX_LEN, sl])
            o_ref[0, 0:CTX_LEN, sl] = unstack(o).astype(o_ref.dtype)
        else:
            o_ref[0, 0:CTX_LEN, sl] = jnp.zeros((CTX_LEN, LANES), o_ref.dtype)


def _neighbourhood(l, nq, nk, nv, bias_tables, with_ctx_out):
    b, t, _ = nq.shape
    blk = pl.BlockSpec((1, t, HEAD_W), lambda i: (i, 0, 0))
    p_cols = NA_UNION_ROWS * GRID_W + CTX_LEN
    return pl.pallas_call(
        functools.partial(_na_kernel, n_tok=t, with_ctx_out=with_ctx_out),
        grid=(b,),
        in_specs=[blk, blk, blk] + [_layer(tbl, l) for tbl in bias_tables],
        out_specs=blk,
        out_shape=jax.ShapeDtypeStruct((b, t, HEAD_W), MXU_DTYPE),
        scratch_shapes=[pltpu.VMEM((2, 2 * NA_BLOCK_ROWS * GRID_W, p_cols), MXU_DTYPE)],
        compiler_params=_params(1),
        name="neighbourhood_attention",
    )(nq, nk, nv, *bias_tables)


def _na_bias_table(rel_bias):
    col = np.arange(GRID_W)
    col_start = np.clip(col - NA_COLS // 2, 0, GRID_W - NA_COLS)
    in_window = (col[None, :] >= col_start[:, None]) & (col[None, :] < col_start[:, None] + NA_COLS)
    dcol = np.clip(col[None, :] - col[:, None], 1 - NA_COLS, NA_COLS - 1) + NA_COLS - 1
    onehot = (dcol[None] == np.arange(2 * NA_COLS - 1)[:, None, None]) & in_window[None]
    rest = rel_bias.astype(F32) * LOG2_E
    cb = None
    for _ in range(3):
        piece = rest.astype(MXU_DTYPE)
        rest = rest - piece.astype(F32)
        part = jnp.einsum("lhdc,cqk->lhqdk", piece, jnp.asarray(onehot, MXU_DTYPE), preferred_element_type=F32)
        cb = part if cb is None else cb + part
    cb = jnp.where(jnp.asarray(in_window)[:, None, :], cb, NEG_INF)
    table = cb.reshape(cb.shape[:3] + ((2 * NA_ROWS - 1) * GRID_W,))
    mid = (NA_ROWS // 2 - 1) * GRID_W
    side = jnp.full(cb.shape[:3] + (GRID_W,), NEG_INF, F32)
    padded = jnp.concatenate([side, table[..., mid:mid + NA_ROWS * GRID_W], side], axis=-1)
    return table, table[..., GRID_W:], padded


def _gqa_kernel(q_ref, k_ref, v_ref, o_ref, *, n_tok, tile_offset):
    tq = q_ref.shape[1]
    n_ctx_tiles = CTX_LEN // tq
    lo = lax.broadcasted_iota(jnp.int32, (tq, LANES), 1) < HEAD_DIM
    zero = jnp.zeros((), q_ref.dtype)
    parts = []
    for j in range(2):
        q = q_ref[0, :, j * LANES:(j + 1) * LANES]
        parts += [jnp.where(lo, q, zero), jnp.where(lo, zero, q)]
    qs = jnp.concatenate(parts, axis=0)

    def run(n_keys):
        m = l = acc = None
        for c0 in range(0, n_keys, GQA_KEY_CHUNK):
            c1 = min(c0 + GQA_KEY_CHUNK, n_keys)
            s = _dot_nt(qs, k_ref[0, c0:c1, :])
            s_max = s.max(axis=-1, keepdims=True)
            if m is None:
                m = s_max
                p = jnp.exp2(s - m)
                l = p.sum(axis=-1, keepdims=True)
                acc = _dot(p.astype(v_ref.dtype), v_ref[0, c0:c1, :])
            else:
                m_new = jnp.maximum(m, s_max)
                alpha = jnp.exp2(m - m_new)
                p = jnp.exp2(s - m_new)
                l = alpha * l + p.sum(axis=-1, keepdims=True)
                acc = alpha * acc + _dot(p.astype(v_ref.dtype), v_ref[0, c0:c1, :])
                m = m_new
        o = acc / l
        for j in range(2):
            o_ref[0, :, j * LANES:(j + 1) * LANES] = jnp.where(
                lo, o[2 * j * tq:(2 * j + 1) * tq], o[(2 * j + 1) * tq:(2 * j + 2) * tq]).astype(o_ref.dtype)

    if tile_offset >= n_ctx_tiles:
        run(n_tok)
    else:
        is_ctx = pl.program_id(2) + tile_offset < n_ctx_tiles
        pl.when(is_ctx)(lambda: run(CTX_LEN))
        pl.when(jnp.logical_not(is_ctx))(lambda: run(n_tok))


def _gqa(gq, gkv, with_ctx_out):
    b, t, _ = gq.shape
    tq = TOKEN_TILE
    off = 0 if with_ctx_out else CTX_LEN // tq
    n_kv = GQA_KV_HEADS
    return pl.pallas_call(
        functools.partial(_gqa_kernel, n_tok=t, tile_offset=off),
        grid=(b, n_kv, t // tq - off),
        in_specs=[pl.BlockSpec((1, tq, 2 * LANES), lambda i, g, j: (i, j + off, g)),
                  pl.BlockSpec((1, t, LANES), lambda i, g, j: (i, 0, g)),
                  pl.BlockSpec((1, t, LANES), lambda i, g, j: (i, 0, n_kv + g))],
        out_specs=pl.BlockSpec((1, tq, 2 * LANES), lambda i, g, j: (i, j, g)),
        out_shape=jax.ShapeDtypeStruct((b, t - off * tq, HEAD_W), MXU_DTYPE),
        compiler_params=_params(3),
        name="gqa_attention",
    )(gq, gkv, gkv)


def _post_kernel(yr_ref, yn_ref, yg_ref, gate_ref, s_ctx_ref, s_lat_ref, gt1_ref, sh2_ref, sc2_ref, gt2_ref,
                 gpost_ref, gpre_ref, gpost2_ref, wr_ref, wn_ref, wg_ref, wout_ref, w1_ref, w2_ref, o_ref,
                 *, tile_offset):
    d = o_ref.shape[-1]
    resid = _pick_source(s_ctx_ref, s_lat_ref, pl.program_id(1) + tile_offset)
    y = None
    for i, (br, w_ref) in enumerate(((yr_ref, wr_ref), (yn_ref, wn_ref), (yg_ref, wg_ref))):
        z = _dot(br[0], w_ref[...]) * gate_ref[0, :, i * d:(i + 1) * d].astype(F32)
        y = z if y is None else y + z
    y = _dot(y.astype(MXU_DTYPE), wout_ref[...])
    x = resid + gt1_ref[0] * (_rms(y) * gpost_ref[...])

    h = _rms(x) * gpre_ref[...]
    h = (h * (1.0 + sc2_ref[0]) + sh2_ref[0]).astype(MXU_DTYPE)
    acc = None
    for j in range(D_FF // d):
        u = jnp.maximum(_dot(h, w1_ref[:, j * d:(j + 1) * d]), 0.0)
        part = _dot((u * u).astype(MXU_DTYPE), w2_ref[j * d:(j + 1) * d, :])
        acc = part if acc is None else acc + part
    o_ref[0] = x + gt2_ref[0] * (_rms(acc) * gpost2_ref[...])


def _post(l, y_ret, y_na, y_gqa, gates, sources, mod3, mod_rows, gains, weights, latent_only):
    b, t, _ = y_ret.shape
    d = sources[-1].shape[-1]
    tm = TOKEN_TILE
    off = CTX_LEN // tm if latent_only else 0
    src_arrays, src_specs = _source_specs(sources, tm, d, off)

    def tok(width, shift=off):
        return pl.BlockSpec((1, tm, width), lambda i, j: (i, j + shift, 0))

    return pl.pallas_call(
        functools.partial(_post_kernel, tile_offset=off),
        grid=(b, t // tm - off),
        in_specs=[tok(HEAD_W), tok(HEAD_W), tok(HEAD_W, 0), tok(3 * d)] + src_specs
        + [_mod_spec(l, mod_rows, d, chunk, off) for chunk in (2, 3, 4, 5)]
        + [_layer(a, l) for a in gains] + [_layer(w, l) for w in weights],
        out_specs=pl.BlockSpec((1, tm, d), lambda i, j: (i, j, 0)),
        out_shape=jax.ShapeDtypeStruct((b, t - off * tm, d), F32),
        compiler_params=_params(2),
        name="merge_mlp",
    )(y_ret, y_na, y_gqa, gates, *src_arrays, mod3, mod3, mod3, mod3, *gains, *weights)


def _rope_tables(n_latent):
    t = np.arange(n_latent)
    pos = np.stack([t // GRID_W, t % GRID_W], axis=-1).astype(np.float64)
    n_freq = HEAD_DIM // 4
    inv_freq = ROPE_BASE ** (-np.arange(n_freq, dtype=np.float64) / n_freq)
    lane = np.arange(LANES) % HEAD_DIM
    axis, second, freq = lane // 32, (lane % 32) // 16, lane % 16
    ang = pos[:, axis] * inv_freq[freq][None, :]
    cos = np.concatenate([np.ones((CTX_LEN, LANES)), np.cos(ang)], axis=0)
    sin = np.concatenate([np.zeros((CTX_LEN, LANES)), np.sin(ang) * np.where(second == 1, 1.0, -1.0)], axis=0)
    return jnp.asarray(cos, F32), jnp.asarray(sin, F32)


def kernel(x, c, ctx, c_ctx, w_mod, b_mod, g_pre_mix, g_post_mix, g_pre_mlp, g_post_mlp, w_in, ret_decay_logit, na_rel_bias, gqa_q_norm, gqa_k_norm, w_br_ret, w_br_na, w_br_gqa, w_out, w_mlp_in, w_mlp_out):
    b, n, d = x.shape
    depth = w_mod.shape[0]
    cdt = MXU_DTYPE

    cos_t, sin_t = _rope_tables(n)
    lane_head = np.arange(LANES) // HEAD_DIM
    gsum = jnp.asarray(lane_head[:, None] == lane_head[None, :], cdt)

    n_rows = -(-(b + 1) // 8) * 8
    cs = jnp.concatenate([c, jnp.zeros((n_rows - b - 1, d), F32), c_ctx[None, :]], axis=0)
    mod3 = _modulation(cs, w_mod, b_mod).reshape(depth * n_rows, 1, 6 * d)

    w_in_c = w_in.astype(cdt)
    post_weights = tuple(w.astype(cdt) for w in (w_br_ret, w_br_na, w_br_gqa, w_out, w_mlp_in, w_mlp_out))
    post_gains = tuple(g[:, None, :] for g in (g_post_mix, g_pre_mlp, g_post_mlp))
    q_gain = jnp.tile(gqa_q_norm, (1, 2))[:, None, :]
    k_gain = jnp.tile(gqa_k_norm, (1, 2))[:, None, :]
    logit_tile = jnp.broadcast_to(ret_decay_logit.astype(F32)[..., None, None], (depth, 2, RET_HEADS, 8, LANES))
    bias_tables = _na_bias_table(na_rel_bias)

    sources = (ctx, x)
    for l in range(depth):
        last = l == depth - 1
        proj = _in_projection(l, sources, mod3, n_rows, g_pre_mix[:, None, :], w_in_c, cos_t, sin_t,
                              q_gain, k_gain, gsum)
        rqk, rv, rg, nq, nk, nv, gq, gkv, gates = proj
        y_ret = _retention(l, logit_tile, rqk, rv, rg)
        y_na = _neighbourhood(l, nq, nk, nv, bias_tables, not last)
        y_gqa = _gqa(gq, gkv, not last)
        stream = _post(l, y_ret, y_na, y_gqa, gates, sources, mod3, n_rows, post_gains, post_weights, last)
        sources = (stream,)
    return stream
```

```python
import functools

import jax
import jax.numpy as jnp
import numpy as np
from jax import lax
from jax.experimental import pallas as pl
from jax.experimental.pallas import tpu as pltpu

F32 = jnp.float32
MXU_DTYPE = jnp.bfloat16

D_MODEL = 1024
GRID_W = 64
CTX_LEN = 256
HEAD_DIM = 64
RET_HEADS = 4
RET_CHUNK = 128
NA_HEADS = 8
NA_ROWS = 8
NA_COLS = 16
NA_BLOCK_ROWS = 4
NA_UNION_ROWS = NA_BLOCK_ROWS + NA_ROWS
GQA_Q_HEADS = 8
GQA_KV_HEADS = 2
GQA_KEY_CHUNK = 768
GQA_ROW_BLOCK = 64
D_FF = 4 * D_MODEL
ROPE_BASE = 10000.0
NORM_EPS = 1e-6
NEG_INF = -1e30
LOG2_E = 1.4426950408889634

LANES = 128
HEAD_W = 512
C_RQ, C_RK, C_RV, C_RG = 0, 256, 512, 1024
C_NQ, C_NK, C_NV = 1536, 2048, 2560
C_GQ, C_GK, C_GV = 3072, 3584, 3712
C_GATE = 3840
IN_W = 6912

TOKEN_TILE = 256
VMEM_LIMIT = 56 * 1024 * 1024


def _rms(x):
    return x * lax.rsqrt(jnp.mean(x * x, axis=-1, keepdims=True) + NORM_EPS)


def _dot(a, b):
    return jnp.dot(a, b, preferred_element_type=F32)


def _dot_nt(a, b):
    return lax.dot_general(a, b, (((1,), (1,)), ((), ())), preferred_element_type=F32)


def _resident(shape):
    return pl.BlockSpec(shape, lambda *_: (0,) * len(shape), pipeline_mode=pl.Buffered(1))


def _layer(arr, l):
    return pl.BlockSpec((None,) + arr.shape[1:], lambda *_: (l,) + (0,) * (arr.ndim - 1),
                        pipeline_mode=pl.Buffered(1))


def _params(n_axes):
    return pltpu.CompilerParams(dimension_semantics=("arbitrary",) * n_axes,
                                vmem_limit_bytes=VMEM_LIMIT)


def _mod_kernel(c_ref, w_ref, b_ref, o_ref):
    c = c_ref[...]
    s = (c * jax.nn.sigmoid(c)).astype(MXU_DTYPE)
    o_ref[0] = _dot(s, w_ref[0].astype(MXU_DTYPE)) + b_ref[0]


def _modulation(cs, w_mod, b_mod):
    depth, d, n = w_mod.shape
    r = cs.shape[0]
    tn = 1024
    return pl.pallas_call(
        _mod_kernel,
        grid=(depth, n // tn),
        in_specs=[pl.BlockSpec((r, d), lambda l, j: (0, 0)),
                  pl.BlockSpec((1, d, tn), lambda l, j: (l, 0, j)),
                  pl.BlockSpec((1, 1, tn), lambda l, j: (l, 0, j))],
        out_specs=pl.BlockSpec((1, r, tn), lambda l, j: (l, 0, j)),
        out_shape=jax.ShapeDtypeStruct((depth, r, n), F32),
        compiler_params=_params(2),
        name="adaln_modulation",
    )(cs, w_mod, b_mod.reshape(depth, 1, n))


def _pick_source(ctx_ref, lat_ref, tile):
    n_ctx = CTX_LEN // ctx_ref.shape[1]
    return jnp.where(tile < n_ctx, ctx_ref[0], lat_ref[0])


def _source_specs(sources, tm, d, off):
    n_ctx = CTX_LEN // tm
    if len(sources) == 1:
        lat_base = 0
        arrays = (sources[0], sources[0])
    else:
        lat_base = n_ctx
        arrays = tuple(sources)
    specs = [pl.BlockSpec((1, tm, d), lambda i, j: (i, jnp.minimum(j + off, n_ctx - 1), 0)),
             pl.BlockSpec((1, tm, d), lambda i, j: (i, jnp.maximum(j + off, n_ctx) - lat_base, 0))]
    return arrays, specs


def _inproj_kernel(sc_ctx_ref, sc_lat_ref, sh_ref, sc_ref, g_ref, w_ref, cos_ref, sin_ref, qg_ref, kg_ref, gsum_ref,
                   rqk_ref, rv_ref, rg_ref, nq_ref, nk_ref, nv_ref, gq_ref, gkv_ref, gate_ref):
    x = _pick_source(sc_ctx_ref, sc_lat_ref, pl.program_id(1))
    h = _rms(x) * g_ref[...]
    h = h * (1.0 + sc_ref[0]) + sh_ref[0]
    hb = h.astype(MXU_DTYPE)
    cos = cos_ref[...]
    sin = sin_ref[...]
    lane = lax.broadcasted_iota(jnp.int32, cos.shape, 1)
    first_half = (lane % 32) < 16
    gsum = gsum_ref[...]

    def proj(col, width):
        return _dot(hb, w_ref[:, col:col + width])

    def rope(v):
        partner = jnp.where(first_half, pltpu.roll(v, LANES - 16, 1), pltpu.roll(v, 16, 1))
        return v * cos + partner * sin

    def head_rms(v):
        ms = _dot((v * v).astype(MXU_DTYPE), gsum) * (1.0 / HEAD_DIM)
        return v * lax.rsqrt(ms + NORM_EPS)

    odt = rqk_ref.dtype
    ret_scale = HEAD_DIM ** -0.5
    att_scale = HEAD_DIM ** -0.5 * LOG2_E

    rq = proj(C_RQ, 256)
    rk = proj(C_RK, 256)
    for j in range(2):
        sl = slice(j * LANES, (j + 1) * LANES)
        rqk_ref[0, :, j * LANES:(j + 1) * LANES] = rope(rq[:, sl]).astype(odt)
        rqk_ref[0, :, 256 + j * LANES:256 + (j + 1) * LANES] = (rope(rk[:, sl]) * ret_scale).astype(odt)
    rv_ref[0] = proj(C_RV, HEAD_W).astype(odt)
    rg = proj(C_RG, HEAD_W)
    rg_ref[0] = (rg * jax.nn.sigmoid(rg)).astype(odt)
    nq_ref[0] = (proj(C_NQ, HEAD_W) * att_scale).astype(odt)
    nk_ref[0] = proj(C_NK, HEAD_W).astype(odt)
    nv = proj(C_NV, HEAD_W).astype(odt)
    for j in range(NA_HEADS // 2):
        nv_ref[0, :, 2 * j * LANES:(2 * j + 1) * LANES] = nv[:, j * LANES:(j + 1) * LANES]
        nv_ref[0, :, (2 * j + 1) * LANES:(2 * j + 2) * LANES] = jnp.ones((nv.shape[0], LANES), odt)

    gq = proj(C_GQ, HEAD_W)
    qg = qg_ref[...]
    for j in range(2):
        normed = head_rms(gq[:, 2 * j * LANES:(2 * j + 2) * LANES])
        for i in range(2):
            sl = slice((2 * j + i) * LANES, (2 * j + i + 1) * LANES)
            gq_ref[0, :, sl] = (rope(normed[:, i * LANES:(i + 1) * LANES] * qg) * att_scale).astype(odt)
    gkv = proj(C_GK, 2 * LANES)
    gk = rope(head_rms(gkv)[:, 0:LANES] * kg_ref[...])
    gv = gkv[:, LANES:]
    low_half = lane < HEAD_DIM
    gk_sw, gv_sw = pltpu.roll(gk, HEAD_DIM, 1), pltpu.roll(gv, HEAD_DIM, 1)
    gkv_ref[0, :, 0:LANES] = jnp.where(low_half, gk, gk_sw).astype(odt)
    gkv_ref[0, :, LANES:2 * LANES] = jnp.where(low_half, gk_sw, gk).astype(odt)
    gkv_ref[0, :, 2 * LANES:3 * LANES] = jnp.where(low_half, gv, 1.0).astype(odt)
    gkv_ref[0, :, 3 * LANES:4 * LANES] = jnp.where(low_half, gv_sw, 1.0).astype(odt)

    for j in range(6):
        sl = slice(j * HEAD_W, (j + 1) * HEAD_W)
        gate_ref[0, :, sl] = jax.nn.sigmoid(proj(C_GATE + j * HEAD_W, HEAD_W)).astype(odt)


def _mod_spec(l, rows, d, chunk, off):
    n_ctx = CTX_LEN // TOKEN_TILE
    return pl.BlockSpec((1, 1, d), lambda i, j: (l * rows + jnp.where(j + off < n_ctx, rows - 1, i), 0, chunk))


def _in_projection(l, sources, mod3, mod_rows, layer_gain, w_in, cos_t, sin_t, q_gain, k_gain, gsum):
    b, _, d = sources[-1].shape
    t = cos_t.shape[0]
    tm = TOKEN_TILE
    src_arrays, src_specs = _source_specs(sources, tm, d, 0)

    def tok(width):
        return pl.BlockSpec((1, tm, width), lambda i, j: (i, j, 0))

    def out(width):
        return jax.ShapeDtypeStruct((b, t, width), MXU_DTYPE)

    out_widths = [HEAD_W] * 5 + [2 * HEAD_W] + [HEAD_W] * 2 + [3 * d]

    return pl.pallas_call(
        _inproj_kernel,
        grid=(b, t // tm),
        in_specs=src_specs + [_mod_spec(l, mod_rows, d, 0, 0), _mod_spec(l, mod_rows, d, 1, 0),
                              _layer(layer_gain, l), _layer(w_in, l),
                              pl.BlockSpec((tm, LANES), lambda i, j: (j, 0)),
                              pl.BlockSpec((tm, LANES), lambda i, j: (j, 0)),
                              _layer(q_gain, l), _layer(k_gain, l), _resident(gsum.shape)],
        out_specs=[tok(w) for w in out_widths],
        out_shape=[out(w) for w in out_widths],
        compiler_params=_params(2),
        name="in_projection",
    )(*src_arrays, mod3, mod3, layer_gain, w_in, cos_t, sin_t, q_gain, k_gain, gsum)


def _retention_kernel(logit_ref, qk_ref, v_ref, g_ref, o_ref, dec_ref, st_ref, *, n_tok):
    c_len = RET_CHUNK
    n_chunks = n_tok // c_len
    n_ctx_chunks = CTX_LEN // c_len
    heads = RET_HEADS
    pairs = heads // 2
    t_mask, t_dkf, t_dkb, t_dsf, t_dsb = (k * heads for k in range(5))
    t_dqf, t_dqb = 5 * heads, 5 * heads + heads // 2
    cdt = qk_ref.dtype

    row = lax.broadcasted_iota(jnp.int32, (c_len, LANES), 0)
    lane = lax.broadcasted_iota(jnp.int32, (c_len, LANES), 1)
    lane_lo = lane < HEAD_DIM
    row_lo = row < HEAD_DIM

    @pl.when(pl.program_id(0) == 0)
    def _build_tables():
        rowf = row.astype(F32)
        lanef = lane.astype(F32)
        rel = rowf - lanef
        lg = [[jnp.broadcast_to(jax.nn.log_sigmoid(logit_ref[dd, hh])[0:1, :], (c_len, LANES))
               for hh in range(heads)] for dd in range(2)]
        for hh in range(heads):
            lf, lb = lg[0][hh], lg[1][hh]
            fwd = jnp.where(rel >= 0, jnp.exp(lf * jnp.maximum(rel, 0.0)), 0.0)
            bwd = jnp.where(rel < 0, jnp.exp(lb * jnp.maximum(-rel, 0.0)), 0.0)
            dec_ref[t_mask + hh] = fwd + bwd
            dec_ref[t_dkf + hh] = jnp.exp(lf * (c_len - 1.0 - lanef))
            dec_ref[t_dkb + hh] = jnp.exp(lb * lanef)
            dec_ref[t_dsf + hh] = jnp.exp(lf * float(c_len))
            dec_ref[t_dsb + hh] = jnp.exp(lb * float(c_len))
        for p in range(heads // 2):
            lf = jnp.where(lane_lo, lg[0][2 * p], lg[0][2 * p + 1])
            lb = jnp.where(lane_lo, lg[1][2 * p], lg[1][2 * p + 1])
            dec_ref[t_dqf + p] = jnp.exp(lf * (rowf + 1.0))
            dec_ref[t_dqb + p] = jnp.exp(lb * (float(c_len) - rowf))

    def chunk_rows(c):
        return pl.ds(pl.multiple_of(c * c_len, c_len), c_len)

    def summaries(c, carry):
        rows = chunk_rows(c)
        for p in range(pairs):
            kt = qk_ref[0, rows, 256 + p * LANES:256 + (p + 1) * LANES].astype(F32).T
            for half in range(2):
                hh = 2 * p + half
                ktm = jnp.where(row_lo if half == 0 else jnp.logical_not(row_lo), kt, 0.0)
                lhs = jnp.concatenate([ktm * dec_ref[t_dkf + hh], ktm * dec_ref[t_dkb + hh]], axis=0)
                st_ref[c, hh] = _dot(lhs.astype(cdt), v_ref[0, rows, hh * LANES:(hh + 1) * LANES])
        return carry

    lax.fori_loop(0, n_chunks, summaries, 0, unroll=6)

    fwd_order = list(range(n_chunks))
    bwd_order = list(range(n_ctx_chunks - 1, -1, -1)) + list(range(n_chunks - 1, n_ctx_chunks - 1, -1))
    for hh in range(heads):
        for order, lo, t_ds in ((fwd_order, 0, t_dsf), (bwd_order, c_len, t_dsb)):
            decay = dec_ref[t_ds + hh]
            state = jnp.zeros((c_len, LANES), F32)
            for c in order:
                z = st_ref[c, hh, lo:lo + c_len, :]
                st_ref[c, hh, lo:lo + c_len, :] = state
                state = state * decay + z

    def outputs(c, carry):
        rows = chunk_rows(c)
        zero = jnp.zeros((), cdt)
        for p in range(pairs):
            q = qk_ref[0, rows, p * LANES:(p + 1) * LANES]
            k = qk_ref[0, rows, 256 + p * LANES:256 + (p + 1) * LANES]
            qs = jnp.concatenate([jnp.where(lane_lo, q, zero), jnp.where(lane_lo, zero, q)], axis=0)
            a = _dot_nt(qs, k)
            q32 = q.astype(F32)
            qf = (q32 * dec_ref[t_dqf + p]).astype(cdt)
            qb = (q32 * dec_ref[t_dqb + p]).astype(cdt)
            for half in range(2):
                hh = 2 * p + half
                sl = slice(hh * LANES, (hh + 1) * LANES)
                pm = (a[half * c_len:(half + 1) * c_len] * dec_ref[t_mask + hh]).astype(cdt)
                lhs = jnp.concatenate([pm, qf, qb], axis=1)
                rhs = jnp.concatenate([v_ref[0, rows, sl], st_ref[c, hh].astype(cdt)], axis=0)
                y = _rms(_dot(lhs, rhs)) * g_ref[0, rows, sl].astype(F32)
                o_ref[0, rows, sl] = y.astype(o_ref.dtype)
        return carry

    lax.fori_loop(0, n_chunks, outputs, 0, unroll=6)


def _retention(l, logit_tile, rqk, rv, rg):
    b, t, _ = rqk.shape
    n_tables = 5 * RET_HEADS + RET_HEADS
    blk = pl.BlockSpec((1, t, HEAD_W), lambda i: (i, 0, 0))
    return pl.pallas_call(
        functools.partial(_retention_kernel, n_tok=t),
        grid=(b,),
        in_specs=[_layer(logit_tile, l), blk, blk, blk],
        out_specs=blk,
        out_shape=jax.ShapeDtypeStruct((b, t, HEAD_W), MXU_DTYPE),
        scratch_shapes=[pltpu.VMEM((n_tables, RET_CHUNK, LANES), F32),
                        pltpu.VMEM((t // RET_CHUNK, RET_HEADS, 2 * RET_CHUNK, LANES), F32)],
        compiler_params=_params(1),
        name="retention",
    )(logit_tile, rqk, rv, rg)


def _na_kernel(q_ref, k_ref, v_ref, bias_ref, bias_shift_ref, bias_pad_ref, o_ref, s_ref, p_ref, *, n_tok, with_ctx_out):
    n_rows = (n_tok - CTX_LEN) // GRID_W
    n_blocks = n_rows // NA_BLOCK_ROWS
    qb = NA_BLOCK_ROWS * GRID_W
    un = NA_UNION_ROWS * GRID_W
    win = NA_ROWS * GRID_W
    half_win = NA_ROWS // 2
    cdt = p_ref.dtype

    def stacked(q):
        lo = lax.broadcasted_iota(jnp.int32, q.shape, 1) < HEAD_DIM
        zero = jnp.zeros((), q.dtype)
        return jnp.concatenate([jnp.where(lo, q, zero), jnp.where(lo, zero, q)], axis=0)

    def unstack(o):
        m = o.shape[0] // 2
        lo = lax.broadcasted_iota(jnp.int32, (m, LANES), 1) < HEAD_DIM
        o = o[:, 0:LANES] / o[:, LANES:]
        return jnp.where(lo, o[0:m], o[m:])

    def softmax(parts):
        m = parts[0].max(axis=-1, keepdims=True)
        for s in parts[1:]:
            m = jnp.maximum(m, s.max(axis=-1, keepdims=True))
        return [jnp.exp2(s - m) for s in parts]

    def values(rows, p):
        return v_ref[0, rows, 2 * p * LANES:(2 * p + 2) * LANES]

    def block(g, config):
        if config == "first":
            u0 = 0
        elif config == "last":
            u0 = n_rows - NA_UNION_ROWS
        else:
            u0 = g * NA_BLOCK_ROWS - half_win
        q_rows = pl.ds(pl.multiple_of(CTX_LEN + g * qb, qb), qb)
        k_rows = pl.ds(pl.multiple_of(CTX_LEN + u0 * GRID_W, GRID_W), un)
        def scores(p):
            sl = slice(p * LANES, (p + 1) * LANES)
            qs = stacked(q_ref[0, q_rows, sl])
            s_ref[p % 2, :, 0:un] = _dot_nt(qs, k_ref[0, k_rows, sl])
            s_ref[p % 2, :, un:] = _dot_nt(qs, k_ref[0, 0:CTX_LEN, sl])

        scores(0)
        for p in range(NA_HEADS // 2):
            sl = slice(p * LANES, (p + 1) * LANES)
            pbuf, sbuf = p_ref.at[p % 2], s_ref.at[p % 2]
            if p + 1 < NA_HEADS // 2:
                scores(p + 1)
            for half in range(2):
                hh = 2 * p + half
                for a in range(NA_BLOCK_ROWS):
                    rows = slice(half * qb + a * GRID_W, half * qb + (a + 1) * GRID_W)
                    if config == "first":
                        w, off = 0, NA_ROWS - 1 - a
                    elif config == "last":
                        w, off = (n_rows - NA_ROWS - u0) * GRID_W, half_win - 1 - a
                    else:
                        w, off = a * GRID_W, half_win - 1
                    if w % LANES:
                        lo, width, bias = w - GRID_W, win + 2 * GRID_W, bias_pad_ref[hh]
                    elif off % 2:
                        lo, width, bias = w, win, bias_shift_ref[hh, :, (off - 1) * GRID_W:(off - 1) * GRID_W + win]
                    else:
                        lo, width, bias = w, win, bias_ref[hh, :, off * GRID_W:off * GRID_W + win]
                    p_loc, p_cx = softmax([sbuf[rows, lo:lo + width] + bias, sbuf[rows, un:]])
                    pieces = [p_loc, jnp.zeros((GRID_W, un - lo - width), F32), p_cx]
                    if lo:
                        pieces = [jnp.zeros((GRID_W, lo), F32)] + pieces
                    pbuf[rows, :] = jnp.concatenate([x for x in pieces if x.shape[1]], axis=1).astype(cdt)
            o = _dot(pbuf[:, 0:un], values(k_rows, p)) + _dot(pbuf[:, un:], values(slice(0, CTX_LEN), p))
            o_ref[0, q_rows, sl] = unstack(o).astype(o_ref.dtype)

    block(0, "first")

    def middle(g, carry):
        block(g, "middle")
        return carry

    lax.fori_loop(1, n_blocks - 1, middle, 0)
    block(n_blocks - 1, "last")

    for p in range(NA_HEADS // 2):
        sl = slice(p * LANES, (p + 1) * LANES)
        if with_ctx_out:
            qs = stacked(q_ref[0, 0:CTX_LEN, sl])
            (pn,) = softmax([_dot_nt(qs, k_ref[0, 0:CTX_LEN, sl])])
            o = _dot(pn.astype(cdt), values(slice(0, CTX_LEN), p))
            o_ref[0, 0:CTX_LEN, sl] = unstack(o).astype(o_ref.dtype)
        else:
            o_ref[0, 0:CTX_LEN, sl] = jnp.zeros((CTX_LEN, LANES), o_ref.dtype)


def _neighbourhood(l, nq, nk, nv, bias_tables, with_ctx_out):
    b, t, _ = nq.shape
    blk = pl.BlockSpec((1, t, HEAD_W), lambda i: (i, 0, 0))
    p_cols = NA_UNION_ROWS * GRID_W + CTX_LEN
    return pl.pallas_call(
        functools.partial(_na_kernel, n_tok=t, with_ctx_out=with_ctx_out),
        grid=(b,),
        in_specs=[blk, blk, pl.BlockSpec((1, t, nv.shape[-1]), lambda i: (i, 0, 0))]
        + [_layer(tbl, l) for tbl in bias_tables],
        out_specs=blk,
        out_shape=jax.ShapeDtypeStruct((b, t, HEAD_W), MXU_DTYPE),
        scratch_shapes=[pltpu.VMEM((2, 2 * NA_BLOCK_ROWS * GRID_W, p_cols), F32),
                        pltpu.VMEM((2, 2 * NA_BLOCK_ROWS * GRID_W, p_cols), MXU_DTYPE)],
        compiler_params=_params(1),
        name="neighbourhood_attention",
    )(nq, nk, nv, *bias_tables)


def _na_bias_table(rel_bias):
    col = np.arange(GRID_W)
    col_start = np.clip(col - NA_COLS // 2, 0, GRID_W - NA_COLS)
    in_window = (col[None, :] >= col_start[:, None]) & (col[None, :] < col_start[:, None] + NA_COLS)
    dcol = np.clip(col[None, :] - col[:, None], 1 - NA_COLS, NA_COLS - 1) + NA_COLS - 1
    onehot = (dcol[None] == np.arange(2 * NA_COLS - 1)[:, None, None]) & in_window[None]
    rest = rel_bias.astype(F32) * LOG2_E
    cb = None
    for _ in range(3):
        piece = rest.astype(MXU_DTYPE)
        rest = rest - piece.astype(F32)
        part = jnp.einsum("lhdc,cqk->lhqdk", piece, jnp.asarray(onehot, MXU_DTYPE), preferred_element_type=F32)
        cb = part if cb is None else cb + part
    cb = jnp.where(jnp.asarray(in_window)[:, None, :], cb, NEG_INF)
    table = cb.reshape(cb.shape[:3] + ((2 * NA_ROWS - 1) * GRID_W,))
    mid = (NA_ROWS // 2 - 1) * GRID_W
    side = jnp.full(cb.shape[:3] + (GRID_W,), NEG_INF, F32)
    padded = jnp.concatenate([side, table[..., mid:mid + NA_ROWS * GRID_W], side], axis=-1)
    return table, table[..., GRID_W:], padded


def _gqa_kernel(q_ref, kv_ref, o_ref, s_ref, p_ref, *, n_tok, tile_offset):
    tq = q_ref.shape[1]
    n_ctx_tiles = CTX_LEN // tq
    n_kv = GQA_KV_HEADS
    rb = GQA_ROW_BLOCK
    n_blk = GQA_Q_HEADS // n_kv * tq // rb
    per_head = tq // rb
    lo = lax.broadcasted_iota(jnp.int32, (tq, LANES), 1) < HEAD_DIM
    zero = jnp.zeros((), q_ref.dtype)

    def stacked_queries(g):
        parts = []
        for j in range(2 * g, 2 * g + 2):
            q = q_ref[0, :, j * LANES:(j + 1) * LANES]
            parts += [jnp.where(lo, q, zero), jnp.where(lo, zero, q)]
        return jnp.concatenate(parts, axis=0)

    def run(n_keys):
        chunks = [(c0, min(c0 + GQA_KEY_CHUNK, n_keys)) for c0 in range(0, n_keys, GQA_KEY_CHUNK)]
        items = [(g, ci) for g in range(n_kv) for ci in range(len(chunks))]
        qs = [stacked_queries(g) for g in range(n_kv)]

        def scores(idx):
            g, ci = items[idx]
            c0, c1 = chunks[ci]
            s_ref[idx % 2, :, 0:c1 - c0] = _dot_nt(qs[g], kv_ref[0, c0:c1, g * LANES:(g + 1) * LANES])

        scores(0)
        m = acc = None
        for idx, (g, ci) in enumerate(items):
            (c0, c1), slot = chunks[ci], idx % 2
            w = c1 - c0
            if idx + 1 < len(items):
                scores(idx + 1)
            if ci == 0:
                m, acc = [None] * n_blk, [None] * n_blk
            alpha = [None] * n_blk
            for i in range(n_blk):
                rows = slice(i * rb, (i + 1) * rb)
                s = s_ref[slot, rows, 0:w]
                s_max = s.max(axis=-1, keepdims=True)
                m_new = s_max if ci == 0 else jnp.maximum(m[i], s_max)
                if ci:
                    alpha[i] = jnp.exp2(m[i] - m_new)
                m[i] = m_new
                p_ref[slot, rows, 0:w] = jnp.exp2(s - m_new).astype(p_ref.dtype)
            pv = _dot(p_ref[slot, :, 0:w], kv_ref[0, c0:c1, (n_kv + g) * LANES:(n_kv + g + 1) * LANES])
            for i in range(n_blk):
                part = pv[i * rb:(i + 1) * rb]
                acc[i] = part if ci == 0 else alpha[i] * acc[i] + part
            if ci + 1 == len(chunks):
                for j in range(2):
                    halves = []
                    for half in range(2):
                        h = 2 * j + half
                        a = jnp.concatenate(acc[h * per_head:(h + 1) * per_head], axis=0)
                        a_sw = pltpu.roll(a, HEAD_DIM, 1)
                        halves.append(a / a_sw if half == 0 else a_sw / a)
                    pair = 2 * g + j
                    o_ref[0, :, pair * LANES:(pair + 1) * LANES] = jnp.where(lo, halves[0], halves[1]).astype(o_ref.dtype)

    if tile_offset >= n_ctx_tiles:
        run(n_tok)
    else:
        is_ctx = pl.program_id(1) + tile_offset < n_ctx_tiles
        pl.when(is_ctx)(lambda: run(CTX_LEN))
        pl.when(jnp.logical_not(is_ctx))(lambda: run(n_tok))


def _gqa(gq, gkv, with_ctx_out):
    b, t, _ = gq.shape
    tq = TOKEN_TILE
    off = 0 if with_ctx_out else CTX_LEN // tq
    stacked_rows = GQA_Q_HEADS // GQA_KV_HEADS * tq
    return pl.pallas_call(
        functools.partial(_gqa_kernel, n_tok=t, tile_offset=off),
        grid=(b, t // tq - off),
        in_specs=[pl.BlockSpec((1, tq, HEAD_W), lambda i, j: (i, j + off, 0)),
                  pl.BlockSpec((1, t, HEAD_W), lambda i, j: (i, 0, 0))],
        out_specs=pl.BlockSpec((1, tq, HEAD_W), lambda i, j: (i, j, 0)),
        out_shape=jax.ShapeDtypeStruct((b, t - off * tq, HEAD_W), MXU_DTYPE),
        scratch_shapes=[pltpu.VMEM((2, stacked_rows, GQA_KEY_CHUNK), F32),
                        pltpu.VMEM((2, stacked_rows, GQA_KEY_CHUNK), MXU_DTYPE)],
        compiler_params=_params(2),
        name="gqa_attention",
    )(gq, gkv)


def _post_kernel(yr_ref, yn_ref, yg_ref, gate_ref, s_ctx_ref, s_lat_ref, gt1_ref, sh2_ref, sc2_ref, gt2_ref,
                 gpost_ref, gpre_ref, gpost2_ref, wr_ref, wn_ref, wg_ref, wout_ref, w1_ref, w2_ref, o_ref,
                 *, tile_offset):
    d = o_ref.shape[-1]
    resid = _pick_source(s_ctx_ref, s_lat_ref, pl.program_id(1) + tile_offset)
    y = None
    for i, (br, w_ref) in enumerate(((yr_ref, wr_ref), (yn_ref, wn_ref), (yg_ref, wg_ref))):
        z = _dot(br[0], w_ref[...]) * gate_ref[0, :, i * d:(i + 1) * d].astype(F32)
        y = z if y is None else y + z
    y = _dot(y.astype(MXU_DTYPE), wout_ref[...])
    x = resid + gt1_ref[0] * (_rms(y) * gpost_ref[...])

    h = _rms(x) * gpre_ref[...]
    h = (h * (1.0 + sc2_ref[0]) + sh2_ref[0]).astype(MXU_DTYPE)
    acc = None
    for j in range(D_FF // d):
        u = jnp.maximum(_dot(h, w1_ref[:, j * d:(j + 1) * d]), 0.0)
        part = _dot((u * u).astype(MXU_DTYPE), w2_ref[j * d:(j + 1) * d, :])
        acc = part if acc is None else acc + part
    o_ref[0] = x + gt2_ref[0] * (_rms(acc) * gpost2_ref[...])


def _post(l, y_ret, y_na, y_gqa, gates, sources, mod3, mod_rows, gains, weights, latent_only):
    b, t, _ = y_ret.shape
    d = sources[-1].shape[-1]
    tm = TOKEN_TILE
    off = CTX_LEN // tm if latent_only else 0
    src_arrays, src_specs = _source_specs(sources, tm, d, off)

    def tok(width, shift=off):
        return pl.BlockSpec((1, tm, width), lambda i, j: (i, j + shift, 0))

    return pl.pallas_call(
        functools.partial(_post_kernel, tile_offset=off),
        grid=(b, t // tm - off),
        in_specs=[tok(HEAD_W), tok(HEAD_W), tok(HEAD_W, 0), tok(3 * d)] + src_specs
        + [_mod_spec(l, mod_rows, d, chunk, off) for chunk in (2, 3, 4, 5)]
        + [_layer(a, l) for a in gains] + [_layer(w, l) for w in weights],
        out_specs=pl.BlockSpec((1, tm, d), lambda i, j: (i, j, 0)),
        out_shape=jax.ShapeDtypeStruct((b, t - off * tm, d), F32),
        compiler_params=_params(2),
        name="merge_mlp",
    )(y_ret, y_na, y_gqa, gates, *src_arrays, mod3, mod3, mod3, mod3, *gains, *weights)


def _rope_tables(n_latent):
    t = np.arange(n_latent)
    pos = np.stack([t // GRID_W, t % GRID_W], axis=-1).astype(np.float64)
    n_freq = HEAD_DIM // 4
    inv_freq = ROPE_BASE ** (-np.arange(n_freq, dtype=np.float64) / n_freq)
    lane = np.arange(LANES) % HEAD_DIM
    axis, second, freq = lane // 32, (lane % 32) // 16, lane % 16
    ang = pos[:, axis] * inv_freq[freq][None, :]
    cos = np.concatenate([np.ones((CTX_LEN, LANES)), np.cos(ang)], axis=0)
    sin = np.concatenate([np.zeros((CTX_LEN, LANES)), np.sin(ang) * np.where(second == 1, 1.0, -1.0)], axis=0)
    return jnp.asarray(cos, F32), jnp.asarray(sin, F32)


def kernel(x, c, ctx, c_ctx, w_mod, b_mod, g_pre_mix, g_post_mix, g_pre_mlp, g_post_mlp, w_in, ret_decay_logit, na_rel_bias, gqa_q_norm, gqa_k_norm, w_br_ret, w_br_na, w_br_gqa, w_out, w_mlp_in, w_mlp_out):
    b, n, d = x.shape
    depth = w_mod.shape[0]
    cdt = MXU_DTYPE

    cos_t, sin_t = _rope_tables(n)
    lane_head = np.arange(2 * LANES) // HEAD_DIM
    gsum = jnp.asarray(lane_head[:, None] == lane_head[None, :], cdt)

    n_rows = -(-(b + 1) // 8) * 8
    cs = jnp.concatenate([c, jnp.zeros((n_rows - b - 1, d), F32), c_ctx[None, :]], axis=0)
    mod3 = _modulation(cs, w_mod, b_mod).reshape(depth * n_rows, 1, 6 * d)

    w_in_c = w_in.astype(cdt)
    post_weights = tuple(w.astype(cdt) for w in (w_br_ret, w_br_na, w_br_gqa, w_out, w_mlp_in, w_mlp_out))
    post_gains = tuple(g[:, None, :] for g in (g_post_mix, g_pre_mlp, g_post_mlp))
    q_gain = jnp.tile(gqa_q_norm, (1, 2))[:, None, :]
    k_gain = jnp.tile(gqa_k_norm, (1, 2))[:, None, :]
    logit_tile = jnp.broadcast_to(ret_decay_logit.astype(F32)[..., None, None], (depth, 2, RET_HEADS, 8, LANES))
    bias_tables = _na_bias_table(na_rel_bias)

    sources = (ctx, x)
    for l in range(depth):
        last = l == depth - 1
        proj = _in_projection(l, sources, mod3, n_rows, g_pre_mix[:, None, :], w_in_c, cos_t, sin_t,
                              q_gain, k_gain, gsum)
        rqk, rv, rg, nq, nk, nv, gq, gkv, gates = proj
        y_ret = _retention(l, logit_tile, rqk, rv, rg)
        y_na = _neighbourhood(l, nq, nk, nv, bias_tables, not last)
        y_gqa = _gqa(gq, gkv, not last)
        stream = _post(l, y_ret, y_na, y_gqa, gates, sources, mod3, n_rows, post_gains, post_weights, last)
        sources = (stream,)
    return stream
```

```python
import functools

import jax
import jax.numpy as jnp
import numpy as np
from jax import lax
from jax.experimental import pallas as pl
from jax.experimental.pallas import tpu as pltpu

F32 = jnp.float32
MXU_DTYPE = jnp.bfloat16

D_MODEL = 1024
GRID_W = 64
CTX_LEN = 256
HEAD_DIM = 64
RET_HEADS = 4
RET_CHUNK = 128
NA_HEADS = 8
NA_ROWS = 8
NA_COLS = 16
NA_BLOCK_ROWS = 4
NA_UNION_ROWS = NA_BLOCK_ROWS + NA_ROWS
GQA_Q_HEADS = 8
GQA_KV_HEADS = 2
GQA_Q_TILE = 256
GQA_KEY_CHUNK = 768
GQA_ROW_BLOCK = 64
D_FF = 4 * D_MODEL
ROPE_BASE = 10000.0
NORM_EPS = 1e-6
NEG_INF = -1e30
LOG2_E = 1.4426950408889634

LANES = 128
HEAD_W = 512
C_RQ, C_RK, C_RV, C_RG = 0, 256, 512, 1024
C_NQ, C_NK, C_NV = 1536, 2048, 2560
C_GQ, C_GK, C_GV = 3072, 3584, 3712
C_GATE = 3840
IN_W = 6912

TOKEN_TILE = 512
POST_ROW_GROUP = 256
VMEM_LIMIT = 56 * 1024 * 1024


def _rms(x):
    return x * lax.rsqrt(jnp.mean(x * x, axis=-1, keepdims=True) + NORM_EPS)


def _dot(a, b):
    return jnp.dot(a, b, preferred_element_type=F32)


def _dot_nt(a, b):
    return lax.dot_general(a, b, (((1,), (1,)), ((), ())), preferred_element_type=F32)


def _resident(shape):
    return pl.BlockSpec(shape, lambda *_: (0,) * len(shape), pipeline_mode=pl.Buffered(1))


def _layer(arr, l):
    return pl.BlockSpec((None,) + arr.shape[1:], lambda *_: (l,) + (0,) * (arr.ndim - 1),
                        pipeline_mode=pl.Buffered(1))


def _batch_block(arr):
    return pl.BlockSpec((1,) + arr.shape[1:], lambda i, *_: (i, 0, 0))


def _params(n_axes):
    return pltpu.CompilerParams(dimension_semantics=("arbitrary",) * n_axes,
                                vmem_limit_bytes=VMEM_LIMIT)


def _mod_kernel(c_ref, w_ref, b_ref, o_ref):
    c = c_ref[...]
    s = (c * jax.nn.sigmoid(c)).astype(MXU_DTYPE)
    o_ref[0] = _dot(s, w_ref[0].astype(MXU_DTYPE)) + b_ref[0]


def _modulation(cs, w_mod, b_mod):
    depth, d, n = w_mod.shape
    r = cs.shape[0]
    tn = 1024
    return pl.pallas_call(
        _mod_kernel,
        grid=(depth, n // tn),
        in_specs=[pl.BlockSpec((r, d), lambda l, j: (0, 0)),
                  pl.BlockSpec((1, d, tn), lambda l, j: (l, 0, j)),
                  pl.BlockSpec((1, 1, tn), lambda l, j: (l, 0, j))],
        out_specs=pl.BlockSpec((1, r, tn), lambda l, j: (l, 0, j)),
        out_shape=jax.ShapeDtypeStruct((depth, r, n), F32),
        compiler_params=_params(2),
        name="adaln_modulation",
    )(cs, w_mod, b_mod.reshape(depth, 1, n))


class _TokenGrid:
    def __init__(self, stream, is_ctx, layer, mod_rows):
        b, n, self.d = stream.shape
        self.is_ctx = is_ctx
        self.shape = (1, b * n) if is_ctx else (b, n)
        self.tm = min(TOKEN_TILE, self.shape[1])
        self.grid = (self.shape[0], self.shape[1] // self.tm)
        self.orig = (b, n)
        self.mod_base = layer * mod_rows + (mod_rows - 1 if is_ctx else 0)

    def view(self, arr):
        return arr.reshape(self.shape + arr.shape[2:])

    def unview(self, arr):
        return arr.reshape(self.orig + arr.shape[2:])

    def tok(self, width):
        return pl.BlockSpec((1, self.tm, width), lambda i, j: (i, j, 0))

    def mod(self, chunk):
        base, per_batch = self.mod_base, 0 if self.is_ctx else 1
        return pl.BlockSpec((1, 1, self.d), lambda i, j: (base + per_batch * i, 0, chunk))


def _inproj_kernel(*refs, with_rope):
    x_ref, sh_ref, sc_ref, g_ref, w_ref = refs[:5]
    if with_rope:
        cos_ref, sin_ref = refs[5:7]
        refs = refs[7:]
    else:
        refs = refs[5:]
    qg_ref, kg_ref, gsum_ref, rqk_ref, rv_ref, rg_ref, nq_ref, nk_ref, nv_ref, gq_ref, gkv_ref, gate_ref = refs

    x = x_ref[0]
    h = _rms(x) * g_ref[...]
    h = h * (1.0 + sc_ref[0]) + sh_ref[0]
    hb = h.astype(MXU_DTYPE)
    lane = lax.broadcasted_iota(jnp.int32, (x.shape[0], LANES), 1)
    gsum = gsum_ref[...]

    def proj(col, width):
        return _dot(hb, w_ref[:, col:col + width])

    if with_rope:
        cos, sin = cos_ref[...], sin_ref[...]
        first_half = (lane % 32) < 16

        def rope(v):
            partner = jnp.where(first_half, pltpu.roll(v, LANES - 16, 1), pltpu.roll(v, 16, 1))
            return v * cos + partner * sin
    else:
        def rope(v):
            return v

    def head_rms(v):
        ms = _dot((v * v).astype(MXU_DTYPE), gsum) * (1.0 / HEAD_DIM)
        return v * lax.rsqrt(ms + NORM_EPS)

    odt = rqk_ref.dtype
    ret_scale = HEAD_DIM ** -0.5
    att_scale = HEAD_DIM ** -0.5 * LOG2_E

    rq = proj(C_RQ, 256)
    rk = proj(C_RK, 256)
    for j in range(2):
        sl = slice(j * LANES, (j + 1) * LANES)
        rqk_ref[0, :, j * LANES:(j + 1) * LANES] = rope(rq[:, sl]).astype(odt)
        rqk_ref[0, :, 256 + j * LANES:256 + (j + 1) * LANES] = (rope(rk[:, sl]) * ret_scale).astype(odt)
    rv_ref[0] = proj(C_RV, HEAD_W).astype(odt)
    rg = proj(C_RG, HEAD_W)
    rg_ref[0] = (rg * jax.nn.sigmoid(rg)).astype(odt)
    nq_ref[0] = (proj(C_NQ, HEAD_W) * att_scale).astype(odt)
    nk_ref[0] = proj(C_NK, HEAD_W).astype(odt)
    nv = proj(C_NV, HEAD_W).astype(odt)
    for j in range(NA_HEADS // 2):
        nv_ref[0, :, 2 * j * LANES:(2 * j + 1) * LANES] = nv[:, j * LANES:(j + 1) * LANES]
        nv_ref[0, :, (2 * j + 1) * LANES:(2 * j + 2) * LANES] = jnp.ones((nv.shape[0], LANES), odt)

    gq = proj(C_GQ, HEAD_W)
    qg = qg_ref[...]
    for j in range(2):
        normed = head_rms(gq[:, 2 * j * LANES:(2 * j + 2) * LANES])
        for i in range(2):
            sl = slice((2 * j + i) * LANES, (2 * j + i + 1) * LANES)
            gq_ref[0, :, sl] = (rope(normed[:, i * LANES:(i + 1) * LANES] * qg) * att_scale).astype(odt)
    gkv = proj(C_GK, 2 * LANES)
    gk = rope(head_rms(gkv)[:, 0:LANES] * kg_ref[...])
    gv = gkv[:, LANES:]
    low_half = lane < HEAD_DIM
    gk_sw, gv_sw = pltpu.roll(gk, HEAD_DIM, 1), pltpu.roll(gv, HEAD_DIM, 1)
    gkv_ref[0, :, 0:LANES] = jnp.where(low_half, gk, gk_sw).astype(odt)
    gkv_ref[0, :, LANES:2 * LANES] = jnp.where(low_half, gk_sw, gk).astype(odt)
    gkv_ref[0, :, 2 * LANES:3 * LANES] = jnp.where(low_half, gv, 1.0).astype(odt)
    gkv_ref[0, :, 3 * LANES:4 * LANES] = jnp.where(low_half, gv_sw, 1.0).astype(odt)

    for j in range(6):
        sl = slice(j * HEAD_W, (j + 1) * HEAD_W)
        gate_ref[0, :, sl] = jax.nn.sigmoid(proj(C_GATE + j * HEAD_W, HEAD_W)).astype(odt)


def _in_projection(l, stream, is_ctx, mod3, mod_rows, layer_gain, w_in, rope_tables, q_gain, k_gain, gsum):
    tg = _TokenGrid(stream, is_ctx, l, mod_rows)
    d = tg.d
    out_widths = [HEAD_W] * 5 + [2 * HEAD_W] + [HEAD_W] * 2 + [3 * d]
    rope_specs = [] if is_ctx else [pl.BlockSpec((tg.tm, LANES), lambda i, j: (j, 0))] * 2
    rope_args = () if is_ctx else rope_tables
    outs = pl.pallas_call(
        functools.partial(_inproj_kernel, with_rope=not is_ctx),
        grid=tg.grid,
        in_specs=[tg.tok(d), tg.mod(0), tg.mod(1), _layer(layer_gain, l), _layer(w_in, l)] + rope_specs
        + [_layer(q_gain, l), _layer(k_gain, l), _resident(gsum.shape)],
        out_specs=[tg.tok(w) for w in out_widths],
        out_shape=[jax.ShapeDtypeStruct(tg.shape + (w,), MXU_DTYPE) for w in out_widths],
        compiler_params=_params(2),
        name="in_projection_ctx" if is_ctx else "in_projection",
    )(tg.view(stream), mod3, mod3, layer_gain, w_in, *rope_args, q_gain, k_gain, gsum)
    return tuple(tg.unview(o) for o in outs)


def _retention_kernel(*refs, with_ctx_out):
    logit_ref, qk_c, v_c, g_c, qk_l, v_l, g_l = refs[:7]
    if with_ctx_out:
        o_c, o_l, dec_ref, st_ref = refs[7:]
    else:
        o_c = None
        o_l, dec_ref, st_ref = refs[7:]
    c_len = RET_CHUNK
    n_ctx_chunks = qk_c.shape[1] // c_len
    n_lat_chunks = qk_l.shape[1] // c_len
    n_chunks = n_ctx_chunks + n_lat_chunks
    heads = RET_HEADS
    pairs = heads // 2
    t_mask, t_dkf, t_dkb, t_dsf, t_dsb = (k * heads for k in range(5))
    t_dqf, t_dqb = 5 * heads, 5 * heads + pairs
    cdt = qk_l.dtype

    row = lax.broadcasted_iota(jnp.int32, (c_len, LANES), 0)
    lane = lax.broadcasted_iota(jnp.int32, (c_len, LANES), 1)
    lane_lo = lane < HEAD_DIM
    row_lo = row < HEAD_DIM

    @pl.when(pl.program_id(0) == 0)
    def _build_tables():
        rowf = row.astype(F32)
        lanef = lane.astype(F32)
        rel = rowf - lanef
        lg = [[jnp.broadcast_to(jax.nn.log_sigmoid(logit_ref[dd, hh])[0:1, :], (c_len, LANES))
               for hh in range(heads)] for dd in range(2)]
        for hh in range(heads):
            lf, lb = lg[0][hh], lg[1][hh]
            fwd = jnp.where(rel >= 0, jnp.exp(lf * jnp.maximum(rel, 0.0)), 0.0)
            bwd = jnp.where(rel < 0, jnp.exp(lb * jnp.maximum(-rel, 0.0)), 0.0)
            dec_ref[t_mask + hh] = fwd + bwd
            dec_ref[t_dkf + hh] = jnp.exp(lf * (c_len - 1.0 - lanef))
            dec_ref[t_dkb + hh] = jnp.exp(lb * lanef)
            dec_ref[t_dsf + hh] = jnp.exp(lf * float(c_len))
            dec_ref[t_dsb + hh] = jnp.exp(lb * float(c_len))
        for p in range(pairs):
            lf = jnp.where(lane_lo, lg[0][2 * p], lg[0][2 * p + 1])
            lb = jnp.where(lane_lo, lg[1][2 * p], lg[1][2 * p + 1])
            dec_ref[t_dqf + p] = jnp.exp(lf * (rowf + 1.0))
            dec_ref[t_dqb + p] = jnp.exp(lb * (float(c_len) - rowf))

    def summaries(qk_ref, v_ref, rows, c):
        for p in range(pairs):
            kt = qk_ref[0, rows, 256 + p * LANES:256 + (p + 1) * LANES].astype(F32).T
            for half in range(2):
                hh = 2 * p + half
                ktm = jnp.where(row_lo if half == 0 else jnp.logical_not(row_lo), kt, 0.0)
                lhs = jnp.concatenate([ktm * dec_ref[t_dkf + hh], ktm * dec_ref[t_dkb + hh]], axis=0)
                st_ref[c, hh] = _dot(lhs.astype(cdt), v_ref[0, rows, hh * LANES:(hh + 1) * LANES])

    def outputs(qk_ref, v_ref, g_ref, o_ref, rows, c):
        zero = jnp.zeros((), cdt)
        for p in range(pairs):
            q = qk_ref[0, rows, p * LANES:(p + 1) * LANES]
            k = qk_ref[0, rows, 256 + p * LANES:256 + (p + 1) * LANES]
            qs = jnp.concatenate([jnp.where(lane_lo, q, zero), jnp.where(lane_lo, zero, q)], axis=0)
            a = _dot_nt(qs, k)
            q32 = q.astype(F32)
            qf = (q32 * dec_ref[t_dqf + p]).astype(cdt)
            qb = (q32 * dec_ref[t_dqb + p]).astype(cdt)
            for half in range(2):
                hh = 2 * p + half
                sl = slice(hh * LANES, (hh + 1) * LANES)
                pm = (a[half * c_len:(half + 1) * c_len] * dec_ref[t_mask + hh]).astype(cdt)
                lhs = jnp.concatenate([pm, qf, qb], axis=1)
                rhs = jnp.concatenate([v_ref[0, rows, sl], st_ref[c, hh].astype(cdt)], axis=0)
                y = _rms(_dot(lhs, rhs)) * g_ref[0, rows, sl].astype(F32)
                o_ref[0, rows, sl] = y.astype(o_ref.dtype)

    def lat_rows(i):
        return pl.ds(pl.multiple_of(i * c_len, c_len), c_len)

    for ci in range(n_ctx_chunks):
        summaries(qk_c, v_c, slice(ci * c_len, (ci + 1) * c_len), ci)

    def lat_summaries(i, carry):
        summaries(qk_l, v_l, lat_rows(i), n_ctx_chunks + i)
        return carry

    lax.fori_loop(0, n_lat_chunks, lat_summaries, 0, unroll=8)

    fwd_order = list(range(n_chunks))
    bwd_order = list(range(n_ctx_chunks - 1, -1, -1)) + list(range(n_chunks - 1, n_ctx_chunks - 1, -1))
    for hh in range(heads):
        for order, lo, t_ds in ((fwd_order, 0, t_dsf), (bwd_order, c_len, t_dsb)):
            decay = dec_ref[t_ds + hh]
            state = jnp.zeros((c_len, LANES), F32)
            for c in order:
                z = st_ref[c, hh, lo:lo + c_len, :]
                st_ref[c, hh, lo:lo + c_len, :] = state
                state = state * decay + z

    if with_ctx_out:
        for ci in range(n_ctx_chunks):
            outputs(qk_c, v_c, g_c, o_c, slice(ci * c_len, (ci + 1) * c_len), ci)

    def lat_outputs(i, carry):
        outputs(qk_l, v_l, g_l, o_l, lat_rows(i), n_ctx_chunks + i)
        return carry

    lax.fori_loop(0, n_lat_chunks, lat_outputs, 0, unroll=8)


def _retention(l, logit_tile, ctx_qvg, lat_qvg, with_ctx_out):
    b, n_ctx, _ = ctx_qvg[0].shape
    n_lat = lat_qvg[0].shape[1]
    n_tables = 5 * RET_HEADS + RET_HEADS
    out_shape = [jax.ShapeDtypeStruct((b, n_lat, HEAD_W), MXU_DTYPE)]
    if with_ctx_out:
        out_shape.insert(0, jax.ShapeDtypeStruct((b, n_ctx, HEAD_W), MXU_DTYPE))
    outs = pl.pallas_call(
        functools.partial(_retention_kernel, with_ctx_out=with_ctx_out),
        grid=(b,),
        in_specs=[_layer(logit_tile, l)] + [_batch_block(a) for a in (*ctx_qvg, *lat_qvg)],
        out_specs=[_batch_block(s) for s in out_shape],
        out_shape=out_shape,
        scratch_shapes=[pltpu.VMEM((n_tables, RET_CHUNK, LANES), F32),
                        pltpu.VMEM(((n_ctx + n_lat) // RET_CHUNK, RET_HEADS, 2 * RET_CHUNK, LANES), F32)],
        compiler_params=_params(1),
        name="retention",
    )(logit_tile, *ctx_qvg, *lat_qvg)
    return (outs[0], outs[1]) if with_ctx_out else (None, outs[0])


def _na_kernel(*refs, with_ctx_out):
    q_c, k_c, v_c, q_l, k_l, v_l, bias_ref, bias_shift_ref, bias_pad_ref = refs[:9]
    if with_ctx_out:
        o_c, o_l, s_ref, p_ref = refs[9:]
    else:
        o_l, s_ref, p_ref = refs[9:]
    n_ctx = k_c.shape[1]
    n_rows = q_l.shape[1] // GRID_W
    n_blocks = n_rows // NA_BLOCK_ROWS
    qb = NA_BLOCK_ROWS * GRID_W
    un = NA_UNION_ROWS * GRID_W
    win = NA_ROWS * GRID_W
    half_win = NA_ROWS // 2
    cdt = p_ref.dtype

    def stacked(q):
        lo = lax.broadcasted_iota(jnp.int32, q.shape, 1) < HEAD_DIM
        zero = jnp.zeros((), q.dtype)
        return jnp.concatenate([jnp.where(lo, q, zero), jnp.where(lo, zero, q)], axis=0)

    def unstack(o):
        m = o.shape[0] // 2
        lo = lax.broadcasted_iota(jnp.int32, (m, LANES), 1) < HEAD_DIM
        o = o[:, 0:LANES] / o[:, LANES:]
        return jnp.where(lo, o[0:m], o[m:])

    def softmax(parts):
        m = parts[0].max(axis=-1, keepdims=True)
        for s in parts[1:]:
            m = jnp.maximum(m, s.max(axis=-1, keepdims=True))
        return [jnp.exp2(s - m) for s in parts]

    def values(v_ref, rows, p):
        return v_ref[0, rows, 2 * p * LANES:(2 * p + 2) * LANES]

    def block(g, config):
        if config == "first":
            u0 = 0
        elif config == "last":
            u0 = n_rows - NA_UNION_ROWS
        else:
            u0 = g * NA_BLOCK_ROWS - half_win
        q_rows = pl.ds(pl.multiple_of(g * qb, qb), qb)
        k_rows = pl.ds(pl.multiple_of(u0 * GRID_W, GRID_W), un)

        def scores(p):
            sl = slice(p * LANES, (p + 1) * LANES)
            qs = stacked(q_l[0, q_rows, sl])
            s_ref[p % 2, :, 0:un] = _dot_nt(qs, k_l[0, k_rows, sl])
            s_ref[p % 2, :, un:] = _dot_nt(qs, k_c[0, :, sl])

        scores(0)
        for p in range(NA_HEADS // 2):
            sl = slice(p * LANES, (p + 1) * LANES)
            pbuf, sbuf = p_ref.at[p % 2], s_ref.at[p % 2]
            if p + 1 < NA_HEADS // 2:
                scores(p + 1)
            for half in range(2):
                hh = 2 * p + half
                for a in range(NA_BLOCK_ROWS):
                    rows = slice(half * qb + a * GRID_W, half * qb + (a + 1) * GRID_W)
                    if config == "first":
                        w, off = 0, NA_ROWS - 1 - a
                    elif config == "last":
                        w, off = (n_rows - NA_ROWS - u0) * GRID_W, half_win - 1 - a
                    else:
                        w, off = a * GRID_W, half_win - 1
                    if w % LANES:
                        lo, width, bias = w - GRID_W, win + 2 * GRID_W, bias_pad_ref[hh]
                    elif off % 2:
                        lo, width, bias = w, win, bias_shift_ref[hh, :, (off - 1) * GRID_W:(off - 1) * GRID_W + win]
                    else:
                        lo, width, bias = w, win, bias_ref[hh, :, off * GRID_W:off * GRID_W + win]
                    p_loc, p_cx = softmax([sbuf[rows, lo:lo + width] + bias, sbuf[rows, un:]])
                    pieces = [p_loc, jnp.zeros((GRID_W, un - lo - width), F32), p_cx]
                    if lo:
                        pieces = [jnp.zeros((GRID_W, lo), F32)] + pieces
                    pbuf[rows, :] = jnp.concatenate([x for x in pieces if x.shape[1]], axis=1).astype(cdt)
            o = _dot(pbuf[:, 0:un], values(v_l, k_rows, p)) + _dot(pbuf[:, un:], values(v_c, slice(None), p))
            o_l[0, q_rows, sl] = unstack(o).astype(o_l.dtype)

    block(0, "first")

    def middle(g, carry):
        block(g, "middle")
        return carry

    lax.fori_loop(1, n_blocks - 1, middle, 0)
    block(n_blocks - 1, "last")

    if with_ctx_out:
        for p in range(NA_HEADS // 2):
            sl = slice(p * LANES, (p + 1) * LANES)
            (pn,) = softmax([_dot_nt(stacked(q_c[0, :, sl]), k_c[0, :, sl])])
            o = _dot(pn.astype(cdt), values(v_c, slice(None), p))
            o_c[0, :, sl] = unstack(o).astype(o_c.dtype)


def _neighbourhood(l, ctx_qkv, lat_qkv, bias_tables, with_ctx_out):
    b, n_ctx, _ = ctx_qkv[0].shape
    n_lat = lat_qkv[0].shape[1]
    p_cols = NA_UNION_ROWS * GRID_W + n_ctx
    stacked_rows = 2 * NA_BLOCK_ROWS * GRID_W
    out_shape = [jax.ShapeDtypeStruct((b, n_lat, HEAD_W), MXU_DTYPE)]
    if with_ctx_out:
        out_shape.insert(0, jax.ShapeDtypeStruct((b, n_ctx, HEAD_W), MXU_DTYPE))
    outs = pl.pallas_call(
        functools.partial(_na_kernel, with_ctx_out=with_ctx_out),
        grid=(b,),
        in_specs=[_batch_block(a) for a in (*ctx_qkv, *lat_qkv)] + [_layer(tbl, l) for tbl in bias_tables],
        out_specs=[_batch_block(s) for s in out_shape],
        out_shape=out_shape,
        scratch_shapes=[pltpu.VMEM((2, stacked_rows, p_cols), F32),
                        pltpu.VMEM((2, stacked_rows, p_cols), MXU_DTYPE)],
        compiler_params=_params(1),
        name="neighbourhood_attention",
    )(*ctx_qkv, *lat_qkv, *bias_tables)
    return (outs[0], outs[1]) if with_ctx_out else (None, outs[0])


def _na_bias_table(rel_bias):
    col = np.arange(GRID_W)
    col_start = np.clip(col - NA_COLS // 2, 0, GRID_W - NA_COLS)
    in_window = (col[None, :] >= col_start[:, None]) & (col[None, :] < col_start[:, None] + NA_COLS)
    dcol = np.clip(col[None, :] - col[:, None], 1 - NA_COLS, NA_COLS - 1) + NA_COLS - 1
    onehot = (dcol[None] == np.arange(2 * NA_COLS - 1)[:, None, None]) & in_window[None]
    rest = rel_bias.astype(F32) * LOG2_E
    cb = None
    for _ in range(3):
        piece = rest.astype(MXU_DTYPE)
        rest = rest - piece.astype(F32)
        part = jnp.einsum("lhdc,cqk->lhqdk", piece, jnp.asarray(onehot, MXU_DTYPE), preferred_element_type=F32)
        cb = part if cb is None else cb + part
    cb = jnp.where(jnp.asarray(in_window)[:, None, :], cb, NEG_INF)
    table = cb.reshape(cb.shape[:3] + ((2 * NA_ROWS - 1) * GRID_W,))
    mid = (NA_ROWS // 2 - 1) * GRID_W
    side = jnp.full(cb.shape[:3] + (GRID_W,), NEG_INF, F32)
    padded = jnp.concatenate([side, table[..., mid:mid + NA_ROWS * GRID_W], side], axis=-1)
    return table, table[..., GRID_W:], padded


def _gqa_kernel(*refs, chunks):
    q_ref, kv_refs, (o_ref, s_ref, p_ref) = refs[0], refs[1:-3], refs[-3:]
    tq = q_ref.shape[1]
    n_kv = GQA_KV_HEADS
    rb = GQA_ROW_BLOCK
    n_blk = GQA_Q_HEADS // n_kv * tq // rb
    per_head = tq // rb
    lo = lax.broadcasted_iota(jnp.int32, (tq, LANES), 1) < HEAD_DIM
    zero = jnp.zeros((), q_ref.dtype)

    def stacked_queries(g):
        parts = []
        for j in range(2 * g, 2 * g + 2):
            q = q_ref[0, :, j * LANES:(j + 1) * LANES]
            parts += [jnp.where(lo, q, zero), jnp.where(lo, zero, q)]
        return jnp.concatenate(parts, axis=0)

    items = [(g, ci) for g in range(n_kv) for ci in range(len(chunks))]
    qs = [stacked_queries(g) for g in range(n_kv)]

    def rows_of(ci, lane_tile):
        pieces = [kv_refs[src][0, c0:c1, lane_tile * LANES:(lane_tile + 1) * LANES] for src, c0, c1 in chunks[ci]]
        return pieces[0] if len(pieces) == 1 else jnp.concatenate(pieces, axis=0)

    def scores(idx):
        g, ci = items[idx]
        s_ref[idx % 2, :, 0:widths[ci]] = _dot_nt(qs[g], rows_of(ci, g))

    widths = [sum(c1 - c0 for _, c0, c1 in pieces) for pieces in chunks]
    scores(0)
    m = acc = None
    for idx, (g, ci) in enumerate(items):
        w, slot = widths[ci], idx % 2
        if idx + 1 < len(items):
            scores(idx + 1)
        if ci == 0:
            m, acc = [None] * n_blk, [None] * n_blk
        alpha = [None] * n_blk
        for i in range(n_blk):
            rows = slice(i * rb, (i + 1) * rb)
            s = s_ref[slot, rows, 0:w]
            s_max = s.max(axis=-1, keepdims=True)
            m_new = s_max if ci == 0 else jnp.maximum(m[i], s_max)
            if ci:
                alpha[i] = jnp.exp2(m[i] - m_new)
            m[i] = m_new
            p_ref[slot, rows, 0:w] = jnp.exp2(s - m_new).astype(p_ref.dtype)
        pv = _dot(p_ref[slot, :, 0:w], rows_of(ci, n_kv + g))
        for i in range(n_blk):
            part = pv[i * rb:(i + 1) * rb]
            acc[i] = part if ci == 0 else alpha[i] * acc[i] + part
        if ci + 1 == len(chunks):
            for j in range(2):
                halves = []
                for half in range(2):
                    h = 2 * j + half
                    a = jnp.concatenate(acc[h * per_head:(h + 1) * per_head], axis=0)
                    a_sw = pltpu.roll(a, HEAD_DIM, 1)
                    halves.append(a / a_sw if half == 0 else a_sw / a)
                pair = 2 * g + j
                o_ref[0, :, pair * LANES:(pair + 1) * LANES] = jnp.where(lo, halves[0], halves[1]).astype(o_ref.dtype)


def _gqa(gq, kv_sources):
    b, n_q, _ = gq.shape
    tq = min(GQA_Q_TILE, n_q)
    starts = np.cumsum([0] + [kv.shape[1] for kv in kv_sources])
    chunks = []
    for c0 in range(0, int(starts[-1]), GQA_KEY_CHUNK):
        c1 = min(c0 + GQA_KEY_CHUNK, int(starts[-1]))
        pieces = []
        for src in range(len(kv_sources)):
            lo, hi = max(c0, int(starts[src])), min(c1, int(starts[src + 1]))
            if lo < hi:
                pieces.append((src, lo - int(starts[src]), hi - int(starts[src])))
        chunks.append(tuple(pieces))
    stacked_rows = GQA_Q_HEADS // GQA_KV_HEADS * tq
    width = min(GQA_KEY_CHUNK, int(starts[-1]))
    return pl.pallas_call(
        functools.partial(_gqa_kernel, chunks=tuple(chunks)),
        grid=(b, n_q // tq),
        in_specs=[pl.BlockSpec((1, tq, HEAD_W), lambda i, j: (i, j, 0))] + [_batch_block(kv) for kv in kv_sources],
        out_specs=pl.BlockSpec((1, tq, HEAD_W), lambda i, j: (i, j, 0)),
        out_shape=jax.ShapeDtypeStruct((b, n_q, HEAD_W), MXU_DTYPE),
        scratch_shapes=[pltpu.VMEM((2, stacked_rows, width), F32),
                        pltpu.VMEM((2, stacked_rows, width), MXU_DTYPE)],
        compiler_params=_params(2),
        name="gqa_attention",
    )(gq, *kv_sources)


def _post_kernel(yr_ref, yn_ref, yg_ref, gate_ref, s_ref, gt1_ref, sh2_ref, sc2_ref, gt2_ref,
                 gpost_ref, gpre_ref, gpost2_ref, wr_ref, wn_ref, wg_ref, wout_ref, w1_ref, w2_ref, o_ref):
    tm, d = o_ref.shape[1:]

    def mixed(rows):
        y = None
        for i, (br, w_ref) in enumerate(((yr_ref, wr_ref), (yn_ref, wn_ref), (yg_ref, wg_ref))):
            z = _dot(br[0, rows, :], w_ref[...]) * gate_ref[0, rows, i * d:(i + 1) * d].astype(F32)
            y = z if y is None else y + z
        return _dot(y.astype(MXU_DTYPE), wout_ref[...])

    def mlp(rows, y):
        x = s_ref[0, rows, :] + gt1_ref[0] * (_rms(y) * gpost_ref[...])
        h = _rms(x) * gpre_ref[...]
        h = (h * (1.0 + sc2_ref[0]) + sh2_ref[0]).astype(MXU_DTYPE)
        acc = None
        for j in range(D_FF // d):
            u = jnp.maximum(_dot(h, w1_ref[:, j * d:(j + 1) * d]), 0.0)
            part = _dot((u * u).astype(MXU_DTYPE), w2_ref[j * d:(j + 1) * d, :])
            acc = part if acc is None else acc + part
        o_ref[0, rows, :] = x + gt2_ref[0] * (_rms(acc) * gpost2_ref[...])

    groups = [slice(r0, r0 + POST_ROW_GROUP) for r0 in range(0, tm, POST_ROW_GROUP)]
    ys = [mixed(rows) for rows in groups]
    for rows, y in zip(groups, ys):
        mlp(rows, y)


def _post(l, y_ret, y_na, y_gqa, gates, stream, is_ctx, mod3, mod_rows, gains, weights):
    tg = _TokenGrid(stream, is_ctx, l, mod_rows)
    d = tg.d
    out = pl.pallas_call(
        _post_kernel,
        grid=tg.grid,
        in_specs=[tg.tok(HEAD_W), tg.tok(HEAD_W), tg.tok(HEAD_W), tg.tok(3 * d), tg.tok(d)]
        + [tg.mod(chunk) for chunk in (2, 3, 4, 5)]
        + [_layer(a, l) for a in gains] + [_layer(w, l) for w in weights],
        out_specs=tg.tok(d),
        out_shape=jax.ShapeDtypeStruct(tg.shape + (d,), F32),
        compiler_params=_params(2),
        name="merge_mlp_ctx" if is_ctx else "merge_mlp",
    )(tg.view(y_ret), tg.view(y_na), tg.view(y_gqa), tg.view(gates), tg.view(stream),
      mod3, mod3, mod3, mod3, *gains, *weights)
    return tg.unview(out)


def _rope_tables(n_latent):
    t = np.arange(n_latent)
    pos = np.stack([t // GRID_W, t % GRID_W], axis=-1).astype(np.float64)
    n_freq = HEAD_DIM // 4
    inv_freq = ROPE_BASE ** (-np.arange(n_freq, dtype=np.float64) / n_freq)
    lane = np.arange(LANES) % HEAD_DIM
    axis, second, freq = lane // 32, (lane % 32) // 16, lane % 16
    ang = pos[:, axis] * inv_freq[freq][None, :]
    return jnp.asarray(np.cos(ang), F32), jnp.asarray(np.sin(ang) * np.where(second == 1, 1.0, -1.0), F32)


def kernel(x, c, ctx, c_ctx, w_mod, b_mod, g_pre_mix, g_post_mix, g_pre_mlp, g_post_mlp, w_in, ret_decay_logit, na_rel_bias, gqa_q_norm, gqa_k_norm, w_br_ret, w_br_na, w_br_gqa, w_out, w_mlp_in, w_mlp_out):
    b, n, d = x.shape
    depth = w_mod.shape[0]
    cdt = MXU_DTYPE

    rope_tables = _rope_tables(n)
    lane_head = np.arange(2 * LANES) // HEAD_DIM
    gsum = jnp.asarray(lane_head[:, None] == lane_head[None, :], cdt)

    n_rows = -(-(b + 1) // 8) * 8
    cs = jnp.concatenate([c, jnp.zeros((n_rows - b - 1, d), F32), c_ctx[None, :]], axis=0)
    mod3 = _modulation(cs, w_mod, b_mod).reshape(depth * n_rows, 1, 6 * d)

    w_in_c = w_in.astype(cdt)
    post_weights = tuple(w.astype(cdt) for w in (w_br_ret, w_br_na, w_br_gqa, w_out, w_mlp_in, w_mlp_out))
    post_gains = tuple(g[:, None, :] for g in (g_post_mix, g_pre_mlp, g_post_mlp))
    pre_gain = g_pre_mix[:, None, :]
    q_gain = jnp.tile(gqa_q_norm, (1, 2))[:, None, :]
    k_gain = jnp.tile(gqa_k_norm, (1, 2))[:, None, :]
    logit_tile = jnp.broadcast_to(ret_decay_logit.astype(F32)[..., None, None], (depth, 2, RET_HEADS, 8, LANES))
    bias_tables = _na_bias_table(na_rel_bias)

    ctx_s, lat_s = ctx, x
    for l in range(depth):
        last = l == depth - 1
        pc = _in_projection(l, ctx_s, True, mod3, n_rows, pre_gain, w_in_c, rope_tables, q_gain, k_gain, gsum)
        pt = _in_projection(l, lat_s, False, mod3, n_rows, pre_gain, w_in_c, rope_tables, q_gain, k_gain, gsum)
        ret_c, ret_l = _retention(l, logit_tile, pc[0:3], pt[0:3], not last)
        na_c, na_l = _neighbourhood(l, pc[3:6], pt[3:6], bias_tables, not last)
        gqa_l = _gqa(pt[6], (pc[7], pt[7]))
        lat_s = _post(l, ret_l, na_l, gqa_l, pt[8], lat_s, False, mod3, n_rows, post_gains, post_weights)
        if not last:
            gqa_c = _gqa(pc[6], (pc[7],))
            ctx_s = _post(l, ret_c, na_c, gqa_c, pc[8], ctx_s, True, mod3, n_rows, post_gains, post_weights)
    return lat_s
```

```python
import functools

import jax
import jax.numpy as jnp
import numpy as np
from jax import lax
from jax.experimental import pallas as pl
from jax.experimental.pallas import tpu as pltpu

F32 = jnp.float32
MXU_DTYPE = jnp.bfloat16

D_MODEL = 1024
GRID_W = 64
CTX_LEN = 256
HEAD_DIM = 64
RET_HEADS = 4
RET_CHUNK = 128
NA_HEADS = 8
NA_ROWS = 8
NA_COLS = 16
NA_BLOCK_ROWS = 4
NA_UNION_ROWS = NA_BLOCK_ROWS + NA_ROWS
GQA_Q_HEADS = 8
GQA_KV_HEADS = 2
GQA_Q_TILE = 256
GQA_KEY_CHUNK = 768
GQA_ROW_BLOCK = 64
D_FF = 4 * D_MODEL
ROPE_BASE = 10000.0
NORM_EPS = 1e-6
NEG_INF = -1e30
LOG2_E = 1.4426950408889634

LANES = 128
HEAD_W = 512
C_RQ, C_RK, C_RV, C_RG = 0, 256, 512, 1024
C_NQ, C_NK, C_NV = 1536, 2048, 2560
C_GQ, C_GK, C_GV = 3072, 3584, 3712
C_GATE = 3840
IN_W = 6912

TOKEN_TILE = 512
POST_ROW_GROUP = 256
VMEM_LIMIT = 56 * 1024 * 1024


def _rms(x):
    return x * lax.rsqrt(jnp.mean(x * x, axis=-1, keepdims=True) + NORM_EPS)


def _dot(a, b):
    return jnp.dot(a, b, preferred_element_type=F32)


def _dot_nt(a, b):
    return lax.dot_general(a, b, (((1,), (1,)), ((), ())), preferred_element_type=F32)


def _resident(shape):
    return pl.BlockSpec(shape, lambda *_: (0,) * len(shape), pipeline_mode=pl.Buffered(1))


def _layer(arr, l):
    return pl.BlockSpec((None,) + arr.shape[1:], lambda *_: (l,) + (0,) * (arr.ndim - 1),
                        pipeline_mode=pl.Buffered(1))


def _batch_block(arr):
    return pl.BlockSpec((1,) + arr.shape[1:], lambda i, *_: (i, 0, 0))


def _params(n_axes):
    return pltpu.CompilerParams(dimension_semantics=("arbitrary",) * n_axes,
                                vmem_limit_bytes=VMEM_LIMIT)


def _mod_kernel(c_ref, w_ref, b_ref, o_ref):
    c = c_ref[...]
    s = (c * jax.nn.sigmoid(c)).astype(MXU_DTYPE)
    o_ref[0] = _dot(s, w_ref[0].astype(MXU_DTYPE)) + b_ref[0]


def _modulation(cs, w_mod, b_mod):
    depth, d, n = w_mod.shape
    r = cs.shape[0]
    tn = 1024
    return pl.pallas_call(
        _mod_kernel,
        grid=(depth, n // tn),
        in_specs=[pl.BlockSpec((r, d), lambda l, j: (0, 0)),
                  pl.BlockSpec((1, d, tn), lambda l, j: (l, 0, j)),
                  pl.BlockSpec((1, 1, tn), lambda l, j: (l, 0, j))],
        out_specs=pl.BlockSpec((1, r, tn), lambda l, j: (l, 0, j)),
        out_shape=jax.ShapeDtypeStruct((depth, r, n), F32),
        compiler_params=_params(2),
        name="adaln_modulation",
    )(cs, w_mod, b_mod.reshape(depth, 1, n))


class _TokenGrid:
    def __init__(self, stream, is_ctx, layer, mod_rows):
        b, n, self.d = stream.shape
        self.is_ctx = is_ctx
        self.shape = (1, b * n) if is_ctx else (b, n)
        self.tm = min(TOKEN_TILE, self.shape[1])
        self.grid = (self.shape[0], self.shape[1] // self.tm)
        self.orig = (b, n)
        self.mod_base = layer * mod_rows + (mod_rows - 1 if is_ctx else 0)

    def view(self, arr):
        return arr.reshape(self.shape + arr.shape[2:])

    def unview(self, arr):
        return arr.reshape(self.orig + arr.shape[2:])

    def tok(self, width):
        return pl.BlockSpec((1, self.tm, width), lambda i, j: (i, j, 0))

    def mod(self, chunk):
        base, per_batch = self.mod_base, 0 if self.is_ctx else 1
        return pl.BlockSpec((1, 1, self.d), lambda i, j: (base + per_batch * i, 0, chunk))


def _inproj_kernel(*refs, with_rope):
    x_ref, sh_ref, sc_ref, g_ref, w_ref = refs[:5]
    if with_rope:
        cos_ref, sin_ref = refs[5:7]
        refs = refs[7:]
    else:
        refs = refs[5:]
    qg_ref, kg_ref, gsum_ref, rqk_ref, rv_ref, rg_ref, nq_ref, nk_ref, nv_ref, gq_ref, gkv_ref, gate_ref = refs

    x = x_ref[0]
    h = _rms(x) * g_ref[...]
    h = h * (1.0 + sc_ref[0]) + sh_ref[0]
    hb = h.astype(MXU_DTYPE)
    lane = lax.broadcasted_iota(jnp.int32, (x.shape[0], LANES), 1)
    gsum = gsum_ref[...]

    def proj(col, width):
        return _dot(hb, w_ref[:, col:col + width])

    if with_rope:
        cos, sin = cos_ref[...], sin_ref[...]
        first_half = (lane % 32) < 16

        def rope(v):
            partner = jnp.where(first_half, pltpu.roll(v, LANES - 16, 1), pltpu.roll(v, 16, 1))
            return v * cos + partner * sin
    else:
        def rope(v):
            return v

    def head_rms(v):
        ms = _dot((v * v).astype(MXU_DTYPE), gsum) * (1.0 / HEAD_DIM)
        return v * lax.rsqrt(ms + NORM_EPS)

    odt = rqk_ref.dtype
    ret_scale = HEAD_DIM ** -0.5
    att_scale = HEAD_DIM ** -0.5 * LOG2_E

    rq = proj(C_RQ, 256)
    rk = proj(C_RK, 256)
    for j in range(2):
        sl = slice(j * LANES, (j + 1) * LANES)
        rqk_ref[0, :, j * LANES:(j + 1) * LANES] = rope(rq[:, sl]).astype(odt)
        rqk_ref[0, :, 256 + j * LANES:256 + (j + 1) * LANES] = (rope(rk[:, sl]) * ret_scale).astype(odt)
    rv_ref[0] = proj(C_RV, HEAD_W).astype(odt)
    rg = proj(C_RG, HEAD_W)
    rg_ref[0] = (rg * jax.nn.sigmoid(rg)).astype(odt)
    nq_ref[0] = (proj(C_NQ, HEAD_W) * att_scale).astype(odt)
    nk_ref[0] = proj(C_NK, HEAD_W).astype(odt)
    nv = proj(C_NV, HEAD_W).astype(odt)
    for j in range(NA_HEADS // 2):
        nv_ref[0, :, 2 * j * LANES:(2 * j + 1) * LANES] = nv[:, j * LANES:(j + 1) * LANES]
        nv_ref[0, :, (2 * j + 1) * LANES:(2 * j + 2) * LANES] = jnp.ones((nv.shape[0], LANES), odt)

    gq = proj(C_GQ, HEAD_W)
    qg = qg_ref[...]
    for j in range(2):
        normed = head_rms(gq[:, 2 * j * LANES:(2 * j + 2) * LANES])
        for i in range(2):
            sl = slice((2 * j + i) * LANES, (2 * j + i + 1) * LANES)
            gq_ref[0, :, sl] = (rope(normed[:, i * LANES:(i + 1) * LANES] * qg) * att_scale).astype(odt)
    gkv = proj(C_GK, 2 * LANES)
    gk = rope(head_rms(gkv)[:, 0:LANES] * kg_ref[...])
    gv = gkv[:, LANES:]
    low_half = lane < HEAD_DIM
    gk_sw, gv_sw = pltpu.roll(gk, HEAD_DIM, 1), pltpu.roll(gv, HEAD_DIM, 1)
    gkv_ref[0, :, 0:LANES] = jnp.where(low_half, gk, gk_sw).astype(odt)
    gkv_ref[0, :, LANES:2 * LANES] = jnp.where(low_half, gk_sw, gk).astype(odt)
    gkv_ref[0, :, 2 * LANES:3 * LANES] = jnp.where(low_half, gv, 1.0).astype(odt)
    gkv_ref[0, :, 3 * LANES:4 * LANES] = jnp.where(low_half, gv_sw, 1.0).astype(odt)

    for j in range(6):
        sl = slice(j * HEAD_W, (j + 1) * HEAD_W)
        gate_ref[0, :, sl] = jax.nn.sigmoid(proj(C_GATE + j * HEAD_W, HEAD_W)).astype(odt)


def _in_projection(l, stream, is_ctx, mod3, mod_rows, layer_gain, w_in, rope_tables, q_gain, k_gain, gsum):
    tg = _TokenGrid(stream, is_ctx, l, mod_rows)
    d = tg.d
    out_widths = [HEAD_W] * 5 + [2 * HEAD_W] + [HEAD_W] * 2 + [3 * d]
    rope_specs = [] if is_ctx else [pl.BlockSpec((tg.tm, LANES), lambda i, j: (j, 0))] * 2
    rope_args = () if is_ctx else rope_tables
    outs = pl.pallas_call(
        functools.partial(_inproj_kernel, with_rope=not is_ctx),
        grid=tg.grid,
        in_specs=[tg.tok(d), tg.mod(0), tg.mod(1), _layer(layer_gain, l), _layer(w_in, l)] + rope_specs
        + [_layer(q_gain, l), _layer(k_gain, l), _resident(gsum.shape)],
        out_specs=[tg.tok(w) for w in out_widths],
        out_shape=[jax.ShapeDtypeStruct(tg.shape + (w,), MXU_DTYPE) for w in out_widths],
        compiler_params=_params(2),
        name="in_projection_ctx" if is_ctx else "in_projection",
    )(tg.view(stream), mod3, mod3, layer_gain, w_in, *rope_args, q_gain, k_gain, gsum)
    return tuple(tg.unview(o) for o in outs)


def _retention_kernel(*refs, with_ctx_out):
    logit_ref, qk_c, v_c, g_c, qk_l, v_l, g_l = refs[:7]
    if with_ctx_out:
        o_c, o_l, dec_ref, st_ref = refs[7:]
    else:
        o_c = None
        o_l, dec_ref, st_ref = refs[7:]
    c_len = RET_CHUNK
    n_ctx_chunks = qk_c.shape[1] // c_len
    n_lat_chunks = qk_l.shape[1] // c_len
    n_chunks = n_ctx_chunks + n_lat_chunks
    heads = RET_HEADS
    pairs = heads // 2
    t_mask, t_dkf, t_dkb, t_dsf, t_dsb = (k * heads for k in range(5))
    t_dqf, t_dqb = 5 * heads, 5 * heads + pairs
    cdt = qk_l.dtype

    row = lax.broadcasted_iota(jnp.int32, (c_len, LANES), 0)
    lane = lax.broadcasted_iota(jnp.int32, (c_len, LANES), 1)
    lane_lo = lane < HEAD_DIM
    row_lo = row < HEAD_DIM

    @pl.when(pl.program_id(0) == 0)
    def _build_tables():
        rowf = row.astype(F32)
        lanef = lane.astype(F32)
        rel = rowf - lanef
        lg = [[jnp.broadcast_to(jax.nn.log_sigmoid(logit_ref[dd, hh])[0:1, :], (c_len, LANES))
               for hh in range(heads)] for dd in range(2)]
        for hh in range(heads):
            lf, lb = lg[0][hh], lg[1][hh]
            fwd = jnp.where(rel >= 0, jnp.exp(lf * jnp.maximum(rel, 0.0)), 0.0)
            bwd = jnp.where(rel < 0, jnp.exp(lb * jnp.maximum(-rel, 0.0)), 0.0)
            dec_ref[t_mask + hh] = fwd + bwd
            dec_ref[t_dkf + hh] = jnp.exp(lf * (c_len - 1.0 - lanef))
            dec_ref[t_dkb + hh] = jnp.exp(lb * lanef)
            dec_ref[t_dsf + hh] = jnp.exp(lf * float(c_len))
            dec_ref[t_dsb + hh] = jnp.exp(lb * float(c_len))
        for p in range(pairs):
            lf = jnp.where(lane_lo, lg[0][2 * p], lg[0][2 * p + 1])
            lb = jnp.where(lane_lo, lg[1][2 * p], lg[1][2 * p + 1])
            dec_ref[t_dqf + p] = jnp.exp(lf * (rowf + 1.0))
            dec_ref[t_dqb + p] = jnp.exp(lb * (float(c_len) - rowf))

    def summaries(qk_ref, v_ref, rows, c):
        for p in range(pairs):
            kt = qk_ref[0, rows, 256 + p * LANES:256 + (p + 1) * LANES].astype(F32).T
            for half in range(2):
                hh = 2 * p + half
                ktm = jnp.where(row_lo if half == 0 else jnp.logical_not(row_lo), kt, 0.0)
                lhs = jnp.concatenate([ktm * dec_ref[t_dkf + hh], ktm * dec_ref[t_dkb + hh]], axis=0)
                st_ref[c, hh] = _dot(lhs.astype(cdt), v_ref[0, rows, hh * LANES:(hh + 1) * LANES])

    def outputs(qk_ref, v_ref, g_ref, o_ref, rows, c):
        zero = jnp.zeros((), cdt)
        for p in range(pairs):
            q = qk_ref[0, rows, p * LANES:(p + 1) * LANES]
            k = qk_ref[0, rows, 256 + p * LANES:256 + (p + 1) * LANES]
            qs = jnp.concatenate([jnp.where(lane_lo, q, zero), jnp.where(lane_lo, zero, q)], axis=0)
            a = _dot_nt(qs, k)
            q32 = q.astype(F32)
            qf = (q32 * dec_ref[t_dqf + p]).astype(cdt)
            qb = (q32 * dec_ref[t_dqb + p]).astype(cdt)
            for half in range(2):
                hh = 2 * p + half
                sl = slice(hh * LANES, (hh + 1) * LANES)
                pm = (a[half * c_len:(half + 1) * c_len] * dec_ref[t_mask + hh]).astype(cdt)
                lhs = jnp.concatenate([pm, qf, qb], axis=1)
                rhs = jnp.concatenate([v_ref[0, rows, sl], st_ref[c, hh].astype(cdt)], axis=0)
                y = _rms(_dot(lhs, rhs)) * g_ref[0, rows, sl].astype(F32)
                o_ref[0, rows, sl] = y.astype(o_ref.dtype)

    def lat_rows(i):
        return pl.ds(pl.multiple_of(i * c_len, c_len), c_len)

    for ci in range(n_ctx_chunks):
        summaries(qk_c, v_c, slice(ci * c_len, (ci + 1) * c_len), ci)

    def lat_summaries(i, carry):
        summaries(qk_l, v_l, lat_rows(i), n_ctx_chunks + i)
        return carry

    lax.fori_loop(0, n_lat_chunks, lat_summaries, 0, unroll=True)

    fwd_order = list(range(n_chunks))
    bwd_order = list(range(n_ctx_chunks - 1, -1, -1)) + list(range(n_chunks - 1, n_ctx_chunks - 1, -1))
    for hh in range(heads):
        for order, lo, t_ds in ((fwd_order, 0, t_dsf), (bwd_order, c_len, t_dsb)):
            decay = dec_ref[t_ds + hh]
            state = jnp.zeros((c_len, LANES), F32)
            for c in order:
                z = st_ref[c, hh, lo:lo + c_len, :]
                st_ref[c, hh, lo:lo + c_len, :] = state
                state = state * decay + z

    if with_ctx_out:
        for ci in range(n_ctx_chunks):
            outputs(qk_c, v_c, g_c, o_c, slice(ci * c_len, (ci + 1) * c_len), ci)

    def lat_outputs(i, carry):
        outputs(qk_l, v_l, g_l, o_l, lat_rows(i), n_ctx_chunks + i)
        return carry

    lax.fori_loop(0, n_lat_chunks, lat_outputs, 0, unroll=True)


def _retention(l, logit_tile, ctx_qvg, lat_qvg, with_ctx_out):
    b, n_ctx, _ = ctx_qvg[0].shape
    n_lat = lat_qvg[0].shape[1]
    n_tables = 5 * RET_HEADS + RET_HEADS
    out_shape = [jax.ShapeDtypeStruct((b, n_lat, HEAD_W), MXU_DTYPE)]
    if with_ctx_out:
        out_shape.insert(0, jax.ShapeDtypeStruct((b, n_ctx, HEAD_W), MXU_DTYPE))
    outs = pl.pallas_call(
        functools.partial(_retention_kernel, with_ctx_out=with_ctx_out),
        grid=(b,),
        in_specs=[_layer(logit_tile, l)] + [_batch_block(a) for a in (*ctx_qvg, *lat_qvg)],
        out_specs=[_batch_block(s) for s in out_shape],
        out_shape=out_shape,
        scratch_shapes=[pltpu.VMEM((n_tables, RET_CHUNK, LANES), F32),
                        pltpu.VMEM(((n_ctx + n_lat) // RET_CHUNK, RET_HEADS, 2 * RET_CHUNK, LANES), F32)],
        compiler_params=_params(1),
        name="retention",
    )(logit_tile, *ctx_qvg, *lat_qvg)
    return (outs[0], outs[1]) if with_ctx_out else (None, outs[0])


def _na_kernel(*refs, with_ctx_out):
    q_c, k_c, v_c, q_l, k_l, v_l, bias_ref, bias_shift_ref, bias_pad_ref = refs[:9]
    if with_ctx_out:
        o_c, o_l, s_ref, p_ref = refs[9:]
    else:
        o_l, s_ref, p_ref = refs[9:]
    n_ctx = k_c.shape[1]
    n_rows = q_l.shape[1] // GRID_W
    n_blocks = n_rows // NA_BLOCK_ROWS
    qb = NA_BLOCK_ROWS * GRID_W
    un = NA_UNION_ROWS * GRID_W
    win = NA_ROWS * GRID_W
    half_win = NA_ROWS // 2
    cdt = p_ref.dtype

    def stacked(q):
        lo = lax.broadcasted_iota(jnp.int32, q.shape, 1) < HEAD_DIM
        zero = jnp.zeros((), q.dtype)
        return jnp.concatenate([jnp.where(lo, q, zero), jnp.where(lo, zero, q)], axis=0)

    def unstack(o):
        m = o.shape[0] // 2
        lo = lax.broadcasted_iota(jnp.int32, (m, LANES), 1) < HEAD_DIM
        o = o[:, 0:LANES] / o[:, LANES:]
        return jnp.where(lo, o[0:m], o[m:])

    def softmax(parts):
        m = parts[0].max(axis=-1, keepdims=True)
        for s in parts[1:]:
            m = jnp.maximum(m, s.max(axis=-1, keepdims=True))
        return [jnp.exp2(s - m) for s in parts]

    def values(v_ref, rows, p):
        return v_ref[0, rows, 2 * p * LANES:(2 * p + 2) * LANES]

    def block(g, config):
        if config == "first":
            u0 = 0
        elif config == "last":
            u0 = n_rows - NA_UNION_ROWS
        else:
            u0 = g * NA_BLOCK_ROWS - half_win
        q_rows = pl.ds(pl.multiple_of(g * qb, qb), qb)
        k_rows = pl.ds(pl.multiple_of(u0 * GRID_W, GRID_W), un)

        def scores(p):
            sl = slice(p * LANES, (p + 1) * LANES)
            qs = stacked(q_l[0, q_rows, sl])
            s_ref[p % 2, :, 0:un] = _dot_nt(qs, k_l[0, k_rows, sl])
            s_ref[p % 2, :, un:] = _dot_nt(qs, k_c[0, :, sl])

        scores(0)
        for p in range(NA_HEADS // 2):
            sl = slice(p * LANES, (p + 1) * LANES)
            pbuf, sbuf = p_ref.at[p % 2], s_ref.at[p % 2]
            if p + 1 < NA_HEADS // 2:
                scores(p + 1)
            for half in range(2):
                hh = 2 * p + half
                for a in range(NA_BLOCK_ROWS):
                    rows = slice(half * qb + a * GRID_W, half * qb + (a + 1) * GRID_W)
                    if config == "first":
                        w, off = 0, NA_ROWS - 1 - a
                    elif config == "last":
                        w, off = (n_rows - NA_ROWS - u0) * GRID_W, half_win - 1 - a
                    else:
                        w, off = a * GRID_W, half_win - 1
                    if w % LANES:
                        lo, width, bias = w - GRID_W, win + 2 * GRID_W, bias_pad_ref[hh]
                    elif off % 2:
                        lo, width, bias = w, win, bias_shift_ref[hh, :, (off - 1) * GRID_W:(off - 1) * GRID_W + win]
                    else:
                        lo, width, bias = w, win, bias_ref[hh, :, off * GRID_W:off * GRID_W + win]
                    p_loc, p_cx = softmax([sbuf[rows, lo:lo + width] + bias, sbuf[rows, un:]])
                    pieces = [p_loc, jnp.zeros((GRID_W, un - lo - width), F32), p_cx]
                    if lo:
                        pieces = [jnp.zeros((GRID_W, lo), F32)] + pieces
                    pbuf[rows, :] = jnp.concatenate([x for x in pieces if x.shape[1]], axis=1).astype(cdt)
            o = _dot(pbuf[:, 0:un], values(v_l, k_rows, p)) + _dot(pbuf[:, un:], values(v_c, slice(None), p))
            o_l[0, q_rows, sl] = unstack(o).astype(o_l.dtype)

    block(0, "first")

    def middle(g, carry):
        block(g, "middle")
        return carry

    lax.fori_loop(1, n_blocks - 1, middle, 0, unroll=True)
    block(n_blocks - 1, "last")

    if with_ctx_out:
        for p in range(NA_HEADS // 2):
            sl = slice(p * LANES, (p + 1) * LANES)
            (pn,) = softmax([_dot_nt(stacked(q_c[0, :, sl]), k_c[0, :, sl])])
            o = _dot(pn.astype(cdt), values(v_c, slice(None), p))
            o_c[0, :, sl] = unstack(o).astype(o_c.dtype)


def _neighbourhood(l, ctx_qkv, lat_qkv, bias_tables, with_ctx_out):
    b, n_ctx, _ = ctx_qkv[0].shape
    n_lat = lat_qkv[0].shape[1]
    p_cols = NA_UNION_ROWS * GRID_W + n_ctx
    stacked_rows = 2 * NA_BLOCK_ROWS * GRID_W
    out_shape = [jax.ShapeDtypeStruct((b, n_lat, HEAD_W), MXU_DTYPE)]
    if with_ctx_out:
        out_shape.insert(0, jax.ShapeDtypeStruct((b, n_ctx, HEAD_W), MXU_DTYPE))
    outs = pl.pallas_call(
        functools.partial(_na_kernel, with_ctx_out=with_ctx_out),
        grid=(b,),
        in_specs=[_batch_block(a) for a in (*ctx_qkv, *lat_qkv)] + [_layer(tbl, l) for tbl in bias_tables],
        out_specs=[_batch_block(s) for s in out_shape],
        out_shape=out_shape,
        scratch_shapes=[pltpu.VMEM((2, stacked_rows, p_cols), F32),
                        pltpu.VMEM((2, stacked_rows, p_cols), MXU_DTYPE)],
        compiler_params=_params(1),
        name="neighbourhood_attention",
    )(*ctx_qkv, *lat_qkv, *bias_tables)
    return (outs[0], outs[1]) if with_ctx_out else (None, outs[0])


def _na_bias_table(rel_bias):
    col = np.arange(GRID_W)
    col_start = np.clip(col - NA_COLS // 2, 0, GRID_W - NA_COLS)
    in_window = (col[None, :] >= col_start[:, None]) & (col[None, :] < col_start[:, None] + NA_COLS)
    dcol = np.clip(col[None, :] - col[:, None], 1 - NA_COLS, NA_COLS - 1) + NA_COLS - 1
    onehot = (dcol[None] == np.arange(2 * NA_COLS - 1)[:, None, None]) & in_window[None]
    rest = rel_bias.astype(F32) * LOG2_E
    cb = None
    for _ in range(3):
        piece = rest.astype(MXU_DTYPE)
        rest = rest - piece.astype(F32)
        part = jnp.einsum("lhdc,cqk->lhqdk", piece, jnp.asarray(onehot, MXU_DTYPE), preferred_element_type=F32)
        cb = part if cb is None else cb + part
    cb = jnp.where(jnp.asarray(in_window)[:, None, :], cb, NEG_INF)
    table = cb.reshape(cb.shape[:3] + ((2 * NA_ROWS - 1) * GRID_W,))
    mid = (NA_ROWS // 2 - 1) * GRID_W
    side = jnp.full(cb.shape[:3] + (GRID_W,), NEG_INF, F32)
    padded = jnp.concatenate([side, table[..., mid:mid + NA_ROWS * GRID_W], side], axis=-1)
    return table, table[..., GRID_W:], padded


def _gqa_kernel(*refs, chunks):
    q_ref, kv_refs, (o_ref, s_ref, p_ref) = refs[0], refs[1:-3], refs[-3:]
    tq = q_ref.shape[1]
    n_kv = GQA_KV_HEADS
    rb = GQA_ROW_BLOCK
    n_blk = GQA_Q_HEADS // n_kv * tq // rb
    per_head = tq // rb
    lo = lax.broadcasted_iota(jnp.int32, (tq, LANES), 1) < HEAD_DIM
    zero = jnp.zeros((), q_ref.dtype)

    def stacked_queries(g):
        parts = []
        for j in range(2 * g, 2 * g + 2):
            q = q_ref[0, :, j * LANES:(j + 1) * LANES]
            parts += [jnp.where(lo, q, zero), jnp.where(lo, zero, q)]
        return jnp.concatenate(parts, axis=0)

    items = [(g, ci) for g in range(n_kv) for ci in range(len(chunks))]
    qs = [stacked_queries(g) for g in range(n_kv)]

    def rows_of(ci, lane_tile):
        pieces = [kv_refs[src][0, c0:c1, lane_tile * LANES:(lane_tile + 1) * LANES] for src, c0, c1 in chunks[ci]]
        return pieces[0] if len(pieces) == 1 else jnp.concatenate(pieces, axis=0)

    def scores(idx):
        g, ci = items[idx]
        s_ref[idx % 2, :, 0:widths[ci]] = _dot_nt(qs[g], rows_of(ci, g))

    widths = [sum(c1 - c0 for _, c0, c1 in pieces) for pieces in chunks]
    scores(0)
    m = acc = None
    for idx, (g, ci) in enumerate(items):
        w, slot = widths[ci], idx % 2
        if idx + 1 < len(items):
            scores(idx + 1)
        if ci == 0:
            m, acc = [None] * n_blk, [None] * n_blk
        alpha = [None] * n_blk
        for i in range(n_blk):
            rows = slice(i * rb, (i + 1) * rb)
            s = s_ref[slot, rows, 0:w]
            s_max = s.max(axis=-1, keepdims=True)
            m_new = s_max if ci == 0 else jnp.maximum(m[i], s_max)
            if ci:
                alpha[i] = jnp.exp2(m[i] - m_new)
            m[i] = m_new
            p_ref[slot, rows, 0:w] = jnp.exp2(s - m_new).astype(p_ref.dtype)
        pv = _dot(p_ref[slot, :, 0:w], rows_of(ci, n_kv + g))
        for i in range(n_blk):
            part = pv[i * rb:(i + 1) * rb]
            acc[i] = part if ci == 0 else alpha[i] * acc[i] + part
        if ci + 1 == len(chunks):
            for j in range(2):
                halves = []
                for half in range(2):
                    h = 2 * j + half
                    a = jnp.concatenate(acc[h * per_head:(h + 1) * per_head], axis=0)
                    a_sw = pltpu.roll(a, HEAD_DIM, 1)
                    halves.append(a / a_sw if half == 0 else a_sw / a)
                pair = 2 * g + j
                o_ref[0, :, pair * LANES:(pair + 1) * LANES] = jnp.where(lo, halves[0], halves[1]).astype(o_ref.dtype)


def _gqa(gq, kv_sources):
    b, n_q, _ = gq.shape
    tq = min(GQA_Q_TILE, n_q)
    starts = np.cumsum([0] + [kv.shape[1] for kv in kv_sources])
    chunks = []
    for c0 in range(0, int(starts[-1]), GQA_KEY_CHUNK):
        c1 = min(c0 + GQA_KEY_CHUNK, int(starts[-1]))
        pieces = []
        for src in range(len(kv_sources)):
            lo, hi = max(c0, int(starts[src])), min(c1, int(starts[src + 1]))
            if lo < hi:
                pieces.append((src, lo - int(starts[src]), hi - int(starts[src])))
        chunks.append(tuple(pieces))
    stacked_rows = GQA_Q_HEADS // GQA_KV_HEADS * tq
    width = min(GQA_KEY_CHUNK, int(starts[-1]))
    return pl.pallas_call(
        functools.partial(_gqa_kernel, chunks=tuple(chunks)),
        grid=(b, n_q // tq),
        in_specs=[pl.BlockSpec((1, tq, HEAD_W), lambda i, j: (i, j, 0))] + [_batch_block(kv) for kv in kv_sources],
        out_specs=pl.BlockSpec((1, tq, HEAD_W), lambda i, j: (i, j, 0)),
        out_shape=jax.ShapeDtypeStruct((b, n_q, HEAD_W), MXU_DTYPE),
        scratch_shapes=[pltpu.VMEM((2, stacked_rows, width), F32),
                        pltpu.VMEM((2, stacked_rows, width), MXU_DTYPE)],
        compiler_params=_params(2),
        name="gqa_attention",
    )(gq, *kv_sources)


def _post_kernel(yr_ref, yn_ref, yg_ref, gate_ref, s_ref, gt1_ref, sh2_ref, sc2_ref, gt2_ref,
                 gpost_ref, gpre_ref, gpost2_ref, wr_ref, wn_ref, wg_ref, wout_ref, w1_ref, w2_ref, o_ref):
    tm, d = o_ref.shape[1:]

    def mixed(rows):
        y = None
        for i, (br, w_ref) in enumerate(((yr_ref, wr_ref), (yn_ref, wn_ref), (yg_ref, wg_ref))):
            z = _dot(br[0, rows, :], w_ref[...]) * gate_ref[0, rows, i * d:(i + 1) * d].astype(F32)
            y = z if y is None else y + z
        return _dot(y.astype(MXU_DTYPE), wout_ref[...])

    def mlp(rows, y):
        x = s_ref[0, rows, :] + gt1_ref[0] * (_rms(y) * gpost_ref[...])
        h = _rms(x) * gpre_ref[...]
        h = (h * (1.0 + sc2_ref[0]) + sh2_ref[0]).astype(MXU_DTYPE)
        acc = None
        for j in range(D_FF // d):
            u = jnp.maximum(_dot(h, w1_ref[:, j * d:(j + 1) * d]), 0.0)
            part = _dot((u * u).astype(MXU_DTYPE), w2_ref[j * d:(j + 1) * d, :])
            acc = part if acc is None else acc + part
        o_ref[0, rows, :] = x + gt2_ref[0] * (_rms(acc) * gpost2_ref[...])

    groups = [slice(r0, r0 + POST_ROW_GROUP) for r0 in range(0, tm, POST_ROW_GROUP)]
    ys = [mixed(rows) for rows in groups]
    for rows, y in zip(groups, ys):
        mlp(rows, y)


def _post(l, y_ret, y_na, y_gqa, gates, stream, is_ctx, mod3, mod_rows, gains, weights):
    tg = _TokenGrid(stream, is_ctx, l, mod_rows)
    d = tg.d
    out = pl.pallas_call(
        _post_kernel,
        grid=tg.grid,
        in_specs=[tg.tok(HEAD_W), tg.tok(HEAD_W), tg.tok(HEAD_W), tg.tok(3 * d), tg.tok(d)]
        + [tg.mod(chunk) for chunk in (2, 3, 4, 5)]
        + [_layer(a, l) for a in gains] + [_layer(w, l) for w in weights],
        out_specs=tg.tok(d),
        out_shape=jax.ShapeDtypeStruct(tg.shape + (d,), F32),
        compiler_params=_params(2),
        name="merge_mlp_ctx" if is_ctx else "merge_mlp",
    )(tg.view(y_ret), tg.view(y_na), tg.view(y_gqa), tg.view(gates), tg.view(stream),
      mod3, mod3, mod3, mod3, *gains, *weights)
    return tg.unview(out)


def _rope_tables(n_latent):
    t = np.arange(n_latent)
    pos = np.stack([t // GRID_W, t % GRID_W], axis=-1).astype(np.float64)
    n_freq = HEAD_DIM // 4
    inv_freq = ROPE_BASE ** (-np.arange(n_freq, dtype=np.float64) / n_freq)
    lane = np.arange(LANES) % HEAD_DIM
    axis, second, freq = lane // 32, (lane % 32) // 16, lane % 16
    ang = pos[:, axis] * inv_freq[freq][None, :]
    return jnp.asarray(np.cos(ang), F32), jnp.asarray(np.sin(ang) * np.where(second == 1, 1.0, -1.0), F32)


def kernel(x, c, ctx, c_ctx, w_mod, b_mod, g_pre_mix, g_post_mix, g_pre_mlp, g_post_mlp, w_in, ret_decay_logit, na_rel_bias, gqa_q_norm, gqa_k_norm, w_br_ret, w_br_na, w_br_gqa, w_out, w_mlp_in, w_mlp_out):
    b, n, d = x.shape
    depth = w_mod.shape[0]
    cdt = MXU_DTYPE

    rope_tables = _rope_tables(n)
    lane_head = np.arange(2 * LANES) // HEAD_DIM
    gsum = jnp.asarray(lane_head[:, None] == lane_head[None, :], cdt)

    n_rows = -(-(b + 1) // 8) * 8
    cs = jnp.concatenate([c, jnp.zeros((n_rows - b - 1, d), F32), c_ctx[None, :]], axis=0)
    mod3 = _modulation(cs, w_mod, b_mod).reshape(depth * n_rows, 1, 6 * d)

    w_in_c = w_in.astype(cdt)
    post_weights = tuple(w.astype(cdt) for w in (w_br_ret, w_br_na, w_br_gqa, w_out, w_mlp_in, w_mlp_out))
    post_gains = tuple(g[:, None, :] for g in (g_post_mix, g_pre_mlp, g_post_mlp))
    pre_gain = g_pre_mix[:, None, :]
    q_gain = jnp.tile(gqa_q_norm, (1, 2))[:, None, :]
    k_gain = jnp.tile(gqa_k_norm, (1, 2))[:, None, :]
    logit_tile = jnp.broadcast_to(ret_decay_logit.astype(F32)[..., None, None], (depth, 2, RET_HEADS, 8, LANES))
    bias_tables = _na_bias_table(na_rel_bias)

    ctx_s, lat_s = ctx, x
    for l in range(depth):
        last = l == depth - 1
        pc = _in_projection(l, ctx_s, True, mod3, n_rows, pre_gain, w_in_c, rope_tables, q_gain, k_gain, gsum)
        pt = _in_projection(l, lat_s, False, mod3, n_rows, pre_gain, w_in_c, rope_tables, q_gain, k_gain, gsum)
        ret_c, ret_l = _retention(l, logit_tile, pc[0:3], pt[0:3], not last)
        na_c, na_l = _neighbourhood(l, pc[3:6], pt[3:6], bias_tables, not last)
        gqa_l = _gqa(pt[6], (pc[7], pt[7]))
        lat_s = _post(l, ret_l, na_l, gqa_l, pt[8], lat_s, False, mod3, n_rows, post_gains, post_weights)
        if not last:
            gqa_c = _gqa(pc[6], (pc[7],))
            ctx_s = _post(l, ret_c, na_c, gqa_c, pc[8], ctx_s, True, mod3, n_rows, post_gains, post_weights)
    return lat_s
```

```python
import functools

import jax
import jax.numpy as jnp
import numpy as np
from jax import lax
from jax.experimental import pallas as pl
from jax.experimental.pallas import tpu as pltpu

F32 = jnp.float32
MXU_DTYPE = jnp.bfloat16

D_MODEL = 1024
GRID_W = 64
CTX_LEN = 256
HEAD_DIM = 64
RET_HEADS = 4
RET_CHUNK = 128
NA_HEADS = 8
NA_ROWS = 8
NA_COLS = 16
NA_BLOCK_ROWS = 4
NA_UNION_ROWS = NA_BLOCK_ROWS + NA_ROWS
GQA_Q_HEADS = 8
GQA_KV_HEADS = 2
GQA_Q_TILE = 256
GQA_KEY_CHUNK = 768
GQA_ROW_BLOCK = 64
D_FF = 4 * D_MODEL
ROPE_BASE = 10000.0
NORM_EPS = 1e-6
NEG_INF = -1e30
LOG2_E = 1.4426950408889634

LANES = 128
HEAD_W = 512
C_RQ, C_RK, C_RV, C_RG = 0, 256, 512, 1024
C_NQ, C_NK, C_NV = 1536, 2048, 2560
C_GQ, C_GK, C_GV = 3072, 3584, 3712
C_GATE = 3840
IN_W = 6912

TOKEN_TILE = 512
POST_ROW_GROUP = 256
VMEM_LIMIT = 56 * 1024 * 1024


def _rms(x):
    return x * lax.rsqrt(jnp.mean(x * x, axis=-1, keepdims=True) + NORM_EPS)


def _dot(a, b):
    return jnp.dot(a, b, preferred_element_type=F32)


def _dot_nt(a, b):
    return lax.dot_general(a, b, (((1,), (1,)), ((), ())), preferred_element_type=F32)


def _resident(shape):
    return pl.BlockSpec(shape, lambda *_: (0,) * len(shape), pipeline_mode=pl.Buffered(1))


def _layer(arr, l):
    return pl.BlockSpec((None,) + arr.shape[1:], lambda *_: (l,) + (0,) * (arr.ndim - 1),
                        pipeline_mode=pl.Buffered(1))


def _batch_block(arr):
    return pl.BlockSpec((1,) + arr.shape[1:], lambda i, *_: (i, 0, 0))


def _params(n_axes):
    return pltpu.CompilerParams(dimension_semantics=("arbitrary",) * n_axes,
                                vmem_limit_bytes=VMEM_LIMIT)


def _mod_kernel(c_ref, w_ref, b_ref, o_ref):
    c = c_ref[...]
    s = (c * jax.nn.sigmoid(c)).astype(MXU_DTYPE)
    o_ref[0] = _dot(s, w_ref[0].astype(MXU_DTYPE)) + b_ref[0]


def _modulation(cs, w_mod, b_mod):
    depth, d, n = w_mod.shape
    r = cs.shape[0]
    tn = 1024
    return pl.pallas_call(
        _mod_kernel,
        grid=(depth, n // tn),
        in_specs=[pl.BlockSpec((r, d), lambda l, j: (0, 0)),
                  pl.BlockSpec((1, d, tn), lambda l, j: (l, 0, j)),
                  pl.BlockSpec((1, 1, tn), lambda l, j: (l, 0, j))],
        out_specs=pl.BlockSpec((1, r, tn), lambda l, j: (l, 0, j)),
        out_shape=jax.ShapeDtypeStruct((depth, r, n), F32),
        compiler_params=_params(2),
        name="adaln_modulation",
    )(cs, w_mod, b_mod.reshape(depth, 1, n))


class _TokenGrid:
    def __init__(self, stream, is_ctx, layer, mod_rows):
        b, n, self.d = stream.shape
        self.is_ctx = is_ctx
        self.shape = (1, b * n) if is_ctx else (b, n)
        self.tm = min(TOKEN_TILE, self.shape[1])
        self.grid = (self.shape[0], self.shape[1] // self.tm)
        self.orig = (b, n)
        self.mod_base = layer * mod_rows + (mod_rows - 1 if is_ctx else 0)

    def view(self, arr):
        return arr.reshape(self.shape + arr.shape[2:])

    def unview(self, arr):
        return arr.reshape(self.orig + arr.shape[2:])

    def tok(self, width):
        return pl.BlockSpec((1, self.tm, width), lambda i, j: (i, j, 0))

    def mod(self, chunk):
        base, per_batch = self.mod_base, 0 if self.is_ctx else 1
        return pl.BlockSpec((1, 1, self.d), lambda i, j: (base + per_batch * i, 0, chunk))

    def cast_specs(self, stacked, layer):
        steps = self.grid[0] * self.grid[1]
        rows, cols = stacked.shape[1:]
        slab = rows // steps
        assert slab * steps == rows and slab % 16 == 0, (rows, steps)
        per_row = self.grid[1]
        return (pl.BlockSpec((None, slab, cols), lambda i, j: (layer, i * per_row + j, 0)),
                pl.BlockSpec((slab, cols), lambda i, j: (i * per_row + j, 0)),
                jax.ShapeDtypeStruct((rows, cols), MXU_DTYPE))


def _cast_slabs(src_refs, dst_refs):
    for src, dst in zip(src_refs, dst_refs):
        dst[...] = src[...].astype(dst.dtype)


def _inproj_kernel(*refs, with_rope, n_cast):
    x_ref, sh_ref, sc_ref, g_ref, w_ref = refs[:5]
    if with_rope:
        cos_ref, sin_ref = refs[5:7]
        refs = refs[7:]
    else:
        refs = refs[5:]
    qg_ref, kg_ref, gsum_ref = refs[:3]
    cast_src, refs = refs[3:3 + n_cast], refs[3 + n_cast:]
    rqk_ref, rv_ref, rg_ref, nq_ref, nk_ref, nv_ref, gq_ref, gkv_ref, gate_ref = refs[:9]
    _cast_slabs(cast_src, refs[9:])

    x = x_ref[0]
    h = _rms(x) * g_ref[...]
    h = h * (1.0 + sc_ref[0]) + sh_ref[0]
    hb = h.astype(MXU_DTYPE)
    lane = lax.broadcasted_iota(jnp.int32, (x.shape[0], LANES), 1)
    gsum = gsum_ref[...]

    def proj(col, width):
        return _dot(hb, w_ref[:, col:col + width])

    if with_rope:
        cos, sin = cos_ref[...], sin_ref[...]
        first_half = (lane % 32) < 16

        def rope(v):
            partner = jnp.where(first_half, pltpu.roll(v, LANES - 16, 1), pltpu.roll(v, 16, 1))
            return v * cos + partner * sin
    else:
        def rope(v):
            return v

    def head_rms(v):
        ms = _dot((v * v).astype(MXU_DTYPE), gsum) * (1.0 / HEAD_DIM)
        return v * lax.rsqrt(ms + NORM_EPS)

    odt = rqk_ref.dtype
    ret_scale = HEAD_DIM ** -0.5
    att_scale = HEAD_DIM ** -0.5 * LOG2_E

    rq = proj(C_RQ, 256)
    rk = proj(C_RK, 256)
    for j in range(2):
        sl = slice(j * LANES, (j + 1) * LANES)
        rqk_ref[0, :, j * LANES:(j + 1) * LANES] = rope(rq[:, sl]).astype(odt)
        rqk_ref[0, :, 256 + j * LANES:256 + (j + 1) * LANES] = (rope(rk[:, sl]) * ret_scale).astype(odt)
    rv_ref[0] = proj(C_RV, HEAD_W).astype(odt)
    rg = proj(C_RG, HEAD_W)
    rg_ref[0] = (rg * jax.nn.sigmoid(rg)).astype(odt)
    nq_ref[0] = (proj(C_NQ, HEAD_W) * att_scale).astype(odt)
    nk_ref[0] = proj(C_NK, HEAD_W).astype(odt)
    nv = proj(C_NV, HEAD_W).astype(odt)
    for j in range(NA_HEADS // 2):
        nv_ref[0, :, 2 * j * LANES:(2 * j + 1) * LANES] = nv[:, j * LANES:(j + 1) * LANES]
        nv_ref[0, :, (2 * j + 1) * LANES:(2 * j + 2) * LANES] = jnp.ones((nv.shape[0], LANES), odt)

    gq = proj(C_GQ, HEAD_W)
    qg = qg_ref[...]
    for j in range(2):
        normed = head_rms(gq[:, 2 * j * LANES:(2 * j + 2) * LANES])
        for i in range(2):
            sl = slice((2 * j + i) * LANES, (2 * j + i + 1) * LANES)
            gq_ref[0, :, sl] = (rope(normed[:, i * LANES:(i + 1) * LANES] * qg) * att_scale).astype(odt)
    gkv = proj(C_GK, 2 * LANES)
    gk = rope(head_rms(gkv)[:, 0:LANES] * kg_ref[...])
    gv = gkv[:, LANES:]
    low_half = lane < HEAD_DIM
    gk_sw, gv_sw = pltpu.roll(gk, HEAD_DIM, 1), pltpu.roll(gv, HEAD_DIM, 1)
    gkv_ref[0, :, 0:LANES] = jnp.where(low_half, gk, gk_sw).astype(odt)
    gkv_ref[0, :, LANES:2 * LANES] = jnp.where(low_half, gk_sw, gk).astype(odt)
    gkv_ref[0, :, 2 * LANES:3 * LANES] = jnp.where(low_half, gv, 1.0).astype(odt)
    gkv_ref[0, :, 3 * LANES:4 * LANES] = jnp.where(low_half, gv_sw, 1.0).astype(odt)

    for j in range(6):
        sl = slice(j * HEAD_W, (j + 1) * HEAD_W)
        gate_ref[0, :, sl] = jax.nn.sigmoid(proj(C_GATE + j * HEAD_W, HEAD_W)).astype(odt)


def _in_projection(l, stream, is_ctx, mod3, mod_rows, layer_gain, w_in, rope_tables, q_gain, k_gain, gsum, cast=()):
    tg = _TokenGrid(stream, is_ctx, l, mod_rows)
    d = tg.d
    out_widths = [HEAD_W] * 5 + [2 * HEAD_W] + [HEAD_W] * 2 + [3 * d]
    rope_specs = [] if is_ctx else [pl.BlockSpec((tg.tm, LANES), lambda i, j: (j, 0))] * 2
    rope_args = () if is_ctx else rope_tables
    cast_specs = [tg.cast_specs(a, l) for a in cast]
    outs = pl.pallas_call(
        functools.partial(_inproj_kernel, with_rope=not is_ctx, n_cast=len(cast)),
        grid=tg.grid,
        in_specs=[tg.tok(d), tg.mod(0), tg.mod(1), _layer(layer_gain, l), _resident(w_in.shape)] + rope_specs
        + [_layer(q_gain, l), _layer(k_gain, l), _resident(gsum.shape)] + [s[0] for s in cast_specs],
        out_specs=[tg.tok(w) for w in out_widths] + [s[1] for s in cast_specs],
        out_shape=[jax.ShapeDtypeStruct(tg.shape + (w,), MXU_DTYPE) for w in out_widths] + [s[2] for s in cast_specs],
        compiler_params=_params(2),
        name="in_projection_ctx" if is_ctx else "in_projection",
    )(tg.view(stream), mod3, mod3, layer_gain, w_in, *rope_args, q_gain, k_gain, gsum, *cast)
    return tuple(tg.unview(o) for o in outs[:len(out_widths)]) + tuple(outs[len(out_widths):])


def _retention_kernel(*refs, with_ctx_out):
    logit_ref, qk_c, v_c, g_c, qk_l, v_l, g_l = refs[:7]
    if with_ctx_out:
        o_c, o_l, dec_ref, st_ref = refs[7:]
    else:
        o_c = None
        o_l, dec_ref, st_ref = refs[7:]
    c_len = RET_CHUNK
    n_ctx_chunks = qk_c.shape[1] // c_len
    n_lat_chunks = qk_l.shape[1] // c_len
    n_chunks = n_ctx_chunks + n_lat_chunks
    heads = RET_HEADS
    pairs = heads // 2
    t_mask, t_dkf, t_dkb, t_dsf, t_dsb = (k * heads for k in range(5))
    t_dqf, t_dqb = 5 * heads, 5 * heads + pairs
    cdt = qk_l.dtype

    row = lax.broadcasted_iota(jnp.int32, (c_len, LANES), 0)
    lane = lax.broadcasted_iota(jnp.int32, (c_len, LANES), 1)
    lane_lo = lane < HEAD_DIM
    row_lo = row < HEAD_DIM

    @pl.when(pl.program_id(0) == 0)
    def _build_tables():
        rowf = row.astype(F32)
        lanef = lane.astype(F32)
        rel = rowf - lanef
        lg = [[jnp.broadcast_to(jax.nn.log_sigmoid(logit_ref[dd, hh])[0:1, :], (c_len, LANES))
               for hh in range(heads)] for dd in range(2)]
        for hh in range(heads):
            lf, lb = lg[0][hh], lg[1][hh]
            fwd = jnp.where(rel >= 0, jnp.exp(lf * jnp.maximum(rel, 0.0)), 0.0)
            bwd = jnp.where(rel < 0, jnp.exp(lb * jnp.maximum(-rel, 0.0)), 0.0)
            dec_ref[t_mask + hh] = fwd + bwd
            dec_ref[t_dkf + hh] = jnp.exp(lf * (c_len - 1.0 - lanef))
            dec_ref[t_dkb + hh] = jnp.exp(lb * lanef)
            dec_ref[t_dsf + hh] = jnp.exp(lf * float(c_len))
            dec_ref[t_dsb + hh] = jnp.exp(lb * float(c_len))
        for p in range(pairs):
            lf = jnp.where(lane_lo, lg[0][2 * p], lg[0][2 * p + 1])
            lb = jnp.where(lane_lo, lg[1][2 * p], lg[1][2 * p + 1])
            dec_ref[t_dqf + p] = jnp.exp(lf * (rowf + 1.0))
            dec_ref[t_dqb + p] = jnp.exp(lb * (float(c_len) - rowf))

    def summaries(qk_ref, v_ref, rows, c):
        for p in range(pairs):
            kt = qk_ref[0, rows, 256 + p * LANES:256 + (p + 1) * LANES].astype(F32).T
            for half in range(2):
                hh = 2 * p + half
                ktm = jnp.where(row_lo if half == 0 else jnp.logical_not(row_lo), kt, 0.0)
                lhs = jnp.concatenate([ktm * dec_ref[t_dkf + hh], ktm * dec_ref[t_dkb + hh]], axis=0)
                st_ref[c, hh] = _dot(lhs.astype(cdt), v_ref[0, rows, hh * LANES:(hh + 1) * LANES])

    def outputs(qk_ref, v_ref, g_ref, o_ref, rows, c):
        zero = jnp.zeros((), cdt)
        for p in range(pairs):
            q = qk_ref[0, rows, p * LANES:(p + 1) * LANES]
            k = qk_ref[0, rows, 256 + p * LANES:256 + (p + 1) * LANES]
            qs = jnp.concatenate([jnp.where(lane_lo, q, zero), jnp.where(lane_lo, zero, q)], axis=0)
            a = _dot_nt(qs, k)
            q32 = q.astype(F32)
            qf = (q32 * dec_ref[t_dqf + p]).astype(cdt)
            qb = (q32 * dec_ref[t_dqb + p]).astype(cdt)
            for half in range(2):
                hh = 2 * p + half
                sl = slice(hh * LANES, (hh + 1) * LANES)
                pm = (a[half * c_len:(half + 1) * c_len] * dec_ref[t_mask + hh]).astype(cdt)
                lhs = jnp.concatenate([pm, qf, qb], axis=1)
                rhs = jnp.concatenate([v_ref[0, rows, sl], st_ref[c, hh].astype(cdt)], axis=0)
                y = _rms(_dot(lhs, rhs)) * g_ref[0, rows, sl].astype(F32)
                o_ref[0, rows, sl] = y.astype(o_ref.dtype)

    def lat_rows(i):
        return pl.ds(pl.multiple_of(i * c_len, c_len), c_len)

    for ci in range(n_ctx_chunks):
        summaries(qk_c, v_c, slice(ci * c_len, (ci + 1) * c_len), ci)

    def lat_summaries(i, carry):
        summaries(qk_l, v_l, lat_rows(i), n_ctx_chunks + i)
        return carry

    lax.fori_loop(0, n_lat_chunks, lat_summaries, 0, unroll=True)

    fwd_order = list(range(n_chunks))
    bwd_order = list(range(n_ctx_chunks - 1, -1, -1)) + list(range(n_chunks - 1, n_ctx_chunks - 1, -1))
    for hh in range(heads):
        for order, lo, t_ds in ((fwd_order, 0, t_dsf), (bwd_order, c_len, t_dsb)):
            decay = dec_ref[t_ds + hh]
            state = jnp.zeros((c_len, LANES), F32)
            for c in order:
                z = st_ref[c, hh, lo:lo + c_len, :]
                st_ref[c, hh, lo:lo + c_len, :] = state
                state = state * decay + z

    if with_ctx_out:
        for ci in range(n_ctx_chunks):
            outputs(qk_c, v_c, g_c, o_c, slice(ci * c_len, (ci + 1) * c_len), ci)

    def lat_outputs(i, carry):
        outputs(qk_l, v_l, g_l, o_l, lat_rows(i), n_ctx_chunks + i)
        return carry

    lax.fori_loop(0, n_lat_chunks, lat_outputs, 0, unroll=True)


def _retention(l, logit_tile, ctx_qvg, lat_qvg, with_ctx_out):
    b, n_ctx, _ = ctx_qvg[0].shape
    n_lat = lat_qvg[0].shape[1]
    n_tables = 5 * RET_HEADS + RET_HEADS
    out_shape = [jax.ShapeDtypeStruct((b, n_lat, HEAD_W), MXU_DTYPE)]
    if with_ctx_out:
        out_shape.insert(0, jax.ShapeDtypeStruct((b, n_ctx, HEAD_W), MXU_DTYPE))
    outs = pl.pallas_call(
        functools.partial(_retention_kernel, with_ctx_out=with_ctx_out),
        grid=(b,),
        in_specs=[_layer(logit_tile, l)] + [_batch_block(a) for a in (*ctx_qvg, *lat_qvg)],
        out_specs=[_batch_block(s) for s in out_shape],
        out_shape=out_shape,
        scratch_shapes=[pltpu.VMEM((n_tables, RET_CHUNK, LANES), F32),
                        pltpu.VMEM(((n_ctx + n_lat) // RET_CHUNK, RET_HEADS, 2 * RET_CHUNK, LANES), F32)],
        compiler_params=_params(1),
        name="retention",
    )(logit_tile, *ctx_qvg, *lat_qvg)
    return (outs[0], outs[1]) if with_ctx_out else (None, outs[0])


def _na_kernel(*refs, with_ctx_out):
    q_c, k_c, v_c, q_l, k_l, v_l, bias_ref, bias_shift_ref, bias_pad_ref = refs[:9]
    if with_ctx_out:
        o_c, o_l, s_ref, p_ref = refs[9:]
    else:
        o_l, s_ref, p_ref = refs[9:]
    n_ctx = k_c.shape[1]
    n_rows = q_l.shape[1] // GRID_W
    n_blocks = n_rows // NA_BLOCK_ROWS
    qb = NA_BLOCK_ROWS * GRID_W
    un = NA_UNION_ROWS * GRID_W
    win = NA_ROWS * GRID_W
    half_win = NA_ROWS // 2
    cdt = p_ref.dtype

    def stacked(q):
        lo = lax.broadcasted_iota(jnp.int32, q.shape, 1) < HEAD_DIM
        zero = jnp.zeros((), q.dtype)
        return jnp.concatenate([jnp.where(lo, q, zero), jnp.where(lo, zero, q)], axis=0)

    def unstack(o):
        m = o.shape[0] // 2
        lo = lax.broadcasted_iota(jnp.int32, (m, LANES), 1) < HEAD_DIM
        o = o[:, 0:LANES] / o[:, LANES:]
        return jnp.where(lo, o[0:m], o[m:])

    def softmax(parts):
        m = parts[0].max(axis=-1, keepdims=True)
        for s in parts[1:]:
            m = jnp.maximum(m, s.max(axis=-1, keepdims=True))
        return [jnp.exp2(s - m) for s in parts]

    def values(v_ref, rows, p):
        return v_ref[0, rows, 2 * p * LANES:(2 * p + 2) * LANES]

    def block(g, config):
        if config == "first":
            u0 = 0
        elif config == "last":
            u0 = n_rows - NA_UNION_ROWS
        else:
            u0 = g * NA_BLOCK_ROWS - half_win
        q_rows = pl.ds(pl.multiple_of(g * qb, qb), qb)
        k_rows = pl.ds(pl.multiple_of(u0 * GRID_W, GRID_W), un)

        def scores(p):
            sl = slice(p * LANES, (p + 1) * LANES)
            qs = stacked(q_l[0, q_rows, sl])
            s_ref[p % 2, :, 0:un] = _dot_nt(qs, k_l[0, k_rows, sl])
            s_ref[p % 2, :, un:] = _dot_nt(qs, k_c[0, :, sl])

        scores(0)
        for p in range(NA_HEADS // 2):
            sl = slice(p * LANES, (p + 1) * LANES)
            pbuf, sbuf = p_ref.at[p % 2], s_ref.at[p % 2]
            if p + 1 < NA_HEADS // 2:
                scores(p + 1)
            for half in range(2):
                hh = 2 * p + half
                for a in range(NA_BLOCK_ROWS):
                    rows = slice(half * qb + a * GRID_W, half * qb + (a + 1) * GRID_W)
                    if config == "first":
                        w, off = 0, NA_ROWS - 1 - a
                    elif config == "last":
                        w, off = (n_rows - NA_ROWS - u0) * GRID_W, half_win - 1 - a
                    else:
                        w, off = a * GRID_W, half_win - 1
                    if w % LANES:
                        lo, width, bias = w - GRID_W, win + 2 * GRID_W, bias_pad_ref[hh]
                    elif off % 2:
                        lo, width, bias = w, win, bias_shift_ref[hh, :, (off - 1) * GRID_W:(off - 1) * GRID_W + win]
                    else:
                        lo, width, bias = w, win, bias_ref[hh, :, off * GRID_W:off * GRID_W + win]
                    p_loc, p_cx = softmax([sbuf[rows, lo:lo + width] + bias, sbuf[rows, un:]])
                    pieces = [p_loc, jnp.zeros((GRID_W, un - lo - width), F32), p_cx]
                    if lo:
                        pieces = [jnp.zeros((GRID_W, lo), F32)] + pieces
                    pbuf[rows, :] = jnp.concatenate([x for x in pieces if x.shape[1]], axis=1).astype(cdt)
            o = _dot(pbuf[:, 0:un], values(v_l, k_rows, p)) + _dot(pbuf[:, un:], values(v_c, slice(None), p))
            o_l[0, q_rows, sl] = unstack(o).astype(o_l.dtype)

    block(0, "first")

    def middle(g, carry):
        block(g, "middle")
        return carry

    lax.fori_loop(1, n_blocks - 1, middle, 0, unroll=True)
    block(n_blocks - 1, "last")

    if with_ctx_out:
        for p in range(NA_HEADS // 2):
            sl = slice(p * LANES, (p + 1) * LANES)
            (pn,) = softmax([_dot_nt(stacked(q_c[0, :, sl]), k_c[0, :, sl])])
            o = _dot(pn.astype(cdt), values(v_c, slice(None), p))
            o_c[0, :, sl] = unstack(o).astype(o_c.dtype)


def _neighbourhood(l, ctx_qkv, lat_qkv, bias_tables, with_ctx_out):
    b, n_ctx, _ = ctx_qkv[0].shape
    n_lat = lat_qkv[0].shape[1]
    p_cols = NA_UNION_ROWS * GRID_W + n_ctx
    stacked_rows = 2 * NA_BLOCK_ROWS * GRID_W
    out_shape = [jax.ShapeDtypeStruct((b, n_lat, HEAD_W), MXU_DTYPE)]
    if with_ctx_out:
        out_shape.insert(0, jax.ShapeDtypeStruct((b, n_ctx, HEAD_W), MXU_DTYPE))
    outs = pl.pallas_call(
        functools.partial(_na_kernel, with_ctx_out=with_ctx_out),
        grid=(b,),
        in_specs=[_batch_block(a) for a in (*ctx_qkv, *lat_qkv)] + [_layer(tbl, l) for tbl in bias_tables],
        out_specs=[_batch_block(s) for s in out_shape],
        out_shape=out_shape,
        scratch_shapes=[pltpu.VMEM((2, stacked_rows, p_cols), F32),
                        pltpu.VMEM((2, stacked_rows, p_cols), MXU_DTYPE)],
        compiler_params=_params(1),
        name="neighbourhood_attention",
    )(*ctx_qkv, *lat_qkv, *bias_tables)
    return (outs[0], outs[1]) if with_ctx_out else (None, outs[0])


def _na_bias_table(rel_bias):
    col = np.arange(GRID_W)
    col_start = np.clip(col - NA_COLS // 2, 0, GRID_W - NA_COLS)
    in_window = (col[None, :] >= col_start[:, None]) & (col[None, :] < col_start[:, None] + NA_COLS)
    dcol = np.clip(col[None, :] - col[:, None], 1 - NA_COLS, NA_COLS - 1) + NA_COLS - 1
    onehot = (dcol[None] == np.arange(2 * NA_COLS - 1)[:, None, None]) & in_window[None]
    rest = rel_bias.astype(F32) * LOG2_E
    cb = None
    for _ in range(3):
        piece = rest.astype(MXU_DTYPE)
        rest = rest - piece.astype(F32)
        part = jnp.einsum("lhdc,cqk->lhqdk", piece, jnp.asarray(onehot, MXU_DTYPE), preferred_element_type=F32)
        cb = part if cb is None else cb + part
    cb = jnp.where(jnp.asarray(in_window)[:, None, :], cb, NEG_INF)
    table = cb.reshape(cb.shape[:3] + ((2 * NA_ROWS - 1) * GRID_W,))
    mid = (NA_ROWS // 2 - 1) * GRID_W
    side = jnp.full(cb.shape[:3] + (GRID_W,), NEG_INF, F32)
    padded = jnp.concatenate([side, table[..., mid:mid + NA_ROWS * GRID_W], side], axis=-1)
    return table, table[..., GRID_W:], padded


def _gqa_kernel(*refs, chunks):
    q_ref, kv_refs, (o_ref, s_ref, p_ref) = refs[0], refs[1:-3], refs[-3:]
    tq = q_ref.shape[1]
    n_kv = GQA_KV_HEADS
    rb = GQA_ROW_BLOCK
    n_blk = GQA_Q_HEADS // n_kv * tq // rb
    per_head = tq // rb
    lo = lax.broadcasted_iota(jnp.int32, (tq, LANES), 1) < HEAD_DIM
    zero = jnp.zeros((), q_ref.dtype)

    def stacked_queries(g):
        parts = []
        for j in range(2 * g, 2 * g + 2):
            q = q_ref[0, :, j * LANES:(j + 1) * LANES]
            parts += [jnp.where(lo, q, zero), jnp.where(lo, zero, q)]
        return jnp.concatenate(parts, axis=0)

    items = [(g, ci) for g in range(n_kv) for ci in range(len(chunks))]
    qs = [stacked_queries(g) for g in range(n_kv)]

    def rows_of(ci, lane_tile):
        pieces = [kv_refs[src][0, c0:c1, lane_tile * LANES:(lane_tile + 1) * LANES] for src, c0, c1 in chunks[ci]]
        return pieces[0] if len(pieces) == 1 else jnp.concatenate(pieces, axis=0)

    def scores(idx):
        g, ci = items[idx]
        s_ref[idx % 2, :, 0:widths[ci]] = _dot_nt(qs[g], rows_of(ci, g))

    widths = [sum(c1 - c0 for _, c0, c1 in pieces) for pieces in chunks]
    scores(0)
    m = acc = None
    for idx, (g, ci) in enumerate(items):
        w, slot = widths[ci], idx % 2
        if idx + 1 < len(items):
            scores(idx + 1)
        if ci == 0:
            m, acc = [None] * n_blk, [None] * n_blk
        alpha = [None] * n_blk
        for i in range(n_blk):
            rows = slice(i * rb, (i + 1) * rb)
            s = s_ref[slot, rows, 0:w]
            s_max = s.max(axis=-1, keepdims=True)
            m_new = s_max if ci == 0 else jnp.maximum(m[i], s_max)
            if ci:
                alpha[i] = jnp.exp2(m[i] - m_new)
            m[i] = m_new
            p_ref[slot, rows, 0:w] = jnp.exp2(s - m_new).astype(p_ref.dtype)
        pv = _dot(p_ref[slot, :, 0:w], rows_of(ci, n_kv + g))
        for i in range(n_blk):
            part = pv[i * rb:(i + 1) * rb]
            acc[i] = part if ci == 0 else alpha[i] * acc[i] + part
        if ci + 1 == len(chunks):
            for j in range(2):
                halves = []
                for half in range(2):
                    h = 2 * j + half
                    a = jnp.concatenate(acc[h * per_head:(h + 1) * per_head], axis=0)
                    a_sw = pltpu.roll(a, HEAD_DIM, 1)
                    halves.append(a / a_sw if half == 0 else a_sw / a)
                pair = 2 * g + j
                o_ref[0, :, pair * LANES:(pair + 1) * LANES] = jnp.where(lo, halves[0], halves[1]).astype(o_ref.dtype)


def _gqa(gq, kv_sources):
    b, n_q, _ = gq.shape
    tq = min(GQA_Q_TILE, n_q)
    starts = np.cumsum([0] + [kv.shape[1] for kv in kv_sources])
    chunks = []
    for c0 in range(0, int(starts[-1]), GQA_KEY_CHUNK):
        c1 = min(c0 + GQA_KEY_CHUNK, int(starts[-1]))
        pieces = []
        for src in range(len(kv_sources)):
            lo, hi = max(c0, int(starts[src])), min(c1, int(starts[src + 1]))
            if lo < hi:
                pieces.append((src, lo - int(starts[src]), hi - int(starts[src])))
        chunks.append(tuple(pieces))
    stacked_rows = GQA_Q_HEADS // GQA_KV_HEADS * tq
    width = min(GQA_KEY_CHUNK, int(starts[-1]))
    return pl.pallas_call(
        functools.partial(_gqa_kernel, chunks=tuple(chunks)),
        grid=(b, n_q // tq),
        in_specs=[pl.BlockSpec((1, tq, HEAD_W), lambda i, j: (i, j, 0))] + [_batch_block(kv) for kv in kv_sources],
        out_specs=pl.BlockSpec((1, tq, HEAD_W), lambda i, j: (i, j, 0)),
        out_shape=jax.ShapeDtypeStruct((b, n_q, HEAD_W), MXU_DTYPE),
        scratch_shapes=[pltpu.VMEM((2, stacked_rows, width), F32),
                        pltpu.VMEM((2, stacked_rows, width), MXU_DTYPE)],
        compiler_params=_params(2),
        name="gqa_attention",
    )(gq, *kv_sources)


def _post_kernel(yr_ref, yn_ref, yg_ref, gate_ref, s_ref, gt1_ref, sh2_ref, sc2_ref, gt2_ref,
                 gpost_ref, gpre_ref, gpost2_ref, wr_ref, wn_ref, wg_ref, wout_ref, w1_ref, w2_ref, *rest):
    n_cast = (len(rest) - 1) // 2
    o_ref = rest[n_cast]
    _cast_slabs(rest[:n_cast], rest[n_cast + 1:])
    tm, d = o_ref.shape[1:]

    def mixed(rows):
        y = None
        for i, (br, w_ref) in enumerate(((yr_ref, wr_ref), (yn_ref, wn_ref), (yg_ref, wg_ref))):
            z = _dot(br[0, rows, :], w_ref[...]) * gate_ref[0, rows, i * d:(i + 1) * d].astype(F32)
            y = z if y is None else y + z
        return _dot(y.astype(MXU_DTYPE), wout_ref[...])

    def mlp(rows, y):
        x = s_ref[0, rows, :] + gt1_ref[0] * (_rms(y) * gpost_ref[...])
        h = _rms(x) * gpre_ref[...]
        h = (h * (1.0 + sc2_ref[0]) + sh2_ref[0]).astype(MXU_DTYPE)
        acc = None
        for j in range(D_FF // d):
            u = jnp.maximum(_dot(h, w1_ref[:, j * d:(j + 1) * d]), 0.0)
            part = _dot((u * u).astype(MXU_DTYPE), w2_ref[j * d:(j + 1) * d, :])
            acc = part if acc is None else acc + part
        o_ref[0, rows, :] = x + gt2_ref[0] * (_rms(acc) * gpost2_ref[...])

    groups = [slice(r0, r0 + POST_ROW_GROUP) for r0 in range(0, tm, POST_ROW_GROUP)]
    ys = [mixed(rows) for rows in groups]
    for rows, y in zip(groups, ys):
        mlp(rows, y)


def _post(l, y_ret, y_na, y_gqa, gates, stream, is_ctx, mod3, mod_rows, gains, merge_weights, mlp_weights, cast=()):
    tg = _TokenGrid(stream, is_ctx, l, mod_rows)
    d = tg.d
    cast_specs = [tg.cast_specs(a, l + 1) for a in cast]
    outs = pl.pallas_call(
        _post_kernel,
        grid=tg.grid,
        in_specs=[tg.tok(HEAD_W), tg.tok(HEAD_W), tg.tok(HEAD_W), tg.tok(3 * d), tg.tok(d)]
        + [tg.mod(chunk) for chunk in (2, 3, 4, 5)]
        + [_layer(a, l) for a in gains] + [_layer(w, l) for w in merge_weights]
        + [_resident(w.shape) for w in mlp_weights] + [s[0] for s in cast_specs],
        out_specs=[tg.tok(d)] + [s[1] for s in cast_specs],
        out_shape=[jax.ShapeDtypeStruct(tg.shape + (d,), F32)] + [s[2] for s in cast_specs],
        compiler_params=_params(2),
        name="merge_mlp_ctx" if is_ctx else "merge_mlp",
    )(tg.view(y_ret), tg.view(y_na), tg.view(y_gqa), tg.view(gates), tg.view(stream),
      mod3, mod3, mod3, mod3, *gains, *merge_weights, *mlp_weights, *cast)
    return (tg.unview(outs[0]),) + tuple(outs[1:])


def _rope_tables(n_latent):
    t = np.arange(n_latent)
    pos = np.stack([t // GRID_W, t % GRID_W], axis=-1).astype(np.float64)
    n_freq = HEAD_DIM // 4
    inv_freq = ROPE_BASE ** (-np.arange(n_freq, dtype=np.float64) / n_freq)
    lane = np.arange(LANES) % HEAD_DIM
    axis, second, freq = lane // 32, (lane % 32) // 16, lane % 16
    ang = pos[:, axis] * inv_freq[freq][None, :]
    return jnp.asarray(np.cos(ang), F32), jnp.asarray(np.sin(ang) * np.where(second == 1, 1.0, -1.0), F32)


def kernel(x, c, ctx, c_ctx, w_mod, b_mod, g_pre_mix, g_post_mix, g_pre_mlp, g_post_mlp, w_in, ret_decay_logit, na_rel_bias, gqa_q_norm, gqa_k_norm, w_br_ret, w_br_na, w_br_gqa, w_out, w_mlp_in, w_mlp_out):
    b, n, d = x.shape
    depth = w_mod.shape[0]
    cdt = MXU_DTYPE

    rope_tables = _rope_tables(n)
    lane_head = np.arange(2 * LANES) // HEAD_DIM
    gsum = jnp.asarray(lane_head[:, None] == lane_head[None, :], cdt)

    n_rows = -(-(b + 1) // 8) * 8
    cs = jnp.concatenate([c, jnp.zeros((n_rows - b - 1, d), F32), c_ctx[None, :]], axis=0)
    mod3 = _modulation(cs, w_mod, b_mod).reshape(depth * n_rows, 1, 6 * d)

    w_in_l = w_in[0].astype(cdt)
    merge_weights = tuple(w.astype(cdt) for w in (w_br_ret, w_br_na, w_br_gqa, w_out))
    post_gains = tuple(g[:, None, :] for g in (g_post_mix, g_pre_mlp, g_post_mlp))
    pre_gain = g_pre_mix[:, None, :]
    q_gain = jnp.tile(gqa_q_norm, (1, 2))[:, None, :]
    k_gain = jnp.tile(gqa_k_norm, (1, 2))[:, None, :]
    logit_tile = jnp.broadcast_to(ret_decay_logit.astype(F32)[..., None, None], (depth, 2, RET_HEADS, 8, LANES))
    bias_tables = _na_bias_table(na_rel_bias)

    ctx_s, lat_s = ctx, x
    for l in range(depth):
        last = l == depth - 1
        pc = _in_projection(l, ctx_s, True, mod3, n_rows, pre_gain, w_in_l, rope_tables, q_gain, k_gain, gsum)
        pt = _in_projection(l, lat_s, False, mod3, n_rows, pre_gain, w_in_l, rope_tables, q_gain, k_gain, gsum,
                            cast=(w_mlp_in, w_mlp_out))
        mlp_weights = pt[9:]
        ret_c, ret_l = _retention(l, logit_tile, pc[0:3], pt[0:3], not last)
        na_c, na_l = _neighbourhood(l, pc[3:6], pt[3:6], bias_tables, not last)
        gqa_l = _gqa(pt[6], (pc[7], pt[7]))
        post_l = _post(l, ret_l, na_l, gqa_l, pt[8], lat_s, False, mod3, n_rows, post_gains, merge_weights,
                       mlp_weights, cast=() if last else (w_in,))
        lat_s = post_l[0]
        if not last:
            w_in_l = post_l[1]
            gqa_c = _gqa(pc[6], (pc[7],))
            ctx_s, = _post(l, ret_c, na_c, gqa_c, pc[8], ctx_s, True, mod3, n_rows, post_gains, merge_weights,
                           mlp_weights)
    return lat_s
```

```python
import functools

import jax
import jax.numpy as jnp
import numpy as np
from jax import lax
from jax.experimental import pallas as pl
from jax.experimental.pallas import tpu as pltpu

F32 = jnp.float32
MXU_DTYPE = jnp.bfloat16

D_MODEL = 1024
GRID_W = 64
CTX_LEN = 256
HEAD_DIM = 64
RET_HEADS = 4
RET_CHUNK = 128
NA_HEADS = 8
NA_ROWS = 8
NA_COLS = 16
NA_BLOCK_ROWS = 4
NA_UNION_ROWS = NA_BLOCK_ROWS + NA_ROWS
GQA_Q_HEADS = 8
GQA_KV_HEADS = 2
GQA_Q_TILE = 256
GQA_KEY_CHUNK = 768
GQA_ROW_BLOCK = 64
D_FF = 4 * D_MODEL
ROPE_BASE = 10000.0
NORM_EPS = 1e-6
NEG_INF = -1e30
LOG2_E = 1.4426950408889634

LANES = 128
HEAD_W = 512
C_RQ, C_RK, C_RV, C_RG = 0, 256, 512, 1024
C_NQ, C_NK, C_NV = 1536, 2048, 2560
C_GQ, C_GK, C_GV = 3072, 3584, 3712
C_GATE = 3840
IN_W = 6912

TOKEN_TILE = 512
POST_ROW_GROUP = 256
VMEM_LIMIT = 56 * 1024 * 1024


def _rms(x):
    return x * lax.rsqrt(jnp.mean(x * x, axis=-1, keepdims=True) + NORM_EPS)


def _dot(a, b):
    return jnp.dot(a, b, preferred_element_type=F32)


def _dot_nt(a, b):
    return lax.dot_general(a, b, (((1,), (1,)), ((), ())), preferred_element_type=F32)


def _resident(shape):
    return pl.BlockSpec(shape, lambda *_: (0,) * len(shape), pipeline_mode=pl.Buffered(1))


def _layer(arr, l):
    return pl.BlockSpec((None,) + arr.shape[1:], lambda *_: (l,) + (0,) * (arr.ndim - 1),
                        pipeline_mode=pl.Buffered(1))


def _batch_block(arr):
    return pl.BlockSpec((1,) + arr.shape[1:], lambda i, *_: (i, 0, 0))


def _params(n_axes):
    return pltpu.CompilerParams(dimension_semantics=("arbitrary",) * n_axes,
                                vmem_limit_bytes=VMEM_LIMIT)


def _mod_kernel(c_ref, w_ref, b_ref, o_ref):
    c = c_ref[...]
    s = (c * jax.nn.sigmoid(c)).astype(MXU_DTYPE)
    o_ref[0] = _dot(s, w_ref[0].astype(MXU_DTYPE)) + b_ref[0]


def _modulation(cs, w_mod, b_mod):
    depth, d, n = w_mod.shape
    r = cs.shape[0]
    tn = 1024
    return pl.pallas_call(
        _mod_kernel,
        grid=(depth, n // tn),
        in_specs=[pl.BlockSpec((r, d), lambda l, j: (0, 0)),
                  pl.BlockSpec((1, d, tn), lambda l, j: (l, 0, j)),
                  pl.BlockSpec((1, 1, tn), lambda l, j: (l, 0, j))],
        out_specs=pl.BlockSpec((1, r, tn), lambda l, j: (l, 0, j)),
        out_shape=jax.ShapeDtypeStruct((depth, r, n), F32),
        compiler_params=_params(2),
        name="adaln_modulation",
    )(cs, w_mod, b_mod.reshape(depth, 1, n))


class _TokenGrid:
    def __init__(self, stream, is_ctx, layer, mod_rows):
        b, n, self.d = stream.shape
        self.is_ctx = is_ctx
        self.shape = (1, b * n) if is_ctx else (b, n)
        self.tm = min(TOKEN_TILE, self.shape[1])
        self.grid = (self.shape[0], self.shape[1] // self.tm)
        self.orig = (b, n)
        self.mod_base = layer * mod_rows + (mod_rows - 1 if is_ctx else 0)

    def view(self, arr):
        return arr.reshape(self.shape + arr.shape[2:])

    def unview(self, arr):
        return arr.reshape(self.orig + arr.shape[2:])

    def tok(self, width):
        return pl.BlockSpec((1, self.tm, width), lambda i, j: (i, j, 0))

    def mod(self, chunk):
        base, per_batch = self.mod_base, 0 if self.is_ctx else 1
        return pl.BlockSpec((1, 1, self.d), lambda i, j: (base + per_batch * i, 0, chunk))

    def cast_specs(self, stacked, layer):
        steps = self.grid[0] * self.grid[1]
        rows, cols = stacked.shape[1:]
        slab = rows // steps
        assert slab * steps == rows and slab % 16 == 0, (rows, steps)
        per_row = self.grid[1]
        return (pl.BlockSpec((None, slab, cols), lambda i, j: (layer, i * per_row + j, 0)),
                pl.BlockSpec((slab, cols), lambda i, j: (i * per_row + j, 0)),
                jax.ShapeDtypeStruct((rows, cols), MXU_DTYPE))


def _cast_slabs(src_refs, dst_refs):
    for src, dst in zip(src_refs, dst_refs):
        dst[...] = src[...].astype(dst.dtype)


def _inproj_kernel(*refs, with_rope, n_cast, kv_only):
    x_ref, sh_ref, sc_ref, g_ref, w_ref = refs[:5]
    if with_rope:
        cos_ref, sin_ref = refs[5:7]
        refs = refs[7:]
    else:
        refs = refs[5:]
    qg_ref, kg_ref, gsum_ref = refs[:3]
    cast_src, refs = refs[3:3 + n_cast], refs[3 + n_cast:]
    if kv_only:
        rqk_ref, rv_ref, nk_ref, nv_ref, gkv_ref = refs[:5]
        n_out = 5
    else:
        rqk_ref, rv_ref, rg_ref, nq_ref, nk_ref, nv_ref, gq_ref, gkv_ref, gate_ref = refs[:9]
        n_out = 9
    _cast_slabs(cast_src, refs[n_out:])

    x = x_ref[0]
    h = _rms(x) * g_ref[...]
    h = h * (1.0 + sc_ref[0]) + sh_ref[0]
    hb = h.astype(MXU_DTYPE)
    lane = lax.broadcasted_iota(jnp.int32, (x.shape[0], LANES), 1)
    gsum = gsum_ref[...]

    def proj(col, width):
        return _dot(hb, w_ref[:, col:col + width])

    if with_rope:
        cos, sin = cos_ref[...], sin_ref[...]
        first_half = (lane % 32) < 16

        def rope(v):
            partner = jnp.where(first_half, pltpu.roll(v, LANES - 16, 1), pltpu.roll(v, 16, 1))
            return v * cos + partner * sin
    else:
        def rope(v):
            return v

    def head_rms(v):
        ms = _dot((v * v).astype(MXU_DTYPE), gsum) * (1.0 / HEAD_DIM)
        return v * lax.rsqrt(ms + NORM_EPS)

    odt = rqk_ref.dtype
    ret_scale = HEAD_DIM ** -0.5
    att_scale = HEAD_DIM ** -0.5 * LOG2_E

    rk = proj(C_RK, 256)
    if kv_only:
        rqk_ref[0, :, 0:256] = jnp.zeros((x.shape[0], 256), odt)
    else:
        rq = proj(C_RQ, 256)
    for j in range(2):
        sl = slice(j * LANES, (j + 1) * LANES)
        if not kv_only:
            rqk_ref[0, :, j * LANES:(j + 1) * LANES] = rope(rq[:, sl]).astype(odt)
        rqk_ref[0, :, 256 + j * LANES:256 + (j + 1) * LANES] = (rope(rk[:, sl]) * ret_scale).astype(odt)
    rv_ref[0] = proj(C_RV, HEAD_W).astype(odt)
    if not kv_only:
        rg = proj(C_RG, HEAD_W)
        rg_ref[0] = (rg * jax.nn.sigmoid(rg)).astype(odt)
        nq_ref[0] = (proj(C_NQ, HEAD_W) * att_scale).astype(odt)
    nk_ref[0] = proj(C_NK, HEAD_W).astype(odt)
    nv = proj(C_NV, HEAD_W).astype(odt)
    for j in range(NA_HEADS // 2):
        nv_ref[0, :, 2 * j * LANES:(2 * j + 1) * LANES] = nv[:, j * LANES:(j + 1) * LANES]
        nv_ref[0, :, (2 * j + 1) * LANES:(2 * j + 2) * LANES] = jnp.ones((nv.shape[0], LANES), odt)

    if not kv_only:
        gq = proj(C_GQ, HEAD_W)
        qg = qg_ref[...]
        for j in range(2):
            normed = head_rms(gq[:, 2 * j * LANES:(2 * j + 2) * LANES])
            for i in range(2):
                sl = slice((2 * j + i) * LANES, (2 * j + i + 1) * LANES)
                gq_ref[0, :, sl] = (rope(normed[:, i * LANES:(i + 1) * LANES] * qg) * att_scale).astype(odt)
    gkv = proj(C_GK, 2 * LANES)
    gk = rope(head_rms(gkv)[:, 0:LANES] * kg_ref[...])
    gv = gkv[:, LANES:]
    low_half = lane < HEAD_DIM
    gk_sw, gv_sw = pltpu.roll(gk, HEAD_DIM, 1), pltpu.roll(gv, HEAD_DIM, 1)
    gkv_ref[0, :, 0:LANES] = jnp.where(low_half, gk, gk_sw).astype(odt)
    gkv_ref[0, :, LANES:2 * LANES] = jnp.where(low_half, gk_sw, gk).astype(odt)
    gkv_ref[0, :, 2 * LANES:3 * LANES] = jnp.where(low_half, gv, 1.0).astype(odt)
    gkv_ref[0, :, 3 * LANES:4 * LANES] = jnp.where(low_half, gv_sw, 1.0).astype(odt)

    if not kv_only:
        for j in range(6):
            sl = slice(j * HEAD_W, (j + 1) * HEAD_W)
            gate_ref[0, :, sl] = jax.nn.sigmoid(proj(C_GATE + j * HEAD_W, HEAD_W)).astype(odt)


def _in_projection(l, stream, is_ctx, mod3, mod_rows, layer_gain, w_in, rope_tables, q_gain, k_gain, gsum, cast=(),
                   kv_only=False):
    tg = _TokenGrid(stream, is_ctx, l, mod_rows)
    d = tg.d
    all_widths = [HEAD_W] * 5 + [2 * HEAD_W] + [HEAD_W] * 2 + [3 * d]
    made = [0, 1, 4, 5, 7] if kv_only else list(range(len(all_widths)))
    out_widths = [all_widths[k] for k in made]
    rope_specs = [] if is_ctx else [pl.BlockSpec((tg.tm, LANES), lambda i, j: (j, 0))] * 2
    rope_args = () if is_ctx else rope_tables
    cast_specs = [tg.cast_specs(a, l) for a in cast]
    outs = pl.pallas_call(
        functools.partial(_inproj_kernel, with_rope=not is_ctx, n_cast=len(cast), kv_only=kv_only),
        grid=tg.grid,
        in_specs=[tg.tok(d), tg.mod(0), tg.mod(1), _layer(layer_gain, l), _resident(w_in.shape)] + rope_specs
        + [_layer(q_gain, l), _layer(k_gain, l), _resident(gsum.shape)] + [s[0] for s in cast_specs],
        out_specs=[tg.tok(w) for w in out_widths] + [s[1] for s in cast_specs],
        out_shape=[jax.ShapeDtypeStruct(tg.shape + (w,), MXU_DTYPE) for w in out_widths] + [s[2] for s in cast_specs],
        compiler_params=_params(2),
        name="in_projection_ctx" if is_ctx else "in_projection",
    )(tg.view(stream), mod3, mod3, layer_gain, w_in, *rope_args, q_gain, k_gain, gsum, *cast)
    projections = [None] * len(all_widths)
    for k, o in zip(made, outs):
        projections[k] = tg.unview(o)
    return tuple(projections) + tuple(outs[len(out_widths):])


def _retention_kernel(*refs, with_ctx_out):
    if with_ctx_out:
        logit_ref, qk_c, v_c, g_c, qk_l, v_l, g_l, o_c, o_l, dec_ref, st_ref = refs
    else:
        logit_ref, qk_c, v_c, qk_l, v_l, g_l, o_l, dec_ref, st_ref = refs
        g_c = o_c = None
    c_len = RET_CHUNK
    n_ctx_chunks = qk_c.shape[1] // c_len
    n_lat_chunks = qk_l.shape[1] // c_len
    n_chunks = n_ctx_chunks + n_lat_chunks
    heads = RET_HEADS
    pairs = heads // 2
    t_mask, t_dkf, t_dkb, t_dsf, t_dsb = (k * heads for k in range(5))
    t_dqf, t_dqb = 5 * heads, 5 * heads + pairs
    cdt = qk_l.dtype

    row = lax.broadcasted_iota(jnp.int32, (c_len, LANES), 0)
    lane = lax.broadcasted_iota(jnp.int32, (c_len, LANES), 1)
    lane_lo = lane < HEAD_DIM
    row_lo = row < HEAD_DIM

    @pl.when(pl.program_id(0) == 0)
    def _build_tables():
        rowf = row.astype(F32)
        lanef = lane.astype(F32)
        rel = rowf - lanef
        lg = [[jnp.broadcast_to(jax.nn.log_sigmoid(logit_ref[dd, hh])[0:1, :], (c_len, LANES))
               for hh in range(heads)] for dd in range(2)]
        for hh in range(heads):
            lf, lb = lg[0][hh], lg[1][hh]
            fwd = jnp.where(rel >= 0, jnp.exp(lf * jnp.maximum(rel, 0.0)), 0.0)
            bwd = jnp.where(rel < 0, jnp.exp(lb * jnp.maximum(-rel, 0.0)), 0.0)
            dec_ref[t_mask + hh] = fwd + bwd
            dec_ref[t_dkf + hh] = jnp.exp(lf * (c_len - 1.0 - lanef))
            dec_ref[t_dkb + hh] = jnp.exp(lb * lanef)
            dec_ref[t_dsf + hh] = jnp.exp(lf * float(c_len))
            dec_ref[t_dsb + hh] = jnp.exp(lb * float(c_len))
        for p in range(pairs):
            lf = jnp.where(lane_lo, lg[0][2 * p], lg[0][2 * p + 1])
            lb = jnp.where(lane_lo, lg[1][2 * p], lg[1][2 * p + 1])
            dec_ref[t_dqf + p] = jnp.exp(lf * (rowf + 1.0))
            dec_ref[t_dqb + p] = jnp.exp(lb * (float(c_len) - rowf))

    def summaries(qk_ref, v_ref, rows, c):
        for p in range(pairs):
            kt = qk_ref[0, rows, 256 + p * LANES:256 + (p + 1) * LANES].astype(F32).T
            for half in range(2):
                hh = 2 * p + half
                ktm = jnp.where(row_lo if half == 0 else jnp.logical_not(row_lo), kt, 0.0)
                lhs = jnp.concatenate([ktm * dec_ref[t_dkf + hh], ktm * dec_ref[t_dkb + hh]], axis=0)
                st_ref[c, hh] = _dot(lhs.astype(cdt), v_ref[0, rows, hh * LANES:(hh + 1) * LANES])

    def outputs(qk_ref, v_ref, g_ref, o_ref, rows, c):
        zero = jnp.zeros((), cdt)
        for p in range(pairs):
            q = qk_ref[0, rows, p * LANES:(p + 1) * LANES]
            k = qk_ref[0, rows, 256 + p * LANES:256 + (p + 1) * LANES]
            qs = jnp.concatenate([jnp.where(lane_lo, q, zero), jnp.where(lane_lo, zero, q)], axis=0)
            a = _dot_nt(qs, k)
            q32 = q.astype(F32)
            qf = (q32 * dec_ref[t_dqf + p]).astype(cdt)
            qb = (q32 * dec_ref[t_dqb + p]).astype(cdt)
            for half in range(2):
                hh = 2 * p + half
                sl = slice(hh * LANES, (hh + 1) * LANES)
                pm = (a[half * c_len:(half + 1) * c_len] * dec_ref[t_mask + hh]).astype(cdt)
                lhs = jnp.concatenate([pm, qf, qb], axis=1)
                rhs = jnp.concatenate([v_ref[0, rows, sl], st_ref[c, hh].astype(cdt)], axis=0)
                y = _rms(_dot(lhs, rhs)) * g_ref[0, rows, sl].astype(F32)
                o_ref[0, rows, sl] = y.astype(o_ref.dtype)

    def lat_rows(i):
        return pl.ds(pl.multiple_of(i * c_len, c_len), c_len)

    for ci in range(n_ctx_chunks):
        summaries(qk_c, v_c, slice(ci * c_len, (ci + 1) * c_len), ci)

    def lat_summaries(i, carry):
        summaries(qk_l, v_l, lat_rows(i), n_ctx_chunks + i)
        return carry

    lax.fori_loop(0, n_lat_chunks, lat_summaries, 0, unroll=True)

    fwd_order = list(range(n_chunks))
    bwd_order = list(range(n_ctx_chunks - 1, -1, -1)) + list(range(n_chunks - 1, n_ctx_chunks - 1, -1))
    for hh in range(heads):
        for order, lo, t_ds in ((fwd_order, 0, t_dsf), (bwd_order, c_len, t_dsb)):
            decay = dec_ref[t_ds + hh]
            state = jnp.zeros((c_len, LANES), F32)
            for c in order:
                z = st_ref[c, hh, lo:lo + c_len, :]
                st_ref[c, hh, lo:lo + c_len, :] = state
                state = state * decay + z

    if with_ctx_out:
        for ci in range(n_ctx_chunks):
            outputs(qk_c, v_c, g_c, o_c, slice(ci * c_len, (ci + 1) * c_len), ci)

    def lat_outputs(i, carry):
        outputs(qk_l, v_l, g_l, o_l, lat_rows(i), n_ctx_chunks + i)
        return carry

    lax.fori_loop(0, n_lat_chunks, lat_outputs, 0, unroll=True)


def _retention(l, logit_tile, ctx_qvg, lat_qvg, with_ctx_out):
    b, n_ctx, _ = ctx_qvg[0].shape
    n_lat = lat_qvg[0].shape[1]
    n_tables = 5 * RET_HEADS + RET_HEADS
    out_shape = [jax.ShapeDtypeStruct((b, n_lat, HEAD_W), MXU_DTYPE)]
    if with_ctx_out:
        out_shape.insert(0, jax.ShapeDtypeStruct((b, n_ctx, HEAD_W), MXU_DTYPE))
    outs = pl.pallas_call(
        functools.partial(_retention_kernel, with_ctx_out=with_ctx_out),
        grid=(b,),
        in_specs=[_layer(logit_tile, l)] + [_batch_block(a) for a in (*ctx_qvg, *lat_qvg)],
        out_specs=[_batch_block(s) for s in out_shape],
        out_shape=out_shape,
        scratch_shapes=[pltpu.VMEM((n_tables, RET_CHUNK, LANES), F32),
                        pltpu.VMEM(((n_ctx + n_lat) // RET_CHUNK, RET_HEADS, 2 * RET_CHUNK, LANES), F32)],
        compiler_params=_params(1),
        name="retention",
    )(logit_tile, *ctx_qvg, *lat_qvg)
    return (outs[0], outs[1]) if with_ctx_out else (None, outs[0])


def _na_kernel(*refs, with_ctx_out):
    if with_ctx_out:
        q_c, k_c, v_c, q_l, k_l, v_l, bias_ref, bias_shift_ref, bias_pad_ref, o_c, o_l, s_ref, p_ref = refs
    else:
        k_c, v_c, q_l, k_l, v_l, bias_ref, bias_shift_ref, bias_pad_ref, o_l, s_ref, p_ref = refs
    n_ctx = k_c.shape[1]
    n_rows = q_l.shape[1] // GRID_W
    n_blocks = n_rows // NA_BLOCK_ROWS
    qb = NA_BLOCK_ROWS * GRID_W
    un = NA_UNION_ROWS * GRID_W
    win = NA_ROWS * GRID_W
    half_win = NA_ROWS // 2
    cdt = p_ref.dtype

    def stacked(q):
        lo = lax.broadcasted_iota(jnp.int32, q.shape, 1) < HEAD_DIM
        zero = jnp.zeros((), q.dtype)
        return jnp.concatenate([jnp.where(lo, q, zero), jnp.where(lo, zero, q)], axis=0)

    def unstack(o):
        m = o.shape[0] // 2
        lo = lax.broadcasted_iota(jnp.int32, (m, LANES), 1) < HEAD_DIM
        o = o[:, 0:LANES] / o[:, LANES:]
        return jnp.where(lo, o[0:m], o[m:])

    def softmax(parts):
        m = parts[0].max(axis=-1, keepdims=True)
        for s in parts[1:]:
            m = jnp.maximum(m, s.max(axis=-1, keepdims=True))
        return [jnp.exp2(s - m) for s in parts]

    def values(v_ref, rows, p):
        return v_ref[0, rows, 2 * p * LANES:(2 * p + 2) * LANES]

    def block(g, config):
        if config == "first":
            u0 = 0
        elif config == "last":
            u0 = n_rows - NA_UNION_ROWS
        else:
            u0 = g * NA_BLOCK_ROWS - half_win
        q_rows = pl.ds(pl.multiple_of(g * qb, qb), qb)
        k_rows = pl.ds(pl.multiple_of(u0 * GRID_W, GRID_W), un)

        def scores(p):
            sl = slice(p * LANES, (p + 1) * LANES)
            qs = stacked(q_l[0, q_rows, sl])
            s_ref[p % 2, :, 0:un] = _dot_nt(qs, k_l[0, k_rows, sl])
            s_ref[p % 2, :, un:] = _dot_nt(qs, k_c[0, :, sl])

        scores(0)
        for p in range(NA_HEADS // 2):
            sl = slice(p * LANES, (p + 1) * LANES)
            pbuf, sbuf = p_ref.at[p % 2], s_ref.at[p % 2]
            if p + 1 < NA_HEADS // 2:
                scores(p + 1)
            for half in range(2):
                hh = 2 * p + half
                for a in range(NA_BLOCK_ROWS):
                    rows = slice(half * qb + a * GRID_W, half * qb + (a + 1) * GRID_W)
                    if config == "first":
                        w, off = 0, NA_ROWS - 1 - a
                    elif config == "last":
                        w, off = (n_rows - NA_ROWS - u0) * GRID_W, half_win - 1 - a
                    else:
                        w, off = a * GRID_W, half_win - 1
                    if w % LANES:
                        lo, width, bias = w - GRID_W, win + 2 * GRID_W, bias_pad_ref[hh]
                    elif off % 2:
                        lo, width, bias = w, win, bias_shift_ref[hh, :, (off - 1) * GRID_W:(off - 1) * GRID_W + win]
                    else:
                        lo, width, bias = w, win, bias_ref[hh, :, off * GRID_W:off * GRID_W + win]
                    p_loc, p_cx = softmax([sbuf[rows, lo:lo + width] + bias, sbuf[rows, un:]])
                    pieces = [p_loc, jnp.zeros((GRID_W, un - lo - width), F32), p_cx]
                    if lo:
                        pieces = [jnp.zeros((GRID_W, lo), F32)] + pieces
                    pbuf[rows, :] = jnp.concatenate([x for x in pieces if x.shape[1]], axis=1).astype(cdt)
            o = _dot(pbuf[:, 0:un], values(v_l, k_rows, p)) + _dot(pbuf[:, un:], values(v_c, slice(None), p))
            o_l[0, q_rows, sl] = unstack(o).astype(o_l.dtype)

    block(0, "first")

    def middle(g, carry):
        block(g, "middle")
        return carry

    lax.fori_loop(1, n_blocks - 1, middle, 0, unroll=True)
    block(n_blocks - 1, "last")

    if with_ctx_out:
        for p in range(NA_HEADS // 2):
            sl = slice(p * LANES, (p + 1) * LANES)
            (pn,) = softmax([_dot_nt(stacked(q_c[0, :, sl]), k_c[0, :, sl])])
            o = _dot(pn.astype(cdt), values(v_c, slice(None), p))
            o_c[0, :, sl] = unstack(o).astype(o_c.dtype)


def _neighbourhood(l, ctx_qkv, lat_qkv, bias_tables, with_ctx_out):
    b, n_ctx, _ = ctx_qkv[0].shape
    n_lat = lat_qkv[0].shape[1]
    p_cols = NA_UNION_ROWS * GRID_W + n_ctx
    stacked_rows = 2 * NA_BLOCK_ROWS * GRID_W
    out_shape = [jax.ShapeDtypeStruct((b, n_lat, HEAD_W), MXU_DTYPE)]
    if with_ctx_out:
        out_shape.insert(0, jax.ShapeDtypeStruct((b, n_ctx, HEAD_W), MXU_DTYPE))
    outs = pl.pallas_call(
        functools.partial(_na_kernel, with_ctx_out=with_ctx_out),
        grid=(b,),
        in_specs=[_batch_block(a) for a in (*ctx_qkv, *lat_qkv)] + [_layer(tbl, l) for tbl in bias_tables],
        out_specs=[_batch_block(s) for s in out_shape],
        out_shape=out_shape,
        scratch_shapes=[pltpu.VMEM((2, stacked_rows, p_cols), F32),
                        pltpu.VMEM((2, stacked_rows, p_cols), MXU_DTYPE)],
        compiler_params=_params(1),
        name="neighbourhood_attention",
    )(*ctx_qkv, *lat_qkv, *bias_tables)
    return (outs[0], outs[1]) if with_ctx_out else (None, outs[0])


def _na_bias_table(rel_bias):
    col = np.arange(GRID_W)
    col_start = np.clip(col - NA_COLS // 2, 0, GRID_W - NA_COLS)
    in_window = (col[None, :] >= col_start[:, None]) & (col[None, :] < col_start[:, None] + NA_COLS)
    dcol = np.clip(col[None, :] - col[:, None], 1 - NA_COLS, NA_COLS - 1) + NA_COLS - 1
    onehot = (dcol[None] == np.arange(2 * NA_COLS - 1)[:, None, None]) & in_window[None]
    rest = rel_bias.astype(F32) * LOG2_E
    cb = None
    for _ in range(3):
        piece = rest.astype(MXU_DTYPE)
        rest = rest - piece.astype(F32)
        part = jnp.einsum("lhdc,cqk->lhqdk", piece, jnp.asarray(onehot, MXU_DTYPE), preferred_element_type=F32)
        cb = part if cb is None else cb + part
    cb = jnp.where(jnp.asarray(in_window)[:, None, :], cb, NEG_INF)
    table = cb.reshape(cb.shape[:3] + ((2 * NA_ROWS - 1) * GRID_W,))
    mid = (NA_ROWS // 2 - 1) * GRID_W
    side = jnp.full(cb.shape[:3] + (GRID_W,), NEG_INF, F32)
    padded = jnp.concatenate([side, table[..., mid:mid + NA_ROWS * GRID_W], side], axis=-1)
    return table, table[..., GRID_W:], padded


def _gqa_kernel(*refs, chunks):
    q_ref, kv_refs, (o_ref, s_ref, p_ref) = refs[0], refs[1:-3], refs[-3:]
    tq = q_ref.shape[1]
    n_kv = GQA_KV_HEADS
    rb = GQA_ROW_BLOCK
    n_blk = GQA_Q_HEADS // n_kv * tq // rb
    per_head = tq // rb
    lo = lax.broadcasted_iota(jnp.int32, (tq, LANES), 1) < HEAD_DIM
    zero = jnp.zeros((), q_ref.dtype)

    def stacked_queries(g):
        parts = []
        for j in range(2 * g, 2 * g + 2):
            q = q_ref[0, :, j * LANES:(j + 1) * LANES]
            parts += [jnp.where(lo, q, zero), jnp.where(lo, zero, q)]
        return jnp.concatenate(parts, axis=0)

    items = [(g, ci) for g in range(n_kv) for ci in range(len(chunks))]
    qs = [stacked_queries(g) for g in range(n_kv)]

    def rows_of(ci, lane_tile):
        pieces = [kv_refs[src][0, c0:c1, lane_tile * LANES:(lane_tile + 1) * LANES] for src, c0, c1 in chunks[ci]]
        return pieces[0] if len(pieces) == 1 else jnp.concatenate(pieces, axis=0)

    def scores(idx):
        g, ci = items[idx]
        s_ref[idx % 2, :, 0:widths[ci]] = _dot_nt(qs[g], rows_of(ci, g))

    widths = [sum(c1 - c0 for _, c0, c1 in pieces) for pieces in chunks]
    scores(0)
    m = acc = None
    for idx, (g, ci) in enumerate(items):
        w, slot = widths[ci], idx % 2
        if idx + 1 < len(items):
            scores(idx + 1)
        if ci == 0:
            m, acc = [None] * n_blk, [None] * n_blk
        alpha = [None] * n_blk
        for i in range(n_blk):
            rows = slice(i * rb, (i + 1) * rb)
            s = s_ref[slot, rows, 0:w]
            s_max = s.max(axis=-1, keepdims=True)
            m_new = s_max if ci == 0 else jnp.maximum(m[i], s_max)
            if ci:
                alpha[i] = jnp.exp2(m[i] - m_new)
            m[i] = m_new
            p_ref[slot, rows, 0:w] = jnp.exp2(s - m_new).astype(p_ref.dtype)
        pv = _dot(p_ref[slot, :, 0:w], rows_of(ci, n_kv + g))
        for i in range(n_blk):
            part = pv[i * rb:(i + 1) * rb]
            acc[i] = part if ci == 0 else alpha[i] * acc[i] + part
        if ci + 1 == len(chunks):
            for j in range(2):
                halves = []
                for half in range(2):
                    h = 2 * j + half
                    a = jnp.concatenate(acc[h * per_head:(h + 1) * per_head], axis=0)
                    a_sw = pltpu.roll(a, HEAD_DIM, 1)
                    halves.append(a / a_sw if half == 0 else a_sw / a)
                pair = 2 * g + j
                o_ref[0, :, pair * LANES:(pair + 1) * LANES] = jnp.where(lo, halves[0], halves[1]).astype(o_ref.dtype)


def _gqa(gq, kv_sources):
    b, n_q, _ = gq.shape
    tq = min(GQA_Q_TILE, n_q)
    starts = np.cumsum([0] + [kv.shape[1] for kv in kv_sources])
    chunks = []
    for c0 in range(0, int(starts[-1]), GQA_KEY_CHUNK):
        c1 = min(c0 + GQA_KEY_CHUNK, int(starts[-1]))
        pieces = []
        for src in range(len(kv_sources)):
            lo, hi = max(c0, int(starts[src])), min(c1, int(starts[src + 1]))
            if lo < hi:
                pieces.append((src, lo - int(starts[src]), hi - int(starts[src])))
        chunks.append(tuple(pieces))
    stacked_rows = GQA_Q_HEADS // GQA_KV_HEADS * tq
    width = min(GQA_KEY_CHUNK, int(starts[-1]))
    return pl.pallas_call(
        functools.partial(_gqa_kernel, chunks=tuple(chunks)),
        grid=(b, n_q // tq),
        in_specs=[pl.BlockSpec((1, tq, HEAD_W), lambda i, j: (i, j, 0))] + [_batch_block(kv) for kv in kv_sources],
        out_specs=pl.BlockSpec((1, tq, HEAD_W), lambda i, j: (i, j, 0)),
        out_shape=jax.ShapeDtypeStruct((b, n_q, HEAD_W), MXU_DTYPE),
        scratch_shapes=[pltpu.VMEM((2, stacked_rows, width), F32),
                        pltpu.VMEM((2, stacked_rows, width), MXU_DTYPE)],
        compiler_params=_params(2),
        name="gqa_attention",
    )(gq, *kv_sources)


def _post_kernel(yr_ref, yn_ref, yg_ref, gate_ref, s_ref, gt1_ref, sh2_ref, sc2_ref, gt2_ref,
                 gpost_ref, gpre_ref, gpost2_ref, wr_ref, wn_ref, wg_ref, wout_ref, w1_ref, w2_ref, *rest):
    n_cast = (len(rest) - 1) // 2
    o_ref = rest[n_cast]
    _cast_slabs(rest[:n_cast], rest[n_cast + 1:])
    tm, d = o_ref.shape[1:]

    def mixed(rows):
        y = None
        for i, (br, w_ref) in enumerate(((yr_ref, wr_ref), (yn_ref, wn_ref), (yg_ref, wg_ref))):
            z = _dot(br[0, rows, :], w_ref[...]) * gate_ref[0, rows, i * d:(i + 1) * d].astype(F32)
            y = z if y is None else y + z
        return _dot(y.astype(MXU_DTYPE), wout_ref[...])

    def mlp(rows, y):
        x = s_ref[0, rows, :] + gt1_ref[0] * (_rms(y) * gpost_ref[...])
        h = _rms(x) * gpre_ref[...]
        h = (h * (1.0 + sc2_ref[0]) + sh2_ref[0]).astype(MXU_DTYPE)
        acc = None
        for j in range(D_FF // d):
            u = jnp.maximum(_dot(h, w1_ref[:, j * d:(j + 1) * d]), 0.0)
            part = _dot((u * u).astype(MXU_DTYPE), w2_ref[j * d:(j + 1) * d, :])
            acc = part if acc is None else acc + part
        o_ref[0, rows, :] = x + gt2_ref[0] * (_rms(acc) * gpost2_ref[...])

    groups = [slice(r0, r0 + POST_ROW_GROUP) for r0 in range(0, tm, POST_ROW_GROUP)]
    ys = [mixed(rows) for rows in groups]
    for rows, y in zip(groups, ys):
        mlp(rows, y)


def _post(l, y_ret, y_na, y_gqa, gates, stream, is_ctx, mod3, mod_rows, gains, merge_weights, mlp_weights, cast=()):
    tg = _TokenGrid(stream, is_ctx, l, mod_rows)
    d = tg.d
    cast_specs = [tg.cast_specs(a, l + 1) for a in cast]
    outs = pl.pallas_call(
        _post_kernel,
        grid=tg.grid,
        in_specs=[tg.tok(HEAD_W), tg.tok(HEAD_W), tg.tok(HEAD_W), tg.tok(3 * d), tg.tok(d)]
        + [tg.mod(chunk) for chunk in (2, 3, 4, 5)]
        + [_layer(a, l) for a in gains] + [_layer(w, l) for w in merge_weights]
        + [_resident(w.shape) for w in mlp_weights] + [s[0] for s in cast_specs],
        out_specs=[tg.tok(d)] + [s[1] for s in cast_specs],
        out_shape=[jax.ShapeDtypeStruct(tg.shape + (d,), F32)] + [s[2] for s in cast_specs],
        compiler_params=_params(2),
        name="merge_mlp_ctx" if is_ctx else "merge_mlp",
    )(tg.view(y_ret), tg.view(y_na), tg.view(y_gqa), tg.view(gates), tg.view(stream),
      mod3, mod3, mod3, mod3, *gains, *merge_weights, *mlp_weights, *cast)
    return (tg.unview(outs[0]),) + tuple(outs[1:])


def _rope_tables(n_latent):
    t = np.arange(n_latent)
    pos = np.stack([t // GRID_W, t % GRID_W], axis=-1).astype(np.float64)
    n_freq = HEAD_DIM // 4
    inv_freq = ROPE_BASE ** (-np.arange(n_freq, dtype=np.float64) / n_freq)
    lane = np.arange(LANES) % HEAD_DIM
    axis, second, freq = lane // 32, (lane % 32) // 16, lane % 16
    ang = pos[:, axis] * inv_freq[freq][None, :]
    return jnp.asarray(np.cos(ang), F32), jnp.asarray(np.sin(ang) * np.where(second == 1, 1.0, -1.0), F32)


def kernel(x, c, ctx, c_ctx, w_mod, b_mod, g_pre_mix, g_post_mix, g_pre_mlp, g_post_mlp, w_in, ret_decay_logit, na_rel_bias, gqa_q_norm, gqa_k_norm, w_br_ret, w_br_na, w_br_gqa, w_out, w_mlp_in, w_mlp_out):
    b, n, d = x.shape
    depth = w_mod.shape[0]
    cdt = MXU_DTYPE

    rope_tables = _rope_tables(n)
    lane_head = np.arange(2 * LANES) // HEAD_DIM
    gsum = jnp.asarray(lane_head[:, None] == lane_head[None, :], cdt)

    n_rows = -(-(b + 1) // 8) * 8
    cs = jnp.concatenate([c, jnp.zeros((n_rows - b - 1, d), F32), c_ctx[None, :]], axis=0)
    mod3 = _modulation(cs, w_mod, b_mod).reshape(depth * n_rows, 1, 6 * d)

    w_in_l = w_in[0].astype(cdt)
    merge_weights = tuple(w.astype(cdt) for w in (w_br_ret, w_br_na, w_br_gqa, w_out))
    post_gains = tuple(g[:, None, :] for g in (g_post_mix, g_pre_mlp, g_post_mlp))
    pre_gain = g_pre_mix[:, None, :]
    q_gain = jnp.tile(gqa_q_norm, (1, 2))[:, None, :]
    k_gain = jnp.tile(gqa_k_norm, (1, 2))[:, None, :]
    logit_tile = jnp.broadcast_to(ret_decay_logit.astype(F32)[..., None, None], (depth, 2, RET_HEADS, 8, LANES))
    bias_tables = _na_bias_table(na_rel_bias)

    ctx_s, lat_s = ctx, x
    for l in range(depth):
        last = l == depth - 1
        pc = _in_projection(l, ctx_s, True, mod3, n_rows, pre_gain, w_in_l, rope_tables, q_gain, k_gain, gsum,
                            kv_only=last)
        pt = _in_projection(l, lat_s, False, mod3, n_rows, pre_gain, w_in_l, rope_tables, q_gain, k_gain, gsum,
                            cast=(w_mlp_in, w_mlp_out))
        mlp_weights = pt[9:]
        ret_c, ret_l = _retention(l, logit_tile, pc[0:2] if last else pc[0:3], pt[0:3], not last)
        na_c, na_l = _neighbourhood(l, pc[4:6] if last else pc[3:6], pt[3:6], bias_tables, not last)
        gqa_l = _gqa(pt[6], (pc[7], pt[7]))
        post_l = _post(l, ret_l, na_l, gqa_l, pt[8], lat_s, False, mod3, n_rows, post_gains, merge_weights,
                       mlp_weights, cast=() if last else (w_in,))
        lat_s = post_l[0]
        if not last:
            w_in_l = post_l[1]
            gqa_c = _gqa(pc[6], (pc[7],))
            ctx_s, = _post(l, ret_c, na_c, gqa_c, pc[8], ctx_s, True, mod3, n_rows, post_gains, merge_weights,
                           mlp_weights)
    return lat_s
```

```python
import functools

import jax
import jax.numpy as jnp
import numpy as np
from jax import lax
from jax.experimental import pallas as pl
from jax.experimental.pallas import tpu as pltpu

F32 = jnp.float32
MXU_DTYPE = jnp.bfloat16

D_MODEL = 1024
GRID_W = 64
CTX_LEN = 256
HEAD_DIM = 64
RET_HEADS = 4
RET_CHUNK = 128
NA_HEADS = 8
NA_ROWS = 8
NA_COLS = 16
NA_BLOCK_ROWS = 4
NA_UNION_ROWS = NA_BLOCK_ROWS + NA_ROWS
GQA_Q_HEADS = 8
GQA_KV_HEADS = 2
GQA_Q_TILE = 256
GQA_KEY_CHUNK = 768
GQA_ROW_BLOCK = 64
D_FF = 4 * D_MODEL
ROPE_BASE = 10000.0
NORM_EPS = 1e-6
NEG_INF = -1e30
LOG2_E = 1.4426950408889634

LANES = 128
HEAD_W = 512
C_RQ, C_RK, C_RV, C_RG = 0, 256, 512, 1024
C_NQ, C_NK, C_NV = 1536, 2048, 2560
C_GQ, C_GK, C_GV = 3072, 3584, 3712
C_GATE = 3840
IN_W = 6912

TOKEN_TILE = 512
POST_ROW_GROUP = 256
VMEM_LIMIT = 56 * 1024 * 1024


def _rms(x):
    return x * lax.rsqrt(jnp.mean(x * x, axis=-1, keepdims=True) + NORM_EPS)


def _dot(a, b):
    return jnp.dot(a, b, preferred_element_type=F32)


def _dot_nt(a, b):
    return lax.dot_general(a, b, (((1,), (1,)), ((), ())), preferred_element_type=F32)


def _resident(shape):
    return pl.BlockSpec(shape, lambda *_: (0,) * len(shape), pipeline_mode=pl.Buffered(1))


def _layer(arr, l):
    return pl.BlockSpec((None,) + arr.shape[1:], lambda *_: (l,) + (0,) * (arr.ndim - 1),
                        pipeline_mode=pl.Buffered(1))


def _batch_block(arr):
    return pl.BlockSpec((1,) + arr.shape[1:], lambda i, *_: (i, 0, 0))


def _params(n_axes):
    return pltpu.CompilerParams(dimension_semantics=("arbitrary",) * n_axes,
                                vmem_limit_bytes=VMEM_LIMIT)


def _mod_kernel(c_ref, w_ref, b_ref, o_ref):
    c = c_ref[...]
    s = (c * jax.nn.sigmoid(c)).astype(MXU_DTYPE)
    o_ref[0] = _dot(s, w_ref[0].astype(MXU_DTYPE)) + b_ref[0]


def _modulation(cs, w_mod, b_mod):
    depth, d, n = w_mod.shape
    r = cs.shape[0]
    tn = 1024
    return pl.pallas_call(
        _mod_kernel,
        grid=(depth, n // tn),
        in_specs=[pl.BlockSpec((r, d), lambda l, j: (0, 0)),
                  pl.BlockSpec((1, d, tn), lambda l, j: (l, 0, j)),
                  pl.BlockSpec((1, 1, tn), lambda l, j: (l, 0, j))],
        out_specs=pl.BlockSpec((1, r, tn), lambda l, j: (l, 0, j)),
        out_shape=jax.ShapeDtypeStruct((depth, r, n), F32),
        compiler_params=_params(2),
        name="adaln_modulation",
    )(cs, w_mod, b_mod.reshape(depth, 1, n))


class _TokenGrid:
    def __init__(self, stream, is_ctx, layer, mod_rows):
        b, n, self.d = stream.shape
        self.is_ctx = is_ctx
        self.shape = (1, b * n) if is_ctx else (b, n)
        self.tm = min(TOKEN_TILE, self.shape[1])
        self.grid = (self.shape[0], self.shape[1] // self.tm)
        self.orig = (b, n)
        self.mod_base = layer * mod_rows + (mod_rows - 1 if is_ctx else 0)

    def view(self, arr):
        return arr.reshape(self.shape + arr.shape[2:])

    def unview(self, arr):
        return arr.reshape(self.orig + arr.shape[2:])

    def tok(self, width):
        return pl.BlockSpec((1, self.tm, width), lambda i, j: (i, j, 0))

    def mod(self, chunk):
        base, per_batch = self.mod_base, 0 if self.is_ctx else 1
        return pl.BlockSpec((1, 1, self.d), lambda i, j: (base + per_batch * i, 0, chunk))

    def cast_specs(self, stacked, layer):
        steps = self.grid[0] * self.grid[1]
        rows, cols = stacked.shape[1:]
        slab = rows // steps
        assert slab * steps == rows and slab % 16 == 0, (rows, steps)
        per_row = self.grid[1]
        return (pl.BlockSpec((None, slab, cols), lambda i, j: (layer, i * per_row + j, 0)),
                pl.BlockSpec((slab, cols), lambda i, j: (i * per_row + j, 0)),
                jax.ShapeDtypeStruct((rows, cols), MXU_DTYPE))


def _cast_slabs(src_refs, dst_refs):
    for src, dst in zip(src_refs, dst_refs):
        dst[...] = src[...].astype(dst.dtype)


def _inproj_kernel(*refs, with_rope, n_cast, kv_only):
    x_ref, sh_ref, sc_ref, g_ref, w_ref = refs[:5]
    if with_rope:
        cos_ref, sin_ref = refs[5:7]
        refs = refs[7:]
    else:
        refs = refs[5:]
    qg_ref, kg_ref, gsum_ref = refs[:3]
    cast_src, refs = refs[3:3 + n_cast], refs[3 + n_cast:]
    if kv_only:
        rqk_ref, rv_ref, nk_ref, nv_ref, gkv_ref = refs[:5]
        n_out = 5
    else:
        rqk_ref, rv_ref, rg_ref, nq_ref, nk_ref, nv_ref, gq_ref, gkv_ref, gate_ref = refs[:9]
        n_out = 9
    _cast_slabs(cast_src, refs[n_out:])

    x = x_ref[0]
    h = _rms(x) * g_ref[...]
    h = h * (1.0 + sc_ref[0]) + sh_ref[0]
    hb = h.astype(MXU_DTYPE)
    lane = lax.broadcasted_iota(jnp.int32, (x.shape[0], LANES), 1)
    gsum = gsum_ref[...]

    def proj(col, width):
        return _dot(hb, w_ref[:, col:col + width])

    if with_rope:
        cos, sin = cos_ref[...], sin_ref[...]
        first_half = (lane % 32) < 16

        def rope(v):
            partner = jnp.where(first_half, pltpu.roll(v, LANES - 16, 1), pltpu.roll(v, 16, 1))
            return v * cos + partner * sin
    else:
        def rope(v):
            return v

    def head_rms(v):
        ms = _dot((v * v).astype(MXU_DTYPE), gsum) * (1.0 / HEAD_DIM)
        return v * lax.rsqrt(ms + NORM_EPS)

    odt = rqk_ref.dtype
    ret_scale = HEAD_DIM ** -0.5
    att_scale = HEAD_DIM ** -0.5 * LOG2_E

    rk = proj(C_RK, 256)
    if kv_only:
        rqk_ref[0, :, 0:256] = jnp.zeros((x.shape[0], 256), odt)
    else:
        rq = proj(C_RQ, 256)
    for j in range(2):
        sl = slice(j * LANES, (j + 1) * LANES)
        if not kv_only:
            rqk_ref[0, :, j * LANES:(j + 1) * LANES] = rope(rq[:, sl]).astype(odt)
        rqk_ref[0, :, 256 + j * LANES:256 + (j + 1) * LANES] = (rope(rk[:, sl]) * ret_scale).astype(odt)
    rv_ref[0] = proj(C_RV, HEAD_W).astype(odt)
    if not kv_only:
        rg = proj(C_RG, HEAD_W)
        rg_ref[0] = (rg * jax.nn.sigmoid(rg)).astype(odt)
        nq_ref[0] = (proj(C_NQ, HEAD_W) * att_scale).astype(odt)
    nk_ref[0] = proj(C_NK, HEAD_W).astype(odt)
    nv = proj(C_NV, HEAD_W).astype(odt)
    for j in range(NA_HEADS // 2):
        nv_ref[0, :, 2 * j * LANES:(2 * j + 1) * LANES] = nv[:, j * LANES:(j + 1) * LANES]
        nv_ref[0, :, (2 * j + 1) * LANES:(2 * j + 2) * LANES] = jnp.ones((nv.shape[0], LANES), odt)

    if not kv_only:
        gq = proj(C_GQ, HEAD_W)
        qg = qg_ref[...]
        for j in range(2):
            normed = head_rms(gq[:, 2 * j * LANES:(2 * j + 2) * LANES])
            for i in range(2):
                sl = slice((2 * j + i) * LANES, (2 * j + i + 1) * LANES)
                gq_ref[0, :, sl] = (rope(normed[:, i * LANES:(i + 1) * LANES] * qg) * att_scale).astype(odt)
    gkv = proj(C_GK, 2 * LANES)
    gk = rope(head_rms(gkv)[:, 0:LANES] * kg_ref[...])
    gv = gkv[:, LANES:]
    low_half = lane < HEAD_DIM
    gk_sw, gv_sw = pltpu.roll(gk, HEAD_DIM, 1), pltpu.roll(gv, HEAD_DIM, 1)
    gkv_ref[0, :, 0:LANES] = jnp.where(low_half, gk, gk_sw).astype(odt)
    gkv_ref[0, :, LANES:2 * LANES] = jnp.where(low_half, gk_sw, gk).astype(odt)
    gkv_ref[0, :, 2 * LANES:3 * LANES] = jnp.where(low_half, gv, 1.0).astype(odt)
    gkv_ref[0, :, 3 * LANES:4 * LANES] = jnp.where(low_half, gv_sw, 1.0).astype(odt)

    if not kv_only:
        for j in range(6):
            sl = slice(j * HEAD_W, (j + 1) * HEAD_W)
            gate_ref[0, :, sl] = jax.nn.sigmoid(proj(C_GATE + j * HEAD_W, HEAD_W)).astype(odt)


def _in_projection(l, stream, is_ctx, mod3, mod_rows, layer_gain, w_in, rope_tables, q_gain, k_gain, gsum, cast=(),
                   kv_only=False):
    tg = _TokenGrid(stream, is_ctx, l, mod_rows)
    d = tg.d
    all_widths = [HEAD_W] * 5 + [2 * HEAD_W] + [HEAD_W] * 2 + [3 * d]
    made = [0, 1, 4, 5, 7] if kv_only else list(range(len(all_widths)))
    out_widths = [all_widths[k] for k in made]
    rope_specs = [] if is_ctx else [pl.BlockSpec((tg.tm, LANES), lambda i, j: (j, 0))] * 2
    rope_args = () if is_ctx else rope_tables
    cast_specs = [tg.cast_specs(a, l) for a in cast]
    outs = pl.pallas_call(
        functools.partial(_inproj_kernel, with_rope=not is_ctx, n_cast=len(cast), kv_only=kv_only),
        grid=tg.grid,
        in_specs=[tg.tok(d), tg.mod(0), tg.mod(1), _layer(layer_gain, l), _resident(w_in.shape)] + rope_specs
        + [_layer(q_gain, l), _layer(k_gain, l), _resident(gsum.shape)] + [s[0] for s in cast_specs],
        out_specs=[tg.tok(w) for w in out_widths] + [s[1] for s in cast_specs],
        out_shape=[jax.ShapeDtypeStruct(tg.shape + (w,), MXU_DTYPE) for w in out_widths] + [s[2] for s in cast_specs],
        compiler_params=_params(2),
        name="in_projection_ctx" if is_ctx else "in_projection",
    )(tg.view(stream), mod3, mod3, layer_gain, w_in, *rope_args, q_gain, k_gain, gsum, *cast)
    projections = [None] * len(all_widths)
    for k, o in zip(made, outs):
        projections[k] = tg.unview(o)
    return tuple(projections) + tuple(outs[len(out_widths):])


def _retention_kernel(*refs, with_ctx_out):
    if with_ctx_out:
        logit_ref, qk_c, v_c, g_c, qk_l, v_l, g_l, o_c, o_l, dec_ref, st_ref = refs
    else:
        logit_ref, qk_c, v_c, qk_l, v_l, g_l, o_l, dec_ref, st_ref = refs
        g_c = o_c = None
    c_len = RET_CHUNK
    n_ctx_chunks = qk_c.shape[1] // c_len
    n_lat_chunks = qk_l.shape[1] // c_len
    n_chunks = n_ctx_chunks + n_lat_chunks
    heads = RET_HEADS
    pairs = heads // 2
    t_mask, t_dkf, t_dkb, t_dsf, t_dsb = (k * heads for k in range(5))
    t_dqf, t_dqb = 5 * heads, 5 * heads + pairs
    cdt = qk_l.dtype

    row = lax.broadcasted_iota(jnp.int32, (c_len, LANES), 0)
    lane = lax.broadcasted_iota(jnp.int32, (c_len, LANES), 1)
    lane_lo = lane < HEAD_DIM
    row_lo = row < HEAD_DIM

    @pl.when(pl.program_id(0) == 0)
    def _build_tables():
        rowf = row.astype(F32)
        lanef = lane.astype(F32)
        rel = rowf - lanef
        lg = [[jnp.broadcast_to(jax.nn.log_sigmoid(logit_ref[dd, hh])[0:1, :], (c_len, LANES))
               for hh in range(heads)] for dd in range(2)]
        for hh in range(heads):
            lf, lb = lg[0][hh], lg[1][hh]
            fwd = jnp.where(rel >= 0, jnp.exp(lf * jnp.maximum(rel, 0.0)), 0.0)
            bwd = jnp.where(rel < 0, jnp.exp(lb * jnp.maximum(-rel, 0.0)), 0.0)
            dec_ref[t_mask + hh] = fwd + bwd
            dec_ref[t_dkf + hh] = jnp.exp(lf * (c_len - 1.0 - lanef))
            dec_ref[t_dkb + hh] = jnp.exp(lb * lanef)
            dec_ref[t_dsf + hh] = jnp.exp(lf * float(c_len))
            dec_ref[t_dsb + hh] = jnp.exp(lb * float(c_len))
        for p in range(pairs):
            lf = jnp.where(lane_lo, lg[0][2 * p], lg[0][2 * p + 1])
            lb = jnp.where(lane_lo, lg[1][2 * p], lg[1][2 * p + 1])
            dec_ref[t_dqf + p] = jnp.exp(lf * (rowf + 1.0))
            dec_ref[t_dqb + p] = jnp.exp(lb * (float(c_len) - rowf))

    def summaries(qk_ref, v_ref, rows, c):
        for p in range(pairs):
            kt = qk_ref[0, rows, 256 + p * LANES:256 + (p + 1) * LANES].astype(F32).T
            for half in range(2):
                hh = 2 * p + half
                ktm = jnp.where(row_lo if half == 0 else jnp.logical_not(row_lo), kt, 0.0)
                lhs = jnp.concatenate([ktm * dec_ref[t_dkf + hh], ktm * dec_ref[t_dkb + hh]], axis=0)
                st_ref[c, hh] = _dot(lhs.astype(cdt), v_ref[0, rows, hh * LANES:(hh + 1) * LANES])

    def outputs(qk_ref, v_ref, g_ref, o_ref, rows, c):
        zero = jnp.zeros((), cdt)
        for p in range(pairs):
            q = qk_ref[0, rows, p * LANES:(p + 1) * LANES]
            k = qk_ref[0, rows, 256 + p * LANES:256 + (p + 1) * LANES]
            qs = jnp.concatenate([jnp.where(lane_lo, q, zero), jnp.where(lane_lo, zero, q)], axis=0)
            a = _dot_nt(qs, k)
            q32 = q.astype(F32)
            qf = (q32 * dec_ref[t_dqf + p]).astype(cdt)
            qb = (q32 * dec_ref[t_dqb + p]).astype(cdt)
            for half in range(2):
                hh = 2 * p + half
                sl = slice(hh * LANES, (hh + 1) * LANES)
                pm = (a[half * c_len:(half + 1) * c_len] * dec_ref[t_mask + hh]).astype(cdt)
                lhs = jnp.concatenate([pm, qf, qb], axis=1)
                rhs = jnp.concatenate([v_ref[0, rows, sl], st_ref[c, hh].astype(cdt)], axis=0)
                y = _rms(_dot(lhs, rhs)) * g_ref[0, rows, sl].astype(F32)
                o_ref[0, rows, sl] = y.astype(o_ref.dtype)

    def lat_rows(i):
        return pl.ds(pl.multiple_of(i * c_len, c_len), c_len)

    for ci in range(n_ctx_chunks):
        summaries(qk_c, v_c, slice(ci * c_len, (ci + 1) * c_len), ci)

    def lat_summaries(i, carry):
        summaries(qk_l, v_l, lat_rows(i), n_ctx_chunks + i)
        return carry

    lax.fori_loop(0, n_lat_chunks, lat_summaries, 0, unroll=True)

    fwd_order = list(range(n_chunks))
    bwd_order = list(range(n_ctx_chunks - 1, -1, -1)) + list(range(n_chunks - 1, n_ctx_chunks - 1, -1))
    for hh in range(heads):
        for order, lo, t_ds in ((fwd_order, 0, t_dsf), (bwd_order, c_len, t_dsb)):
            decay = dec_ref[t_ds + hh]
            state = jnp.zeros((c_len, LANES), F32)
            for c in order:
                z = st_ref[c, hh, lo:lo + c_len, :]
                st_ref[c, hh, lo:lo + c_len, :] = state
                state = state * decay + z

    if with_ctx_out:
        for ci in range(n_ctx_chunks):
            outputs(qk_c, v_c, g_c, o_c, slice(ci * c_len, (ci + 1) * c_len), ci)

    def lat_outputs(i, carry):
        outputs(qk_l, v_l, g_l, o_l, lat_rows(i), n_ctx_chunks + i)
        return carry

    lax.fori_loop(0, n_lat_chunks, lat_outputs, 0, unroll=True)


def _retention(l, logit_tile, ctx_qvg, lat_qvg, with_ctx_out):
    b, n_ctx, _ = ctx_qvg[0].shape
    n_lat = lat_qvg[0].shape[1]
    n_tables = 5 * RET_HEADS + RET_HEADS
    out_shape = [jax.ShapeDtypeStruct((b, n_lat, HEAD_W), MXU_DTYPE)]
    if with_ctx_out:
        out_shape.insert(0, jax.ShapeDtypeStruct((b, n_ctx, HEAD_W), MXU_DTYPE))
    outs = pl.pallas_call(
        functools.partial(_retention_kernel, with_ctx_out=with_ctx_out),
        grid=(b,),
        in_specs=[_layer(logit_tile, l)] + [_batch_block(a) for a in (*ctx_qvg, *lat_qvg)],
        out_specs=[_batch_block(s) for s in out_shape],
        out_shape=out_shape,
        scratch_shapes=[pltpu.VMEM((n_tables, RET_CHUNK, LANES), F32),
                        pltpu.VMEM(((n_ctx + n_lat) // RET_CHUNK, RET_HEADS, 2 * RET_CHUNK, LANES), F32)],
        compiler_params=_params(1),
        name="retention",
    )(logit_tile, *ctx_qvg, *lat_qvg)
    return (outs[0], outs[1]) if with_ctx_out else (None, outs[0])


def _na_kernel(*refs, with_ctx_out):
    if with_ctx_out:
        q_c, k_c, v_c, q_l, k_l, v_l, bias_ref, bias_shift_ref, bias_pad_ref, o_c, o_l, s_ref, p_ref = refs
    else:
        k_c, v_c, q_l, k_l, v_l, bias_ref, bias_shift_ref, bias_pad_ref, o_l, s_ref, p_ref = refs
    n_ctx = k_c.shape[1]
    n_rows = q_l.shape[1] // GRID_W
    n_blocks = n_rows // NA_BLOCK_ROWS
    qb = NA_BLOCK_ROWS * GRID_W
    un = NA_UNION_ROWS * GRID_W
    win = NA_ROWS * GRID_W
    half_win = NA_ROWS // 2
    cdt = p_ref.dtype

    def stacked(q):
        lo = lax.broadcasted_iota(jnp.int32, q.shape, 1) < HEAD_DIM
        zero = jnp.zeros((), q.dtype)
        return jnp.concatenate([jnp.where(lo, q, zero), jnp.where(lo, zero, q)], axis=0)

    def unstack(o):
        m = o.shape[0] // 2
        lo = lax.broadcasted_iota(jnp.int32, (m, LANES), 1) < HEAD_DIM
        o = o[:, 0:LANES] / o[:, LANES:]
        return jnp.where(lo, o[0:m], o[m:])

    def softmax(parts):
        m = parts[0].max(axis=-1, keepdims=True)
        for s in parts[1:]:
            m = jnp.maximum(m, s.max(axis=-1, keepdims=True))
        return [jnp.exp2(s - m) for s in parts]

    def values(v_ref, rows, p):
        return v_ref[0, rows, 2 * p * LANES:(2 * p + 2) * LANES]

    def block(g, config):
        if config == "first":
            u0 = 0
        elif config == "last":
            u0 = n_rows - NA_UNION_ROWS
        else:
            u0 = g * NA_BLOCK_ROWS - half_win
        q_rows = pl.ds(pl.multiple_of(g * qb, qb), qb)
        k_rows = pl.ds(pl.multiple_of(u0 * GRID_W, GRID_W), un)

        def scores(p):
            sl = slice(p * LANES, (p + 1) * LANES)
            qs = stacked(q_l[0, q_rows, sl])
            s_ref[p % 2, :, 0:un] = _dot_nt(qs, k_l[0, k_rows, sl])
            s_ref[p % 2, :, un:] = _dot_nt(qs, k_c[0, :, sl])

        scores(0)
        for p in range(NA_HEADS // 2):
            sl = slice(p * LANES, (p + 1) * LANES)
            pbuf, sbuf = p_ref.at[p % 2], s_ref.at[p % 2]
            if p + 1 < NA_HEADS // 2:
                scores(p + 1)
            for half in range(2):
                hh = 2 * p + half
                for a in range(NA_BLOCK_ROWS):
                    rows = slice(half * qb + a * GRID_W, half * qb + (a + 1) * GRID_W)
                    if config == "first":
                        w, off = 0, NA_ROWS - 1 - a
                    elif config == "last":
                        w, off = (n_rows - NA_ROWS - u0) * GRID_W, half_win - 1 - a
                    else:
                        w, off = a * GRID_W, half_win - 1
                    if w % LANES:
                        lo, width, bias = w - GRID_W, win + 2 * GRID_W, bias_pad_ref[hh]
                    elif off % 2:
                        lo, width, bias = w, win, bias_shift_ref[hh, :, (off - 1) * GRID_W:(off - 1) * GRID_W + win]
                    else:
                        lo, width, bias = w, win, bias_ref[hh, :, off * GRID_W:off * GRID_W + win]
                    p_loc, p_cx = softmax([sbuf[rows, lo:lo + width] + bias, sbuf[rows, un:]])
                    pieces = [p_loc, jnp.zeros((GRID_W, un - lo - width), F32), p_cx]
                    if lo:
                        pieces = [jnp.zeros((GRID_W, lo), F32)] + pieces
                    pbuf[rows, :] = jnp.concatenate([x for x in pieces if x.shape[1]], axis=1).astype(cdt)
            o = _dot(pbuf[:, 0:un], values(v_l, k_rows, p)) + _dot(pbuf[:, un:], values(v_c, slice(None), p))
            o_l[0, q_rows, sl] = unstack(o).astype(o_l.dtype)

    block(0, "first")

    def middle(g, carry):
        block(g, "middle")
        return carry

    lax.fori_loop(1, n_blocks - 1, middle, 0, unroll=True)
    block(n_blocks - 1, "last")

    if with_ctx_out:
        for p in range(NA_HEADS // 2):
            sl = slice(p * LANES, (p + 1) * LANES)
            (pn,) = softmax([_dot_nt(stacked(q_c[0, :, sl]), k_c[0, :, sl])])
            o = _dot(pn.astype(cdt), values(v_c, slice(None), p))
            o_c[0, :, sl] = unstack(o).astype(o_c.dtype)


def _neighbourhood(l, ctx_qkv, lat_qkv, bias_tables, with_ctx_out):
    b, n_ctx, _ = ctx_qkv[0].shape
    n_lat = lat_qkv[0].shape[1]
    p_cols = NA_UNION_ROWS * GRID_W + n_ctx
    stacked_rows = 2 * NA_BLOCK_ROWS * GRID_W
    out_shape = [jax.ShapeDtypeStruct((b, n_lat, HEAD_W), MXU_DTYPE)]
    if with_ctx_out:
        out_shape.insert(0, jax.ShapeDtypeStruct((b, n_ctx, HEAD_W), MXU_DTYPE))
    outs = pl.pallas_call(
        functools.partial(_na_kernel, with_ctx_out=with_ctx_out),
        grid=(b,),
        in_specs=[_batch_block(a) for a in (*ctx_qkv, *lat_qkv)] + [_layer(tbl, l) for tbl in bias_tables],
        out_specs=[_batch_block(s) for s in out_shape],
        out_shape=out_shape,
        scratch_shapes=[pltpu.VMEM((2, stacked_rows, p_cols), F32),
                        pltpu.VMEM((2, stacked_rows, p_cols), MXU_DTYPE)],
        compiler_params=_params(1),
        name="neighbourhood_attention",
    )(*ctx_qkv, *lat_qkv, *bias_tables)
    return (outs[0], outs[1]) if with_ctx_out else (None, outs[0])


def _na_bias_table(rel_bias):
    col = np.arange(GRID_W)
    col_start = np.clip(col - NA_COLS // 2, 0, GRID_W - NA_COLS)
    in_window = (col[None, :] >= col_start[:, None]) & (col[None, :] < col_start[:, None] + NA_COLS)
    dcol = np.clip(col[None, :] - col[:, None], 1 - NA_COLS, NA_COLS - 1) + NA_COLS - 1
    onehot = (dcol[None] == np.arange(2 * NA_COLS - 1)[:, None, None]) & in_window[None]
    rest = rel_bias.astype(F32) * LOG2_E
    cb = None
    for _ in range(3):
        piece = rest.astype(MXU_DTYPE)
        rest = rest - piece.astype(F32)
        part = jnp.einsum("lhdc,cqk->lhqdk", piece, jnp.asarray(onehot, MXU_DTYPE), preferred_element_type=F32)
        cb = part if cb is None else cb + part
    cb = jnp.where(jnp.asarray(in_window)[:, None, :], cb, NEG_INF)
    table = cb.reshape(cb.shape[:3] + ((2 * NA_ROWS - 1) * GRID_W,))
    mid = (NA_ROWS // 2 - 1) * GRID_W
    side = jnp.full(cb.shape[:3] + (GRID_W,), NEG_INF, F32)
    padded = jnp.concatenate([side, table[..., mid:mid + NA_ROWS * GRID_W], side], axis=-1)
    return table, table[..., GRID_W:], padded


def _gqa_tile(q_ref, o_ref, q_rows, kv_refs, chunks, s_ref, p_ref):
    tq = GQA_Q_TILE
    n_kv = GQA_KV_HEADS
    rb = GQA_ROW_BLOCK
    n_blk = GQA_Q_HEADS // n_kv * tq // rb
    per_head = tq // rb
    lo = lax.broadcasted_iota(jnp.int32, (tq, LANES), 1) < HEAD_DIM
    zero = jnp.zeros((), q_ref.dtype)

    def stacked_queries(g):
        parts = []
        for j in range(2 * g, 2 * g + 2):
            q = q_ref[0, q_rows, j * LANES:(j + 1) * LANES]
            parts += [jnp.where(lo, q, zero), jnp.where(lo, zero, q)]
        return jnp.concatenate(parts, axis=0)

    items = [(g, ci) for g in range(n_kv) for ci in range(len(chunks))]
    widths = [sum(c1 - c0 for _, c0, c1 in pieces) for pieces in chunks]
    qs = [stacked_queries(g) for g in range(n_kv)]

    def rows_of(ci, lane_tile):
        pieces = [kv_refs[src][0, c0:c1, lane_tile * LANES:(lane_tile + 1) * LANES] for src, c0, c1 in chunks[ci]]
        return pieces[0] if len(pieces) == 1 else jnp.concatenate(pieces, axis=0)

    def scores(idx):
        g, ci = items[idx]
        s_ref[idx % 2, :, 0:widths[ci]] = _dot_nt(qs[g], rows_of(ci, g))

    scores(0)
    m = acc = None
    for idx, (g, ci) in enumerate(items):
        w, slot = widths[ci], idx % 2
        if idx + 1 < len(items):
            scores(idx + 1)
        if ci == 0:
            m, acc = [None] * n_blk, [None] * n_blk
        alpha = [None] * n_blk
        for i in range(n_blk):
            rows = slice(i * rb, (i + 1) * rb)
            s = s_ref[slot, rows, 0:w]
            s_max = s.max(axis=-1, keepdims=True)
            m_new = s_max if ci == 0 else jnp.maximum(m[i], s_max)
            if ci:
                alpha[i] = jnp.exp2(m[i] - m_new)
            m[i] = m_new
            p_ref[slot, rows, 0:w] = jnp.exp2(s - m_new).astype(p_ref.dtype)
        pv = _dot(p_ref[slot, :, 0:w], rows_of(ci, n_kv + g))
        for i in range(n_blk):
            part = pv[i * rb:(i + 1) * rb]
            acc[i] = part if ci == 0 else alpha[i] * acc[i] + part
        if ci + 1 == len(chunks):
            for j in range(2):
                halves = []
                for half in range(2):
                    h = 2 * j + half
                    a = jnp.concatenate(acc[h * per_head:(h + 1) * per_head], axis=0)
                    a_sw = pltpu.roll(a, HEAD_DIM, 1)
                    halves.append(a / a_sw if half == 0 else a_sw / a)
                pair = 2 * g + j
                o_ref[0, q_rows, pair * LANES:(pair + 1) * LANES] = jnp.where(lo, halves[0], halves[1]).astype(o_ref.dtype)


def _gqa_kernel(*refs, lat_chunks, ctx_chunks):
    if ctx_chunks is None:
        kv_c, q_l, kv_l, o_l, s_ref, p_ref = refs
    else:
        q_c, kv_c, q_l, kv_l, o_c, o_l, s_ref, p_ref = refs
    tq = GQA_Q_TILE

    def tile(j, carry):
        _gqa_tile(q_l, o_l, pl.ds(pl.multiple_of(j * tq, tq), tq), (kv_c, kv_l), lat_chunks, s_ref, p_ref)
        return carry

    lax.fori_loop(0, q_l.shape[1] // tq, tile, 0)
    if ctx_chunks is not None:
        _gqa_tile(q_c, o_c, slice(None), (kv_c,), ctx_chunks, s_ref, p_ref)


def _key_chunks(sizes):
    starts = [0]
    for n in sizes:
        starts.append(starts[-1] + n)
    chunks = []
    for c0 in range(0, starts[-1], GQA_KEY_CHUNK):
        c1 = min(c0 + GQA_KEY_CHUNK, starts[-1])
        pieces = [(src, max(c0, starts[src]) - starts[src], min(c1, starts[src + 1]) - starts[src])
                  for src in range(len(sizes)) if max(c0, starts[src]) < min(c1, starts[src + 1])]
        chunks.append(tuple(pieces))
    return tuple(chunks)


def _gqa(q_lat, kv_ctx, kv_lat, q_ctx=None):
    b, n_lat, _ = q_lat.shape
    n_ctx = kv_ctx.shape[1]
    assert n_lat % GQA_Q_TILE == 0 and n_ctx == GQA_Q_TILE
    with_ctx = q_ctx is not None
    stacked_rows = GQA_Q_HEADS // GQA_KV_HEADS * GQA_Q_TILE
    ins = ([q_ctx] if with_ctx else []) + [kv_ctx, q_lat, kv_lat]
    out_shape = [jax.ShapeDtypeStruct(q_lat.shape, MXU_DTYPE)]
    if with_ctx:
        out_shape.insert(0, jax.ShapeDtypeStruct(q_ctx.shape, MXU_DTYPE))
    outs = pl.pallas_call(
        functools.partial(_gqa_kernel, lat_chunks=_key_chunks((n_ctx, n_lat)),
                          ctx_chunks=_key_chunks((n_ctx,)) if with_ctx else None),
        grid=(b,),
        in_specs=[_batch_block(a) for a in ins],
        out_specs=[_batch_block(s) for s in out_shape],
        out_shape=out_shape,
        scratch_shapes=[pltpu.VMEM((2, stacked_rows, GQA_KEY_CHUNK), F32),
                        pltpu.VMEM((2, stacked_rows, GQA_KEY_CHUNK), MXU_DTYPE)],
        compiler_params=_params(1),
        name="gqa_attention",
    )(*ins)
    return (outs[0], outs[1]) if with_ctx else (None, outs[0])


def _post_kernel(yr_ref, yn_ref, yg_ref, gate_ref, s_ref, gt1_ref, sh2_ref, sc2_ref, gt2_ref,
                 gpost_ref, gpre_ref, gpost2_ref, wr_ref, wn_ref, wg_ref, wout_ref, w1_ref, w2_ref, *rest):
    n_cast = (len(rest) - 1) // 2
    o_ref = rest[n_cast]
    _cast_slabs(rest[:n_cast], rest[n_cast + 1:])
    tm, d = o_ref.shape[1:]

    def mixed(rows):
        y = None
        for i, (br, w_ref) in enumerate(((yr_ref, wr_ref), (yn_ref, wn_ref), (yg_ref, wg_ref))):
            z = _dot(br[0, rows, :], w_ref[...]) * gate_ref[0, rows, i * d:(i + 1) * d].astype(F32)
            y = z if y is None else y + z
        return _dot(y.astype(MXU_DTYPE), wout_ref[...])

    def mlp(rows, y):
        x = s_ref[0, rows, :] + gt1_ref[0] * (_rms(y) * gpost_ref[...])
        h = _rms(x) * gpre_ref[...]
        h = (h * (1.0 + sc2_ref[0]) + sh2_ref[0]).astype(MXU_DTYPE)
        acc = None
        for j in range(D_FF // d):
            u = jnp.maximum(_dot(h, w1_ref[:, j * d:(j + 1) * d]), 0.0)
            part = _dot((u * u).astype(MXU_DTYPE), w2_ref[j * d:(j + 1) * d, :])
            acc = part if acc is None else acc + part
        o_ref[0, rows, :] = x + gt2_ref[0] * (_rms(acc) * gpost2_ref[...])

    groups = [slice(r0, r0 + POST_ROW_GROUP) for r0 in range(0, tm, POST_ROW_GROUP)]
    ys = [mixed(rows) for rows in groups]
    for rows, y in zip(groups, ys):
        mlp(rows, y)


def _post(l, y_ret, y_na, y_gqa, gates, stream, is_ctx, mod3, mod_rows, gains, weights, cast=()):
    tg = _TokenGrid(stream, is_ctx, l, mod_rows)
    d = tg.d
    cast_specs = [tg.cast_specs(a, l + 1) for a in cast]
    outs = pl.pallas_call(
        _post_kernel,
        grid=tg.grid,
        in_specs=[tg.tok(HEAD_W), tg.tok(HEAD_W), tg.tok(HEAD_W), tg.tok(3 * d), tg.tok(d)]
        + [tg.mod(chunk) for chunk in (2, 3, 4, 5)]
        + [_layer(a, l) for a in gains] + [_resident(w.shape) for w in weights] + [s[0] for s in cast_specs],
        out_specs=[tg.tok(d)] + [s[1] for s in cast_specs],
        out_shape=[jax.ShapeDtypeStruct(tg.shape + (d,), F32)] + [s[2] for s in cast_specs],
        compiler_params=_params(2),
        name="merge_mlp_ctx" if is_ctx else "merge_mlp",
    )(tg.view(y_ret), tg.view(y_na), tg.view(y_gqa), tg.view(gates), tg.view(stream),
      mod3, mod3, mod3, mod3, *gains, *weights, *cast)
    return (tg.unview(outs[0]),) + tuple(outs[1:])


def _rope_tables(n_latent):
    t = np.arange(n_latent)
    pos = np.stack([t // GRID_W, t % GRID_W], axis=-1).astype(np.float64)
    n_freq = HEAD_DIM // 4
    inv_freq = ROPE_BASE ** (-np.arange(n_freq, dtype=np.float64) / n_freq)
    lane = np.arange(LANES) % HEAD_DIM
    axis, second, freq = lane // 32, (lane % 32) // 16, lane % 16
    ang = pos[:, axis] * inv_freq[freq][None, :]
    return jnp.asarray(np.cos(ang), F32), jnp.asarray(np.sin(ang) * np.where(second == 1, 1.0, -1.0), F32)


def kernel(x, c, ctx, c_ctx, w_mod, b_mod, g_pre_mix, g_post_mix, g_pre_mlp, g_post_mlp, w_in, ret_decay_logit, na_rel_bias, gqa_q_norm, gqa_k_norm, w_br_ret, w_br_na, w_br_gqa, w_out, w_mlp_in, w_mlp_out):
    b, n, d = x.shape
    depth = w_mod.shape[0]
    cdt = MXU_DTYPE

    rope_tables = _rope_tables(n)
    lane_head = np.arange(2 * LANES) // HEAD_DIM
    gsum = jnp.asarray(lane_head[:, None] == lane_head[None, :], cdt)

    n_rows = -(-(b + 1) // 8) * 8
    cs = jnp.concatenate([c, jnp.zeros((n_rows - b - 1, d), F32), c_ctx[None, :]], axis=0)
    mod3 = _modulation(cs, w_mod, b_mod).reshape(depth * n_rows, 1, 6 * d)

    w_in_l = w_in[0].astype(cdt)
    post_weight_stacks = (w_br_ret, w_br_na, w_br_gqa, w_out, w_mlp_in, w_mlp_out)
    post_gains = tuple(g[:, None, :] for g in (g_post_mix, g_pre_mlp, g_post_mlp))
    pre_gain = g_pre_mix[:, None, :]
    q_gain = jnp.tile(gqa_q_norm, (1, 2))[:, None, :]
    k_gain = jnp.tile(gqa_k_norm, (1, 2))[:, None, :]
    logit_tile = jnp.broadcast_to(ret_decay_logit.astype(F32)[..., None, None], (depth, 2, RET_HEADS, 8, LANES))
    bias_tables = _na_bias_table(na_rel_bias)

    ctx_s, lat_s = ctx, x
    for l in range(depth):
        last = l == depth - 1
        pc = _in_projection(l, ctx_s, True, mod3, n_rows, pre_gain, w_in_l, rope_tables, q_gain, k_gain, gsum,
                            kv_only=last)
        pt = _in_projection(l, lat_s, False, mod3, n_rows, pre_gain, w_in_l, rope_tables, q_gain, k_gain, gsum,
                            cast=post_weight_stacks)
        post_weights = pt[9:]
        ret_c, ret_l = _retention(l, logit_tile, pc[0:2] if last else pc[0:3], pt[0:3], not last)
        na_c, na_l = _neighbourhood(l, pc[4:6] if last else pc[3:6], pt[3:6], bias_tables, not last)
        gqa_c, gqa_l = _gqa(pt[6], pc[7], pt[7], None if last else pc[6])
        post_l = _post(l, ret_l, na_l, gqa_l, pt[8], lat_s, False, mod3, n_rows, post_gains, post_weights,
                       cast=() if last else (w_in,))
        lat_s = post_l[0]
        if not last:
            w_in_l = post_l[1]
            ctx_s, = _post(l, ret_c, na_c, gqa_c, pc[8], ctx_s, True, mod3, n_rows, post_gains, post_weights)
    return lat_s
```

```python
import functools

import jax
import jax.numpy as jnp
import numpy as np
from jax import lax
from jax.experimental import pallas as pl
from jax.experimental.pallas import tpu as pltpu

F32 = jnp.float32
MXU_DTYPE = jnp.bfloat16

D_MODEL = 1024
GRID_W = 64
CTX_LEN = 256
HEAD_DIM = 64
RET_HEADS = 4
RET_CHUNK = 128
NA_HEADS = 8
NA_ROWS = 8
NA_COLS = 16
NA_BLOCK_ROWS = 4
NA_UNION_ROWS = NA_BLOCK_ROWS + NA_ROWS
GQA_Q_HEADS = 8
GQA_KV_HEADS = 2
GQA_Q_TILE = 256
GQA_KEY_CHUNK = 768
GQA_ROW_BLOCK = 64
D_FF = 4 * D_MODEL
ROPE_BASE = 10000.0
NORM_EPS = 1e-6
NEG_INF = -1e30
LOG2_E = 1.4426950408889634

LANES = 128
HEAD_W = 512
C_RQ, C_RK, C_RV, C_RG = 0, 256, 512, 1024
C_NQ, C_NK, C_NV = 1536, 2048, 2560
C_GQ, C_GK, C_GV = 3072, 3584, 3712
C_GATE = 3840
IN_W = 6912

TOKEN_TILE = 512
POST_ROW_GROUP = 256
VMEM_LIMIT = 56 * 1024 * 1024


def _rms(x):
    return x * lax.rsqrt(jnp.mean(x * x, axis=-1, keepdims=True) + NORM_EPS)


def _dot(a, b):
    return jnp.dot(a, b, preferred_element_type=F32)


def _dot_nt(a, b):
    return lax.dot_general(a, b, (((1,), (1,)), ((), ())), preferred_element_type=F32)


def _resident(shape):
    return pl.BlockSpec(shape, lambda *_: (0,) * len(shape), pipeline_mode=pl.Buffered(1))


def _layer(arr, l):
    return pl.BlockSpec((None,) + arr.shape[1:], lambda *_: (l,) + (0,) * (arr.ndim - 1),
                        pipeline_mode=pl.Buffered(1))


def _batch_block(arr):
    return pl.BlockSpec((1,) + arr.shape[1:], lambda i, *_: (i, 0, 0))


def _params(n_axes):
    return pltpu.CompilerParams(dimension_semantics=("arbitrary",) * n_axes,
                                vmem_limit_bytes=VMEM_LIMIT)


def _mod_kernel(c_ref, w_ref, b_ref, o_ref):
    c = c_ref[...]
    s = (c * jax.nn.sigmoid(c)).astype(MXU_DTYPE)
    o_ref[0] = _dot(s, w_ref[0].astype(MXU_DTYPE)) + b_ref[0]


def _modulation(cs, w_mod, b_mod):
    depth, d, n = w_mod.shape
    r = cs.shape[0]
    tn = 3072
    return pl.pallas_call(
        _mod_kernel,
        grid=(depth, n // tn),
        in_specs=[pl.BlockSpec((r, d), lambda l, j: (0, 0)),
                  pl.BlockSpec((1, d, tn), lambda l, j: (l, 0, j)),
                  pl.BlockSpec((1, 1, tn), lambda l, j: (l, 0, j))],
        out_specs=pl.BlockSpec((1, r, tn), lambda l, j: (l, 0, j)),
        out_shape=jax.ShapeDtypeStruct((depth, r, n), F32),
        compiler_params=_params(2),
        name="adaln_modulation",
    )(cs, w_mod, b_mod.reshape(depth, 1, n))


class _TokenGrid:
    def __init__(self, stream, is_ctx, layer, mod_rows):
        b, n, self.d = stream.shape
        self.is_ctx = is_ctx
        self.shape = (1, b * n) if is_ctx else (b, n)
        self.tm = min(TOKEN_TILE, self.shape[1])
        self.grid = (self.shape[0], self.shape[1] // self.tm)
        self.orig = (b, n)
        self.mod_base = layer * mod_rows + (mod_rows - 1 if is_ctx else 0)

    def view(self, arr):
        return arr.reshape(self.shape + arr.shape[2:])

    def unview(self, arr):
        return arr.reshape(self.orig + arr.shape[2:])

    def tok(self, width):
        return pl.BlockSpec((1, self.tm, width), lambda i, j: (i, j, 0))

    def mod(self, chunk):
        base, per_batch = self.mod_base, 0 if self.is_ctx else 1
        return pl.BlockSpec((1, 1, self.d), lambda i, j: (base + per_batch * i, 0, chunk))

    def cast_specs(self, stacked, layer):
        steps = self.grid[0] * self.grid[1]
        rows, cols = stacked.shape[1:]
        slab = rows // steps
        assert slab * steps == rows and slab % 16 == 0, (rows, steps)
        per_row = self.grid[1]
        return (pl.BlockSpec((None, slab, cols), lambda i, j: (layer, i * per_row + j, 0)),
                pl.BlockSpec((slab, cols), lambda i, j: (i * per_row + j, 0)),
                jax.ShapeDtypeStruct((rows, cols), MXU_DTYPE))


def _cast_slabs(src_refs, dst_refs):
    for src, dst in zip(src_refs, dst_refs):
        dst[...] = src[...].astype(dst.dtype)


def _inproj_kernel(*refs, with_rope, n_cast, kv_only):
    x_ref, sh_ref, sc_ref, g_ref, w_ref = refs[:5]
    if with_rope:
        cos_ref, sin_ref = refs[5:7]
        refs = refs[7:]
    else:
        refs = refs[5:]
    qg_ref, kg_ref, gsum_ref = refs[:3]
    cast_src, refs = refs[3:3 + n_cast], refs[3 + n_cast:]
    if kv_only:
        rqk_ref, rv_ref, nk_ref, nv_ref, gkv_ref = refs[:5]
        n_out = 5
    else:
        rqk_ref, rv_ref, rg_ref, nq_ref, nk_ref, nv_ref, gq_ref, gkv_ref, gate_ref = refs[:9]
        n_out = 9
    _cast_slabs(cast_src, refs[n_out:])

    x = x_ref[0]
    h = _rms(x) * g_ref[...]
    h = h * (1.0 + sc_ref[0]) + sh_ref[0]
    hb = h.astype(MXU_DTYPE)
    lane = lax.broadcasted_iota(jnp.int32, (x.shape[0], LANES), 1)
    gsum = gsum_ref[...]

    def proj(col, width):
        return _dot(hb, w_ref[:, col:col + width])

    if with_rope:
        cos, sin = cos_ref[...], sin_ref[...]
        first_half = (lane % 32) < 16

        def rope(v):
            partner = jnp.where(first_half, pltpu.roll(v, LANES - 16, 1), pltpu.roll(v, 16, 1))
            return v * cos + partner * sin
    else:
        def rope(v):
            return v

    def head_rms(v):
        ms = _dot((v * v).astype(MXU_DTYPE), gsum) * (1.0 / HEAD_DIM)
        return v * lax.rsqrt(ms + NORM_EPS)

    odt = rqk_ref.dtype
    ret_scale = HEAD_DIM ** -0.5
    att_scale = HEAD_DIM ** -0.5 * LOG2_E

    rk = proj(C_RK, 256)
    if kv_only:
        rqk_ref[0, :, 0:256] = jnp.zeros((x.shape[0], 256), odt)
    else:
        rq = proj(C_RQ, 256)
    for j in range(2):
        sl = slice(j * LANES, (j + 1) * LANES)
        if not kv_only:
            rqk_ref[0, :, j * LANES:(j + 1) * LANES] = rope(rq[:, sl]).astype(odt)
        rqk_ref[0, :, 256 + j * LANES:256 + (j + 1) * LANES] = (rope(rk[:, sl]) * ret_scale).astype(odt)
    rv_ref[0] = proj(C_RV, HEAD_W).astype(odt)
    if not kv_only:
        rg = proj(C_RG, HEAD_W)
        rg_ref[0] = (rg * jax.nn.sigmoid(rg)).astype(odt)
        nq_ref[0] = (proj(C_NQ, HEAD_W) * att_scale).astype(odt)
    nk_ref[0] = proj(C_NK, HEAD_W).astype(odt)
    nv = proj(C_NV, HEAD_W).astype(odt)
    for j in range(NA_HEADS // 2):
        nv_ref[0, :, 2 * j * LANES:(2 * j + 1) * LANES] = nv[:, j * LANES:(j + 1) * LANES]
        nv_ref[0, :, (2 * j + 1) * LANES:(2 * j + 2) * LANES] = jnp.ones((nv.shape[0], LANES), odt)

    if not kv_only:
        gq = proj(C_GQ, HEAD_W)
        qg = qg_ref[...]
        for j in range(2):
            normed = head_rms(gq[:, 2 * j * LANES:(2 * j + 2) * LANES])
            for i in range(2):
                sl = slice((2 * j + i) * LANES, (2 * j + i + 1) * LANES)
                gq_ref[0, :, sl] = (rope(normed[:, i * LANES:(i + 1) * LANES] * qg) * att_scale).astype(odt)
    gkv = proj(C_GK, 2 * LANES)
    gk = rope(head_rms(gkv)[:, 0:LANES] * kg_ref[...])
    gv = gkv[:, LANES:]
    low_half = lane < HEAD_DIM
    gk_sw, gv_sw = pltpu.roll(gk, HEAD_DIM, 1), pltpu.roll(gv, HEAD_DIM, 1)
    gkv_ref[0, :, 0:LANES] = jnp.where(low_half, gk, gk_sw).astype(odt)
    gkv_ref[0, :, LANES:2 * LANES] = jnp.where(low_half, gk_sw, gk).astype(odt)
    gkv_ref[0, :, 2 * LANES:3 * LANES] = jnp.where(low_half, gv, 1.0).astype(odt)
    gkv_ref[0, :, 3 * LANES:4 * LANES] = jnp.where(low_half, gv_sw, 1.0).astype(odt)

    if not kv_only:
        for j in range(6):
            sl = slice(j * HEAD_W, (j + 1) * HEAD_W)
            gate_ref[0, :, sl] = jax.nn.sigmoid(proj(C_GATE + j * HEAD_W, HEAD_W)).astype(odt)


def _in_projection(l, stream, is_ctx, mod3, mod_rows, layer_gain, w_in, rope_tables, q_gain, k_gain, gsum, cast=(),
                   kv_only=False):
    tg = _TokenGrid(stream, is_ctx, l, mod_rows)
    d = tg.d
    all_widths = [HEAD_W] * 5 + [2 * HEAD_W] + [HEAD_W] * 2 + [3 * d]
    made = [0, 1, 4, 5, 7] if kv_only else list(range(len(all_widths)))
    out_widths = [all_widths[k] for k in made]
    rope_specs = [] if is_ctx else [pl.BlockSpec((tg.tm, LANES), lambda i, j: (j, 0))] * 2
    rope_args = () if is_ctx else rope_tables
    cast_specs = [tg.cast_specs(a, l) for a in cast]
    outs = pl.pallas_call(
        functools.partial(_inproj_kernel, with_rope=not is_ctx, n_cast=len(cast), kv_only=kv_only),
        grid=tg.grid,
        in_specs=[tg.tok(d), tg.mod(0), tg.mod(1), _layer(layer_gain, l), _resident(w_in.shape)] + rope_specs
        + [_layer(q_gain, l), _layer(k_gain, l), _resident(gsum.shape)] + [s[0] for s in cast_specs],
        out_specs=[tg.tok(w) for w in out_widths] + [s[1] for s in cast_specs],
        out_shape=[jax.ShapeDtypeStruct(tg.shape + (w,), MXU_DTYPE) for w in out_widths] + [s[2] for s in cast_specs],
        compiler_params=_params(2),
        name="in_projection_ctx" if is_ctx else "in_projection",
    )(tg.view(stream), mod3, mod3, layer_gain, w_in, *rope_args, q_gain, k_gain, gsum, *cast)
    projections = [None] * len(all_widths)
    for k, o in zip(made, outs):
        projections[k] = tg.unview(o)
    return tuple(projections) + tuple(outs[len(out_widths):])


def _retention_kernel(*refs, with_ctx_out):
    if with_ctx_out:
        logit_ref, qk_c, v_c, g_c, qk_l, v_l, g_l, o_c, o_l, dec_ref, st_ref = refs
    else:
        logit_ref, qk_c, v_c, qk_l, v_l, g_l, o_l, dec_ref, st_ref = refs
        g_c = o_c = None
    c_len = RET_CHUNK
    n_ctx_chunks = qk_c.shape[1] // c_len
    n_lat_chunks = qk_l.shape[1] // c_len
    n_chunks = n_ctx_chunks + n_lat_chunks
    heads = RET_HEADS
    pairs = heads // 2
    t_mask, t_dkf, t_dkb, t_dsf, t_dsb = (k * heads for k in range(5))
    t_dqf, t_dqb = 5 * heads, 5 * heads + pairs
    cdt = qk_l.dtype

    row = lax.broadcasted_iota(jnp.int32, (c_len, LANES), 0)
    lane = lax.broadcasted_iota(jnp.int32, (c_len, LANES), 1)
    lane_lo = lane < HEAD_DIM
    row_lo = row < HEAD_DIM

    @pl.when(pl.program_id(0) == 0)
    def _build_tables():
        rowf = row.astype(F32)
        lanef = lane.astype(F32)
        rel = rowf - lanef
        lg = [[jnp.broadcast_to(jax.nn.log_sigmoid(logit_ref[dd, hh])[0:1, :], (c_len, LANES))
               for hh in range(heads)] for dd in range(2)]
        for hh in range(heads):
            lf, lb = lg[0][hh], lg[1][hh]
            fwd = jnp.where(rel >= 0, jnp.exp(lf * jnp.maximum(rel, 0.0)), 0.0)
            bwd = jnp.where(rel < 0, jnp.exp(lb * jnp.maximum(-rel, 0.0)), 0.0)
            dec_ref[t_mask + hh] = fwd + bwd
            dec_ref[t_dkf + hh] = jnp.exp(lf * (c_len - 1.0 - lanef))
            dec_ref[t_dkb + hh] = jnp.exp(lb * lanef)
            dec_ref[t_dsf + hh] = jnp.exp(lf * float(c_len))
            dec_ref[t_dsb + hh] = jnp.exp(lb * float(c_len))
        for p in range(pairs):
            lf = jnp.where(lane_lo, lg[0][2 * p], lg[0][2 * p + 1])
            lb = jnp.where(lane_lo, lg[1][2 * p], lg[1][2 * p + 1])
            dec_ref[t_dqf + p] = jnp.exp(lf * (rowf + 1.0))
            dec_ref[t_dqb + p] = jnp.exp(lb * (float(c_len) - rowf))

    def summaries(qk_ref, v_ref, rows, c):
        for p in range(pairs):
            kt = qk_ref[0, rows, 256 + p * LANES:256 + (p + 1) * LANES].astype(F32).T
            for half in range(2):
                hh = 2 * p + half
                ktm = jnp.where(row_lo if half == 0 else jnp.logical_not(row_lo), kt, 0.0)
                lhs = jnp.concatenate([ktm * dec_ref[t_dkf + hh], ktm * dec_ref[t_dkb + hh]], axis=0)
                st_ref[c, hh] = _dot(lhs.astype(cdt), v_ref[0, rows, hh * LANES:(hh + 1) * LANES])

    def outputs(qk_ref, v_ref, g_ref, o_ref, rows, c):
        zero = jnp.zeros((), cdt)
        for p in range(pairs):
            q = qk_ref[0, rows, p * LANES:(p + 1) * LANES]
            k = qk_ref[0, rows, 256 + p * LANES:256 + (p + 1) * LANES]
            qs = jnp.concatenate([jnp.where(lane_lo, q, zero), jnp.where(lane_lo, zero, q)], axis=0)
            a = _dot_nt(qs, k)
            q32 = q.astype(F32)
            qf = (q32 * dec_ref[t_dqf + p]).astype(cdt)
            qb = (q32 * dec_ref[t_dqb + p]).astype(cdt)
            for half in range(2):
                hh = 2 * p + half
                sl = slice(hh * LANES, (hh + 1) * LANES)
                pm = (a[half * c_len:(half + 1) * c_len] * dec_ref[t_mask + hh]).astype(cdt)
                lhs = jnp.concatenate([pm, qf, qb], axis=1)
                rhs = jnp.concatenate([v_ref[0, rows, sl], st_ref[c, hh].astype(cdt)], axis=0)
                y = _rms(_dot(lhs, rhs)) * g_ref[0, rows, sl].astype(F32)
                o_ref[0, rows, sl] = y.astype(o_ref.dtype)

    def lat_rows(i):
        return pl.ds(pl.multiple_of(i * c_len, c_len), c_len)

    for ci in range(n_ctx_chunks):
        summaries(qk_c, v_c, slice(ci * c_len, (ci + 1) * c_len), ci)

    def lat_summaries(i, carry):
        summaries(qk_l, v_l, lat_rows(i), n_ctx_chunks + i)
        return carry

    lax.fori_loop(0, n_lat_chunks, lat_summaries, 0, unroll=True)

    fwd_order = list(range(n_chunks))
    bwd_order = list(range(n_ctx_chunks - 1, -1, -1)) + list(range(n_chunks - 1, n_ctx_chunks - 1, -1))
    for hh in range(heads):
        for order, lo, t_ds in ((fwd_order, 0, t_dsf), (bwd_order, c_len, t_dsb)):
            decay = dec_ref[t_ds + hh]
            state = jnp.zeros((c_len, LANES), F32)
            for c in order:
                z = st_ref[c, hh, lo:lo + c_len, :]
                st_ref[c, hh, lo:lo + c_len, :] = state
                state = state * decay + z

    if with_ctx_out:
        for ci in range(n_ctx_chunks):
            outputs(qk_c, v_c, g_c, o_c, slice(ci * c_len, (ci + 1) * c_len), ci)

    def lat_outputs(i, carry):
        outputs(qk_l, v_l, g_l, o_l, lat_rows(i), n_ctx_chunks + i)
        return carry

    lax.fori_loop(0, n_lat_chunks, lat_outputs, 0, unroll=True)


def _retention(l, logit_tile, ctx_qvg, lat_qvg, with_ctx_out):
    b, n_ctx, _ = ctx_qvg[0].shape
    n_lat = lat_qvg[0].shape[1]
    n_tables = 5 * RET_HEADS + RET_HEADS
    out_shape = [jax.ShapeDtypeStruct((b, n_lat, HEAD_W), MXU_DTYPE)]
    if with_ctx_out:
        out_shape.insert(0, jax.ShapeDtypeStruct((b, n_ctx, HEAD_W), MXU_DTYPE))
    outs = pl.pallas_call(
        functools.partial(_retention_kernel, with_ctx_out=with_ctx_out),
        grid=(b,),
        in_specs=[_layer(logit_tile, l)] + [_batch_block(a) for a in (*ctx_qvg, *lat_qvg)],
        out_specs=[_batch_block(s) for s in out_shape],
        out_shape=out_shape,
        scratch_shapes=[pltpu.VMEM((n_tables, RET_CHUNK, LANES), F32),
                        pltpu.VMEM(((n_ctx + n_lat) // RET_CHUNK, RET_HEADS, 2 * RET_CHUNK, LANES), F32)],
        compiler_params=_params(1),
        name="retention",
    )(logit_tile, *ctx_qvg, *lat_qvg)
    return (outs[0], outs[1]) if with_ctx_out else (None, outs[0])


def _na_kernel(*refs, with_ctx_out):
    if with_ctx_out:
        q_c, k_c, v_c, q_l, k_l, v_l, bias_ref, bias_shift_ref, bias_pad_ref, o_c, o_l, s_ref, p_ref = refs
    else:
        k_c, v_c, q_l, k_l, v_l, bias_ref, bias_shift_ref, bias_pad_ref, o_l, s_ref, p_ref = refs
    n_ctx = k_c.shape[1]
    n_rows = q_l.shape[1] // GRID_W
    n_blocks = n_rows // NA_BLOCK_ROWS
    qb = NA_BLOCK_ROWS * GRID_W
    un = NA_UNION_ROWS * GRID_W
    win = NA_ROWS * GRID_W
    half_win = NA_ROWS // 2
    cdt = p_ref.dtype

    def stacked(q):
        lo = lax.broadcasted_iota(jnp.int32, q.shape, 1) < HEAD_DIM
        zero = jnp.zeros((), q.dtype)
        return jnp.concatenate([jnp.where(lo, q, zero), jnp.where(lo, zero, q)], axis=0)

    def unstack(o):
        m = o.shape[0] // 2
        lo = lax.broadcasted_iota(jnp.int32, (m, LANES), 1) < HEAD_DIM
        o = o[:, 0:LANES] / o[:, LANES:]
        return jnp.where(lo, o[0:m], o[m:])

    def softmax(parts):
        m = parts[0].max(axis=-1, keepdims=True)
        for s in parts[1:]:
            m = jnp.maximum(m, s.max(axis=-1, keepdims=True))
        return [jnp.exp2(s - m) for s in parts]

    def values(v_ref, rows, p):
        return v_ref[0, rows, 2 * p * LANES:(2 * p + 2) * LANES]

    def block(g, config):
        if config == "first":
            u0, ku = 0, win
        elif config == "last":
            u0, ku = n_rows - NA_ROWS, win
        else:
            u0, ku = g * NA_BLOCK_ROWS - half_win, un
        q_rows = pl.ds(pl.multiple_of(g * qb, qb), qb)
        k_rows = pl.ds(pl.multiple_of(u0 * GRID_W, GRID_W), ku)

        def scores(p):
            sl = slice(p * LANES, (p + 1) * LANES)
            qs = stacked(q_l[0, q_rows, sl])
            s_ref[p % 2, :, 0:ku] = _dot_nt(qs, k_l[0, k_rows, sl])
            s_ref[p % 2, :, ku:ku + n_ctx] = _dot_nt(qs, k_c[0, :, sl])

        scores(0)
        for p in range(NA_HEADS // 2):
            sl = slice(p * LANES, (p + 1) * LANES)
            pbuf, sbuf = p_ref.at[p % 2], s_ref.at[p % 2]
            if p + 1 < NA_HEADS // 2:
                scores(p + 1)
            for half in range(2):
                hh = 2 * p + half
                for a in range(NA_BLOCK_ROWS):
                    rows = slice(half * qb + a * GRID_W, half * qb + (a + 1) * GRID_W)
                    if config == "first":
                        w, off = 0, NA_ROWS - 1 - a
                    elif config == "last":
                        w, off = 0, half_win - 1 - a
                    else:
                        w, off = a * GRID_W, half_win - 1
                    if w % LANES:
                        lo, width, bias = w - GRID_W, win + 2 * GRID_W, bias_pad_ref[hh]
                    elif off % 2:
                        lo, width, bias = w, win, bias_shift_ref[hh, :, (off - 1) * GRID_W:(off - 1) * GRID_W + win]
                    else:
                        lo, width, bias = w, win, bias_ref[hh, :, off * GRID_W:off * GRID_W + win]
                    p_loc, p_cx = softmax([sbuf[rows, lo:lo + width] + bias, sbuf[rows, ku:ku + n_ctx]])
                    pieces = [p_loc, jnp.zeros((GRID_W, ku - lo - width), F32), p_cx]
                    if lo:
                        pieces = [jnp.zeros((GRID_W, lo), F32)] + pieces
                    pbuf[rows, 0:ku + n_ctx] = jnp.concatenate([x for x in pieces if x.shape[1]], axis=1).astype(cdt)
            o = (_dot(pbuf[:, 0:ku], values(v_l, k_rows, p))
                 + _dot(pbuf[:, ku:ku + n_ctx], values(v_c, slice(None), p)))
            o_l[0, q_rows, sl] = unstack(o).astype(o_l.dtype)

    block(0, "first")

    def middle(g, carry):
        block(g, "middle")
        return carry

    lax.fori_loop(1, n_blocks - 1, middle, 0, unroll=True)
    block(n_blocks - 1, "last")

    if with_ctx_out:
        for p in range(NA_HEADS // 2):
            sl = slice(p * LANES, (p + 1) * LANES)
            (pn,) = softmax([_dot_nt(stacked(q_c[0, :, sl]), k_c[0, :, sl])])
            o = _dot(pn.astype(cdt), values(v_c, slice(None), p))
            o_c[0, :, sl] = unstack(o).astype(o_c.dtype)


def _neighbourhood(l, ctx_qkv, lat_qkv, bias_tables, with_ctx_out):
    b, n_ctx, _ = ctx_qkv[0].shape
    n_lat = lat_qkv[0].shape[1]
    p_cols = NA_UNION_ROWS * GRID_W + n_ctx
    stacked_rows = 2 * NA_BLOCK_ROWS * GRID_W
    out_shape = [jax.ShapeDtypeStruct((b, n_lat, HEAD_W), MXU_DTYPE)]
    if with_ctx_out:
        out_shape.insert(0, jax.ShapeDtypeStruct((b, n_ctx, HEAD_W), MXU_DTYPE))
    outs = pl.pallas_call(
        functools.partial(_na_kernel, with_ctx_out=with_ctx_out),
        grid=(b,),
        in_specs=[_batch_block(a) for a in (*ctx_qkv, *lat_qkv)] + [_layer(tbl, l) for tbl in bias_tables],
        out_specs=[_batch_block(s) for s in out_shape],
        out_shape=out_shape,
        scratch_shapes=[pltpu.VMEM((2, stacked_rows, p_cols), F32),
                        pltpu.VMEM((2, stacked_rows, p_cols), MXU_DTYPE)],
        compiler_params=_params(1),
        name="neighbourhood_attention",
    )(*ctx_qkv, *lat_qkv, *bias_tables)
    return (outs[0], outs[1]) if with_ctx_out else (None, outs[0])


def _na_bias_table(rel_bias):
    col = np.arange(GRID_W)
    col_start = np.clip(col - NA_COLS // 2, 0, GRID_W - NA_COLS)
    in_window = (col[None, :] >= col_start[:, None]) & (col[None, :] < col_start[:, None] + NA_COLS)
    dcol = np.clip(col[None, :] - col[:, None], 1 - NA_COLS, NA_COLS - 1) + NA_COLS - 1
    onehot = (dcol[None] == np.arange(2 * NA_COLS - 1)[:, None, None]) & in_window[None]
    rest = rel_bias.astype(F32) * LOG2_E
    cb = None
    for _ in range(3):
        piece = rest.astype(MXU_DTYPE)
        rest = rest - piece.astype(F32)
        part = jnp.einsum("lhdc,cqk->lhqdk", piece, jnp.asarray(onehot, MXU_DTYPE), preferred_element_type=F32)
        cb = part if cb is None else cb + part
    cb = jnp.where(jnp.asarray(in_window)[:, None, :], cb, NEG_INF)
    table = cb.reshape(cb.shape[:3] + ((2 * NA_ROWS - 1) * GRID_W,))
    mid = (NA_ROWS // 2 - 1) * GRID_W
    side = jnp.full(cb.shape[:3] + (GRID_W,), NEG_INF, F32)
    padded = jnp.concatenate([side, table[..., mid:mid + NA_ROWS * GRID_W], side], axis=-1)
    return table, table[..., GRID_W:], padded


def _gqa_tile(q_ref, o_ref, q_rows, kv_refs, chunks, s_ref, p_ref):
    tq = GQA_Q_TILE
    n_kv = GQA_KV_HEADS
    rb = GQA_ROW_BLOCK
    n_blk = GQA_Q_HEADS // n_kv * tq // rb
    per_head = tq // rb
    lo = lax.broadcasted_iota(jnp.int32, (tq, LANES), 1) < HEAD_DIM
    zero = jnp.zeros((), q_ref.dtype)

    def stacked_queries(g):
        parts = []
        for j in range(2 * g, 2 * g + 2):
            q = q_ref[0, q_rows, j * LANES:(j + 1) * LANES]
            parts += [jnp.where(lo, q, zero), jnp.where(lo, zero, q)]
        return jnp.concatenate(parts, axis=0)

    items = [(g, ci) for g in range(n_kv) for ci in range(len(chunks))]
    widths = [sum(c1 - c0 for _, c0, c1 in pieces) for pieces in chunks]
    qs = [stacked_queries(g) for g in range(n_kv)]

    def rows_of(ci, lane_tile):
        pieces = [kv_refs[src][0, c0:c1, lane_tile * LANES:(lane_tile + 1) * LANES] for src, c0, c1 in chunks[ci]]
        return pieces[0] if len(pieces) == 1 else jnp.concatenate(pieces, axis=0)

    def scores(idx):
        g, ci = items[idx]
        s_ref[idx % 2, :, 0:widths[ci]] = _dot_nt(qs[g], rows_of(ci, g))

    scores(0)
    m = acc = None
    for idx, (g, ci) in enumerate(items):
        w, slot = widths[ci], idx % 2
        if idx + 1 < len(items):
            scores(idx + 1)
        if ci == 0:
            m, acc = [None] * n_blk, [None] * n_blk
        alpha = [None] * n_blk
        for i in range(n_blk):
            rows = slice(i * rb, (i + 1) * rb)
            s = s_ref[slot, rows, 0:w]
            s_max = s.max(axis=-1, keepdims=True)
            m_new = s_max if ci == 0 else jnp.maximum(m[i], s_max)
            if ci:
                alpha[i] = jnp.exp2(m[i] - m_new)
            m[i] = m_new
            p_ref[slot, rows, 0:w] = jnp.exp2(s - m_new).astype(p_ref.dtype)
        pv = _dot(p_ref[slot, :, 0:w], rows_of(ci, n_kv + g))
        for i in range(n_blk):
            part = pv[i * rb:(i + 1) * rb]
            acc[i] = part if ci == 0 else alpha[i] * acc[i] + part
        if ci + 1 == len(chunks):
            for j in range(2):
                halves = []
                for half in range(2):
                    h = 2 * j + half
                    a = jnp.concatenate(acc[h * per_head:(h + 1) * per_head], axis=0)
                    a_sw = pltpu.roll(a, HEAD_DIM, 1)
                    halves.append(a / a_sw if half == 0 else a_sw / a)
                pair = 2 * g + j
                o_ref[0, q_rows, pair * LANES:(pair + 1) * LANES] = jnp.where(lo, halves[0], halves[1]).astype(o_ref.dtype)


def _gqa_kernel(*refs, lat_chunks, ctx_chunks):
    if ctx_chunks is None:
        kv_c, q_l, kv_l, o_l, s_ref, p_ref = refs
    else:
        q_c, kv_c, q_l, kv_l, o_c, o_l, s_ref, p_ref = refs
    tq = GQA_Q_TILE

    def tile(j, carry):
        _gqa_tile(q_l, o_l, pl.ds(pl.multiple_of(j * tq, tq), tq), (kv_c, kv_l), lat_chunks, s_ref, p_ref)
        return carry

    lax.fori_loop(0, q_l.shape[1] // tq, tile, 0)
    if ctx_chunks is not None:
        _gqa_tile(q_c, o_c, slice(None), (kv_c,), ctx_chunks, s_ref, p_ref)


def _key_chunks(sizes):
    starts = [0]
    for n in sizes:
        starts.append(starts[-1] + n)
    chunks = []
    for c0 in range(0, starts[-1], GQA_KEY_CHUNK):
        c1 = min(c0 + GQA_KEY_CHUNK, starts[-1])
        pieces = [(src, max(c0, starts[src]) - starts[src], min(c1, starts[src + 1]) - starts[src])
                  for src in range(len(sizes)) if max(c0, starts[src]) < min(c1, starts[src + 1])]
        chunks.append(tuple(pieces))
    return tuple(chunks)


def _gqa(q_lat, kv_ctx, kv_lat, q_ctx=None):
    b, n_lat, _ = q_lat.shape
    n_ctx = kv_ctx.shape[1]
    assert n_lat % GQA_Q_TILE == 0 and n_ctx == GQA_Q_TILE
    with_ctx = q_ctx is not None
    stacked_rows = GQA_Q_HEADS // GQA_KV_HEADS * GQA_Q_TILE
    ins = ([q_ctx] if with_ctx else []) + [kv_ctx, q_lat, kv_lat]
    out_shape = [jax.ShapeDtypeStruct(q_lat.shape, MXU_DTYPE)]
    if with_ctx:
        out_shape.insert(0, jax.ShapeDtypeStruct(q_ctx.shape, MXU_DTYPE))
    outs = pl.pallas_call(
        functools.partial(_gqa_kernel, lat_chunks=_key_chunks((n_ctx, n_lat)),
                          ctx_chunks=_key_chunks((n_ctx,)) if with_ctx else None),
        grid=(b,),
        in_specs=[_batch_block(a) for a in ins],
        out_specs=[_batch_block(s) for s in out_shape],
        out_shape=out_shape,
        scratch_shapes=[pltpu.VMEM((2, stacked_rows, GQA_KEY_CHUNK), F32),
                        pltpu.VMEM((2, stacked_rows, GQA_KEY_CHUNK), MXU_DTYPE)],
        compiler_params=_params(1),
        name="gqa_attention",
    )(*ins)
    return (outs[0], outs[1]) if with_ctx else (None, outs[0])


def _post_kernel(yr_ref, yn_ref, yg_ref, gate_ref, s_ref, gt1_ref, sh2_ref, sc2_ref, gt2_ref,
                 gpost_ref, gpre_ref, gpost2_ref, wr_ref, wn_ref, wg_ref, wout_ref, w1_ref, w2_ref, *rest):
    n_cast = (len(rest) - 1) // 2
    o_ref = rest[n_cast]
    _cast_slabs(rest[:n_cast], rest[n_cast + 1:])
    tm, d = o_ref.shape[1:]

    def mixed(rows):
        y = None
        for i, (br, w_ref) in enumerate(((yr_ref, wr_ref), (yn_ref, wn_ref), (yg_ref, wg_ref))):
            z = _dot(br[0, rows, :], w_ref[...]) * gate_ref[0, rows, i * d:(i + 1) * d].astype(F32)
            y = z if y is None else y + z
        return _dot(y.astype(MXU_DTYPE), wout_ref[...])

    def mlp(rows, y):
        x = s_ref[0, rows, :] + gt1_ref[0] * (_rms(y) * gpost_ref[...])
        h = _rms(x) * gpre_ref[...]
        h = (h * (1.0 + sc2_ref[0]) + sh2_ref[0]).astype(MXU_DTYPE)
        acc = None
        for j in range(D_FF // d):
            u = jnp.maximum(_dot(h, w1_ref[:, j * d:(j + 1) * d]), 0.0)
            part = _dot((u * u).astype(MXU_DTYPE), w2_ref[j * d:(j + 1) * d, :])
            acc = part if acc is None else acc + part
        o_ref[0, rows, :] = x + gt2_ref[0] * (_rms(acc) * gpost2_ref[...])

    groups = [slice(r0, r0 + POST_ROW_GROUP) for r0 in range(0, tm, POST_ROW_GROUP)]
    ys = [mixed(rows) for rows in groups]
    for rows, y in zip(groups, ys):
        mlp(rows, y)


def _post(l, y_ret, y_na, y_gqa, gates, stream, is_ctx, mod3, mod_rows, gains, weights, cast=()):
    tg = _TokenGrid(stream, is_ctx, l, mod_rows)
    d = tg.d
    cast_specs = [tg.cast_specs(a, l + 1) for a in cast]
    outs = pl.pallas_call(
        _post_kernel,
        grid=tg.grid,
        in_specs=[tg.tok(HEAD_W), tg.tok(HEAD_W), tg.tok(HEAD_W), tg.tok(3 * d), tg.tok(d)]
        + [tg.mod(chunk) for chunk in (2, 3, 4, 5)]
        + [_layer(a, l) for a in gains] + [_resident(w.shape) for w in weights] + [s[0] for s in cast_specs],
        out_specs=[tg.tok(d)] + [s[1] for s in cast_specs],
        out_shape=[jax.ShapeDtypeStruct(tg.shape + (d,), F32)] + [s[2] for s in cast_specs],
        compiler_params=_params(2),
        name="merge_mlp_ctx" if is_ctx else "merge_mlp",
    )(tg.view(y_ret), tg.view(y_na), tg.view(y_gqa), tg.view(gates), tg.view(stream),
      mod3, mod3, mod3, mod3, *gains, *weights, *cast)
    return (tg.unview(outs[0]),) + tuple(outs[1:])


def _rope_tables(n_latent):
    t = np.arange(n_latent)
    pos = np.stack([t // GRID_W, t % GRID_W], axis=-1).astype(np.float64)
    n_freq = HEAD_DIM // 4
    inv_freq = ROPE_BASE ** (-np.arange(n_freq, dtype=np.float64) / n_freq)
    lane = np.arange(LANES) % HEAD_DIM
    axis, second, freq = lane // 32, (lane % 32) // 16, lane % 16
    ang = pos[:, axis] * inv_freq[freq][None, :]
    return jnp.asarray(np.cos(ang), F32), jnp.asarray(np.sin(ang) * np.where(second == 1, 1.0, -1.0), F32)


def kernel(x, c, ctx, c_ctx, w_mod, b_mod, g_pre_mix, g_post_mix, g_pre_mlp, g_post_mlp, w_in, ret_decay_logit, na_rel_bias, gqa_q_norm, gqa_k_norm, w_br_ret, w_br_na, w_br_gqa, w_out, w_mlp_in, w_mlp_out):
    b, n, d = x.shape
    depth = w_mod.shape[0]
    cdt = MXU_DTYPE

    rope_tables = _rope_tables(n)
    lane_head = np.arange(2 * LANES) // HEAD_DIM
    gsum = jnp.asarray(lane_head[:, None] == lane_head[None, :], cdt)

    n_rows = -(-(b + 1) // 8) * 8
    cs = jnp.concatenate([c, jnp.zeros((n_rows - b - 1, d), F32), c_ctx[None, :]], axis=0)
    mod3 = _modulation(cs, w_mod, b_mod).reshape(depth * n_rows, 1, 6 * d)

    w_in_l = w_in[0].astype(cdt)
    post_weight_stacks = (w_br_ret, w_br_na, w_br_gqa, w_out, w_mlp_in, w_mlp_out)
    post_gains = tuple(g[:, None, :] for g in (g_post_mix, g_pre_mlp, g_post_mlp))
    pre_gain = g_pre_mix[:, None, :]
    q_gain = jnp.tile(gqa_q_norm, (1, 2))[:, None, :]
    k_gain = jnp.tile(gqa_k_norm, (1, 2))[:, None, :]
    logit_tile = jnp.broadcast_to(ret_decay_logit.astype(F32)[..., None, None], (depth, 2, RET_HEADS, 8, LANES))
    bias_tables = _na_bias_table(na_rel_bias)

    ctx_s, lat_s = ctx, x
    for l in range(depth):
        last = l == depth - 1
        pc = _in_projection(l, ctx_s, True, mod3, n_rows, pre_gain, w_in_l, rope_tables, q_gain, k_gain, gsum,
                            kv_only=last)
        pt = _in_projection(l, lat_s, False, mod3, n_rows, pre_gain, w_in_l, rope_tables, q_gain, k_gain, gsum,
                            cast=post_weight_stacks)
        post_weights = pt[9:]
        ret_c, ret_l = _retention(l, logit_tile, pc[0:2] if last else pc[0:3], pt[0:3], not last)
        na_c, na_l = _neighbourhood(l, pc[4:6] if last else pc[3:6], pt[3:6], bias_tables, not last)
        gqa_c, gqa_l = _gqa(pt[6], pc[7], pt[7], None if last else pc[6])
        post_l = _post(l, ret_l, na_l, gqa_l, pt[8], lat_s, False, mod3, n_rows, post_gains, post_weights,
                       cast=() if last else (w_in,))
        lat_s = post_l[0]
        if not last:
            w_in_l = post_l[1]
            ctx_s, = _post(l, ret_c, na_c, gqa_c, pc[8], ctx_s, True, mod3, n_rows, post_gains, post_weights)
    return lat_s
```

```python
import functools

import jax
import jax.numpy as jnp
import numpy as np
from jax import lax
from jax.experimental import pallas as pl
from jax.experimental.pallas import tpu as pltpu

F32 = jnp.float32
MXU_DTYPE = jnp.bfloat16

D_MODEL = 1024
GRID_W = 64
CTX_LEN = 256
HEAD_DIM = 64
RET_HEADS = 4
RET_CHUNK = 128
NA_HEADS = 8
NA_ROWS = 8
NA_COLS = 16
NA_BLOCK_ROWS = 4
NA_UNION_ROWS = NA_BLOCK_ROWS + NA_ROWS
GQA_Q_HEADS = 8
GQA_KV_HEADS = 2
GQA_Q_TILE = 256
GQA_KEY_CHUNK = 768
GQA_ROW_BLOCK = 64
D_FF = 4 * D_MODEL
ROPE_BASE = 10000.0
NORM_EPS = 1e-6
NEG_INF = -1e30
LOG2_E = 1.4426950408889634

LANES = 128
HEAD_W = 512
C_RQ, C_RK, C_RV, C_RG = 0, 256, 512, 1024
C_NQ, C_NK, C_NV = 1536, 2048, 2560
C_GQ, C_GK, C_GV = 3072, 3584, 3712
C_GATE = 3840
IN_W = 6912

TOKEN_TILE = 512
POST_ROW_GROUP = 256
VMEM_LIMIT = 56 * 1024 * 1024


def _rms(x):
    return x * lax.rsqrt(jnp.mean(x * x, axis=-1, keepdims=True) + NORM_EPS)


def _dot(a, b):
    return jnp.dot(a, b, preferred_element_type=F32)


def _dot_nt(a, b):
    return lax.dot_general(a, b, (((1,), (1,)), ((), ())), preferred_element_type=F32)


def _resident(shape):
    return pl.BlockSpec(shape, lambda *_: (0,) * len(shape), pipeline_mode=pl.Buffered(1))


def _layer(arr, l):
    return pl.BlockSpec((None,) + arr.shape[1:], lambda *_: (l,) + (0,) * (arr.ndim - 1),
                        pipeline_mode=pl.Buffered(1))


def _batch_block(arr):
    return pl.BlockSpec((1,) + arr.shape[1:], lambda i, *_: (i, 0, 0))


def _params(n_axes):
    return pltpu.CompilerParams(dimension_semantics=("arbitrary",) * n_axes,
                                vmem_limit_bytes=VMEM_LIMIT)


def _mod_kernel(c_ref, w_ref, b_ref, win_ref, o_ref, wout_ref):
    c = c_ref[...]
    s = (c * jax.nn.sigmoid(c)).astype(MXU_DTYPE)
    o_ref[0] = _dot(s, w_ref[0].astype(MXU_DTYPE)) + b_ref[0]
    wout_ref[...] = win_ref[...].astype(wout_ref.dtype)


def _modulation(cs, w_mod, b_mod, w_in):
    depth, d, n = w_mod.shape
    r = cs.shape[0]
    tn = 1536
    per_layer = n // tn
    rows, cols = w_in.shape[1:]
    slab = rows // (depth * per_layer)
    assert slab * depth * per_layer == rows and slab % 16 == 0
    return pl.pallas_call(
        _mod_kernel,
        grid=(depth, per_layer),
        in_specs=[pl.BlockSpec((r, d), lambda l, j: (0, 0)),
                  pl.BlockSpec((1, d, tn), lambda l, j: (l, 0, j)),
                  pl.BlockSpec((1, 1, tn), lambda l, j: (l, 0, j)),
                  pl.BlockSpec((None, slab, cols), lambda l, j: (0, l * per_layer + j, 0))],
        out_specs=[pl.BlockSpec((1, r, tn), lambda l, j: (l, 0, j)),
                   pl.BlockSpec((slab, cols), lambda l, j: (l * per_layer + j, 0))],
        out_shape=[jax.ShapeDtypeStruct((depth, r, n), F32), jax.ShapeDtypeStruct((rows, cols), MXU_DTYPE)],
        compiler_params=_params(2),
        name="adaln_modulation",
    )(cs, w_mod, b_mod.reshape(depth, 1, n), w_in)


class _TokenGrid:
    def __init__(self, stream, is_ctx, layer, mod_rows):
        b, n, self.d = stream.shape
        self.is_ctx = is_ctx
        self.shape = (1, b * n) if is_ctx else (b, n)
        self.tm = min(TOKEN_TILE, self.shape[1])
        self.grid = (self.shape[0], self.shape[1] // self.tm)
        self.orig = (b, n)
        self.mod_base = layer * mod_rows + (mod_rows - 1 if is_ctx else 0)

    def view(self, arr):
        return arr.reshape(self.shape + arr.shape[2:])

    def unview(self, arr):
        return arr.reshape(self.orig + arr.shape[2:])

    def tok(self, width):
        return pl.BlockSpec((1, self.tm, width), lambda i, j: (i, j, 0))

    def mod(self, chunk):
        base, per_batch = self.mod_base, 0 if self.is_ctx else 1
        return pl.BlockSpec((1, 1, self.d), lambda i, j: (base + per_batch * i, 0, chunk))

    def cast_specs(self, stacked, layer):
        steps = self.grid[0] * self.grid[1]
        rows, cols = stacked.shape[1:]
        slab = rows // steps
        assert slab * steps == rows and slab % 16 == 0, (rows, steps)
        per_row = self.grid[1]
        return (pl.BlockSpec((None, slab, cols), lambda i, j: (layer, i * per_row + j, 0)),
                pl.BlockSpec((slab, cols), lambda i, j: (i * per_row + j, 0)),
                jax.ShapeDtypeStruct((rows, cols), MXU_DTYPE))


def _cast_slabs(src_refs, dst_refs):
    for src, dst in zip(src_refs, dst_refs):
        dst[...] = src[...].astype(dst.dtype)


def _inproj_kernel(*refs, with_rope, n_cast, kv_only):
    x_ref, sh_ref, sc_ref, g_ref, w_ref = refs[:5]
    if with_rope:
        cos_ref, sin_ref = refs[5:7]
        refs = refs[7:]
    else:
        refs = refs[5:]
    qg_ref, kg_ref, gsum_ref = refs[:3]
    cast_src, refs = refs[3:3 + n_cast], refs[3 + n_cast:]
    if kv_only:
        rqk_ref, rv_ref, nk_ref, nv_ref, gkv_ref = refs[:5]
        n_out = 5
    else:
        rqk_ref, rv_ref, rg_ref, nq_ref, nk_ref, nv_ref, gq_ref, gkv_ref, gate_ref = refs[:9]
        n_out = 9
    _cast_slabs(cast_src, refs[n_out:])

    x = x_ref[0]
    h = _rms(x) * g_ref[...]
    h = h * (1.0 + sc_ref[0]) + sh_ref[0]
    hb = h.astype(MXU_DTYPE)
    lane = lax.broadcasted_iota(jnp.int32, (x.shape[0], LANES), 1)
    gsum = gsum_ref[...]

    def proj(col, width):
        return _dot(hb, w_ref[:, col:col + width])

    if with_rope:
        cos, sin = cos_ref[...], sin_ref[...]
        first_half = (lane % 32) < 16

        def rope(v):
            partner = jnp.where(first_half, pltpu.roll(v, LANES - 16, 1), pltpu.roll(v, 16, 1))
            return v * cos + partner * sin
    else:
        def rope(v):
            return v

    def head_rms(v):
        ms = _dot((v * v).astype(MXU_DTYPE), gsum) * (1.0 / HEAD_DIM)
        return v * lax.rsqrt(ms + NORM_EPS)

    odt = rqk_ref.dtype
    ret_scale = HEAD_DIM ** -0.5
    att_scale = HEAD_DIM ** -0.5 * LOG2_E

    rk = proj(C_RK, 256)
    if kv_only:
        rqk_ref[0, :, 0:256] = jnp.zeros((x.shape[0], 256), odt)
    else:
        rq = proj(C_RQ, 256)
    for j in range(2):
        sl = slice(j * LANES, (j + 1) * LANES)
        if not kv_only:
            rqk_ref[0, :, j * LANES:(j + 1) * LANES] = rope(rq[:, sl]).astype(odt)
        rqk_ref[0, :, 256 + j * LANES:256 + (j + 1) * LANES] = (rope(rk[:, sl]) * ret_scale).astype(odt)
    rv_ref[0] = proj(C_RV, HEAD_W).astype(odt)
    if not kv_only:
        rg = proj(C_RG, HEAD_W)
        rg_ref[0] = (rg * jax.nn.sigmoid(rg)).astype(odt)
        nq_ref[0] = (proj(C_NQ, HEAD_W) * att_scale).astype(odt)
    nk_ref[0] = proj(C_NK, HEAD_W).astype(odt)
    nv = proj(C_NV, HEAD_W).astype(odt)
    for j in range(NA_HEADS // 2):
        nv_ref[0, :, 2 * j * LANES:(2 * j + 1) * LANES] = nv[:, j * LANES:(j + 1) * LANES]
        nv_ref[0, :, (2 * j + 1) * LANES:(2 * j + 2) * LANES] = jnp.ones((nv.shape[0], LANES), odt)

    if not kv_only:
        gq = proj(C_GQ, HEAD_W)
        qg = qg_ref[...]
        for j in range(2):
            normed = head_rms(gq[:, 2 * j * LANES:(2 * j + 2) * LANES])
            for i in range(2):
                sl = slice((2 * j + i) * LANES, (2 * j + i + 1) * LANES)
                gq_ref[0, :, sl] = (rope(normed[:, i * LANES:(i + 1) * LANES] * qg) * att_scale).astype(odt)
    gkv = proj(C_GK, 2 * LANES)
    gk = rope(head_rms(gkv)[:, 0:LANES] * kg_ref[...])
    gv = gkv[:, LANES:]
    low_half = lane < HEAD_DIM
    gk_sw, gv_sw = pltpu.roll(gk, HEAD_DIM, 1), pltpu.roll(gv, HEAD_DIM, 1)
    gkv_ref[0, :, 0:LANES] = jnp.where(low_half, gk, gk_sw).astype(odt)
    gkv_ref[0, :, LANES:2 * LANES] = jnp.where(low_half, gk_sw, gk).astype(odt)
    gkv_ref[0, :, 2 * LANES:3 * LANES] = jnp.where(low_half, gv, 1.0).astype(odt)
    gkv_ref[0, :, 3 * LANES:4 * LANES] = jnp.where(low_half, gv_sw, 1.0).astype(odt)

    if not kv_only:
        for j in range(6):
            sl = slice(j * HEAD_W, (j + 1) * HEAD_W)
            gate_ref[0, :, sl] = jax.nn.sigmoid(proj(C_GATE + j * HEAD_W, HEAD_W)).astype(odt)


def _in_projection(l, stream, is_ctx, mod3, mod_rows, layer_gain, w_in, rope_tables, q_gain, k_gain, gsum, cast=(),
                   kv_only=False):
    tg = _TokenGrid(stream, is_ctx, l, mod_rows)
    d = tg.d
    all_widths = [HEAD_W] * 5 + [2 * HEAD_W] + [HEAD_W] * 2 + [3 * d]
    made = [0, 1, 4, 5, 7] if kv_only else list(range(len(all_widths)))
    out_widths = [all_widths[k] for k in made]
    rope_specs = [] if is_ctx else [pl.BlockSpec((tg.tm, LANES), lambda i, j: (j, 0))] * 2
    rope_args = () if is_ctx else rope_tables
    cast_specs = [tg.cast_specs(a, l) for a in cast]
    outs = pl.pallas_call(
        functools.partial(_inproj_kernel, with_rope=not is_ctx, n_cast=len(cast), kv_only=kv_only),
        grid=tg.grid,
        in_specs=[tg.tok(d), tg.mod(0), tg.mod(1), _layer(layer_gain, l), _resident(w_in.shape)] + rope_specs
        + [_layer(q_gain, l), _layer(k_gain, l), _resident(gsum.shape)] + [s[0] for s in cast_specs],
        out_specs=[tg.tok(w) for w in out_widths] + [s[1] for s in cast_specs],
        out_shape=[jax.ShapeDtypeStruct(tg.shape + (w,), MXU_DTYPE) for w in out_widths] + [s[2] for s in cast_specs],
        compiler_params=_params(2),
        name="in_projection_ctx" if is_ctx else "in_projection",
    )(tg.view(stream), mod3, mod3, layer_gain, w_in, *rope_args, q_gain, k_gain, gsum, *cast)
    projections = [None] * len(all_widths)
    for k, o in zip(made, outs):
        projections[k] = tg.unview(o)
    return tuple(projections) + tuple(outs[len(out_widths):])


def _retention_kernel(*refs, with_ctx_out):
    if with_ctx_out:
        logit_ref, qk_c, v_c, g_c, qk_l, v_l, g_l, o_c, o_l, dec_ref, st_ref = refs
    else:
        logit_ref, qk_c, v_c, qk_l, v_l, g_l, o_l, dec_ref, st_ref = refs
        g_c = o_c = None
    c_len = RET_CHUNK
    n_ctx_chunks = qk_c.shape[1] // c_len
    n_lat_chunks = qk_l.shape[1] // c_len
    n_chunks = n_ctx_chunks + n_lat_chunks
    heads = RET_HEADS
    pairs = heads // 2
    t_mask, t_dkf, t_dkb, t_dsf, t_dsb = (k * heads for k in range(5))
    t_dqf, t_dqb = 5 * heads, 5 * heads + pairs
    cdt = qk_l.dtype

    row = lax.broadcasted_iota(jnp.int32, (c_len, LANES), 0)
    lane = lax.broadcasted_iota(jnp.int32, (c_len, LANES), 1)
    lane_lo = lane < HEAD_DIM
    row_lo = row < HEAD_DIM

    @pl.when(pl.program_id(0) == 0)
    def _build_tables():
        rowf = row.astype(F32)
        lanef = lane.astype(F32)
        rel = rowf - lanef
        lg = [[jnp.broadcast_to(jax.nn.log_sigmoid(logit_ref[dd, hh])[0:1, :], (c_len, LANES))
               for hh in range(heads)] for dd in range(2)]
        for hh in range(heads):
            lf, lb = lg[0][hh], lg[1][hh]
            fwd = jnp.where(rel >= 0, jnp.exp(lf * jnp.maximum(rel, 0.0)), 0.0)
            bwd = jnp.where(rel < 0, jnp.exp(lb * jnp.maximum(-rel, 0.0)), 0.0)
            dec_ref[t_mask + hh] = fwd + bwd
            dec_ref[t_dkf + hh] = jnp.exp(lf * (c_len - 1.0 - lanef))
            dec_ref[t_dkb + hh] = jnp.exp(lb * lanef)
            dec_ref[t_dsf + hh] = jnp.exp(lf * float(c_len))
            dec_ref[t_dsb + hh] = jnp.exp(lb * float(c_len))
        for p in range(pairs):
            lf = jnp.where(lane_lo, lg[0][2 * p], lg[0][2 * p + 1])
            lb = jnp.where(lane_lo, lg[1][2 * p], lg[1][2 * p + 1])
            dec_ref[t_dqf + p] = jnp.exp(lf * (rowf + 1.0))
            dec_ref[t_dqb + p] = jnp.exp(lb * (float(c_len) - rowf))

    def summaries(qk_ref, v_ref, rows, c):
        for p in range(pairs):
            kt = qk_ref[0, rows, 256 + p * LANES:256 + (p + 1) * LANES].astype(F32).T
            for half in range(2):
                hh = 2 * p + half
                ktm = jnp.where(row_lo if half == 0 else jnp.logical_not(row_lo), kt, 0.0)
                lhs = jnp.concatenate([ktm * dec_ref[t_dkf + hh], ktm * dec_ref[t_dkb + hh]], axis=0)
                st_ref[c, hh] = _dot(lhs.astype(cdt), v_ref[0, rows, hh * LANES:(hh + 1) * LANES])

    def outputs(qk_ref, v_ref, g_ref, o_ref, rows, c):
        zero = jnp.zeros((), cdt)
        for p in range(pairs):
            q = qk_ref[0, rows, p * LANES:(p + 1) * LANES]
            k = qk_ref[0, rows, 256 + p * LANES:256 + (p + 1) * LANES]
            qs = jnp.concatenate([jnp.where(lane_lo, q, zero), jnp.where(lane_lo, zero, q)], axis=0)
            a = _dot_nt(qs, k)
            q32 = q.astype(F32)
            qf = (q32 * dec_ref[t_dqf + p]).astype(cdt)
            qb = (q32 * dec_ref[t_dqb + p]).astype(cdt)
            for half in range(2):
                hh = 2 * p + half
                sl = slice(hh * LANES, (hh + 1) * LANES)
                pm = (a[half * c_len:(half + 1) * c_len] * dec_ref[t_mask + hh]).astype(cdt)
                lhs = jnp.concatenate([pm, qf, qb], axis=1)
                rhs = jnp.concatenate([v_ref[0, rows, sl], st_ref[c, hh].astype(cdt)], axis=0)
                y = _rms(_dot(lhs, rhs)) * g_ref[0, rows, sl].astype(F32)
                o_ref[0, rows, sl] = y.astype(o_ref.dtype)

    def lat_rows(i):
        return pl.ds(pl.multiple_of(i * c_len, c_len), c_len)

    for ci in range(n_ctx_chunks):
        summaries(qk_c, v_c, slice(ci * c_len, (ci + 1) * c_len), ci)

    def lat_summaries(i, carry):
        summaries(qk_l, v_l, lat_rows(i), n_ctx_chunks + i)
        return carry

    lax.fori_loop(0, n_lat_chunks, lat_summaries, 0, unroll=True)

    fwd_order = list(range(n_chunks))
    bwd_order = list(range(n_ctx_chunks - 1, -1, -1)) + list(range(n_chunks - 1, n_ctx_chunks - 1, -1))
    for hh in range(heads):
        for order, lo, t_ds in ((fwd_order, 0, t_dsf), (bwd_order, c_len, t_dsb)):
            decay = dec_ref[t_ds + hh]
            state = jnp.zeros((c_len, LANES), F32)
            for c in order:
                z = st_ref[c, hh, lo:lo + c_len, :]
                st_ref[c, hh, lo:lo + c_len, :] = state
                state = state * decay + z

    if with_ctx_out:
        for ci in range(n_ctx_chunks):
            outputs(qk_c, v_c, g_c, o_c, slice(ci * c_len, (ci + 1) * c_len), ci)

    def lat_outputs(i, carry):
        outputs(qk_l, v_l, g_l, o_l, lat_rows(i), n_ctx_chunks + i)
        return carry

    lax.fori_loop(0, n_lat_chunks, lat_outputs, 0, unroll=True)


def _retention(l, logit_tile, ctx_qvg, lat_qvg, with_ctx_out):
    b, n_ctx, _ = ctx_qvg[0].shape
    n_lat = lat_qvg[0].shape[1]
    n_tables = 5 * RET_HEADS + RET_HEADS
    out_shape = [jax.ShapeDtypeStruct((b, n_lat, HEAD_W), MXU_DTYPE)]
    if with_ctx_out:
        out_shape.insert(0, jax.ShapeDtypeStruct((b, n_ctx, HEAD_W), MXU_DTYPE))
    outs = pl.pallas_call(
        functools.partial(_retention_kernel, with_ctx_out=with_ctx_out),
        grid=(b,),
        in_specs=[_layer(logit_tile, l)] + [_batch_block(a) for a in (*ctx_qvg, *lat_qvg)],
        out_specs=[_batch_block(s) for s in out_shape],
        out_shape=out_shape,
        scratch_shapes=[pltpu.VMEM((n_tables, RET_CHUNK, LANES), F32),
                        pltpu.VMEM(((n_ctx + n_lat) // RET_CHUNK, RET_HEADS, 2 * RET_CHUNK, LANES), F32)],
        compiler_params=_params(1),
        name="retention",
    )(logit_tile, *ctx_qvg, *lat_qvg)
    return (outs[0], outs[1]) if with_ctx_out else (None, outs[0])


def _na_kernel(*refs, with_ctx_out):
    if with_ctx_out:
        q_c, k_c, v_c, q_l, k_l, v_l, bias_ref, bias_shift_ref, bias_pad_ref, o_c, o_l, s_ref, p_ref = refs
    else:
        k_c, v_c, q_l, k_l, v_l, bias_ref, bias_shift_ref, bias_pad_ref, o_l, s_ref, p_ref = refs
    n_ctx = k_c.shape[1]
    n_rows = q_l.shape[1] // GRID_W
    n_blocks = n_rows // NA_BLOCK_ROWS
    qb = NA_BLOCK_ROWS * GRID_W
    un = NA_UNION_ROWS * GRID_W
    win = NA_ROWS * GRID_W
    half_win = NA_ROWS // 2
    cdt = p_ref.dtype

    def stacked(q):
        lo = lax.broadcasted_iota(jnp.int32, q.shape, 1) < HEAD_DIM
        zero = jnp.zeros((), q.dtype)
        return jnp.concatenate([jnp.where(lo, q, zero), jnp.where(lo, zero, q)], axis=0)

    def unstack(o):
        m = o.shape[0] // 2
        lo = lax.broadcasted_iota(jnp.int32, (m, LANES), 1) < HEAD_DIM
        o = o[:, 0:LANES] / o[:, LANES:]
        return jnp.where(lo, o[0:m], o[m:])

    def softmax(parts):
        m = parts[0].max(axis=-1, keepdims=True)
        for s in parts[1:]:
            m = jnp.maximum(m, s.max(axis=-1, keepdims=True))
        return [jnp.exp2(s - m) for s in parts]

    def values(v_ref, rows, p):
        return v_ref[0, rows, 2 * p * LANES:(2 * p + 2) * LANES]

    def block(g, config):
        if config == "first":
            u0, ku = 0, win
        elif config == "last":
            u0, ku = n_rows - NA_ROWS, win
        else:
            u0, ku = g * NA_BLOCK_ROWS - half_win, un
        q_rows = pl.ds(pl.multiple_of(g * qb, qb), qb)
        k_rows = pl.ds(pl.multiple_of(u0 * GRID_W, GRID_W), ku)

        def scores(p):
            sl = slice(p * LANES, (p + 1) * LANES)
            qs = stacked(q_l[0, q_rows, sl])
            s_ref[p % 2, :, 0:ku] = _dot_nt(qs, k_l[0, k_rows, sl])
            s_ref[p % 2, :, ku:ku + n_ctx] = _dot_nt(qs, k_c[0, :, sl])

        scores(0)
        for p in range(NA_HEADS // 2):
            sl = slice(p * LANES, (p + 1) * LANES)
            pbuf, sbuf = p_ref.at[p % 2], s_ref.at[p % 2]
            if p + 1 < NA_HEADS // 2:
                scores(p + 1)
            for half in range(2):
                hh = 2 * p + half
                for a in range(NA_BLOCK_ROWS):
                    rows = slice(half * qb + a * GRID_W, half * qb + (a + 1) * GRID_W)
                    if config == "first":
                        w, off = 0, NA_ROWS - 1 - a
                    elif config == "last":
                        w, off = 0, half_win - 1 - a
                    else:
                        w, off = a * GRID_W, half_win - 1
                    if w % LANES:
                        lo, width, bias = w - GRID_W, win + 2 * GRID_W, bias_pad_ref[hh]
                    elif off % 2:
                        lo, width, bias = w, win, bias_shift_ref[hh, :, (off - 1) * GRID_W:(off - 1) * GRID_W + win]
                    else:
                        lo, width, bias = w, win, bias_ref[hh, :, off * GRID_W:off * GRID_W + win]
                    p_loc, p_cx = softmax([sbuf[rows, lo:lo + width] + bias, sbuf[rows, ku:ku + n_ctx]])
                    pieces = [p_loc, jnp.zeros((GRID_W, ku - lo - width), F32), p_cx]
                    if lo:
                        pieces = [jnp.zeros((GRID_W, lo), F32)] + pieces
                    pbuf[rows, 0:ku + n_ctx] = jnp.concatenate([x for x in pieces if x.shape[1]], axis=1).astype(cdt)
            o = (_dot(pbuf[:, 0:ku], values(v_l, k_rows, p))
                 + _dot(pbuf[:, ku:ku + n_ctx], values(v_c, slice(None), p)))
            o_l[0, q_rows, sl] = unstack(o).astype(o_l.dtype)

    block(0, "first")

    def middle(g, carry):
        block(g, "middle")
        return carry

    lax.fori_loop(1, n_blocks - 1, middle, 0, unroll=True)
    block(n_blocks - 1, "last")

    if with_ctx_out:
        for p in range(NA_HEADS // 2):
            sl = slice(p * LANES, (p + 1) * LANES)
            (pn,) = softmax([_dot_nt(stacked(q_c[0, :, sl]), k_c[0, :, sl])])
            o = _dot(pn.astype(cdt), values(v_c, slice(None), p))
            o_c[0, :, sl] = unstack(o).astype(o_c.dtype)


def _neighbourhood(l, ctx_qkv, lat_qkv, bias_tables, with_ctx_out):
    b, n_ctx, _ = ctx_qkv[0].shape
    n_lat = lat_qkv[0].shape[1]
    p_cols = NA_UNION_ROWS * GRID_W + n_ctx
    stacked_rows = 2 * NA_BLOCK_ROWS * GRID_W
    out_shape = [jax.ShapeDtypeStruct((b, n_lat, HEAD_W), MXU_DTYPE)]
    if with_ctx_out:
        out_shape.insert(0, jax.ShapeDtypeStruct((b, n_ctx, HEAD_W), MXU_DTYPE))
    outs = pl.pallas_call(
        functools.partial(_na_kernel, with_ctx_out=with_ctx_out),
        grid=(b,),
        in_specs=[_batch_block(a) for a in (*ctx_qkv, *lat_qkv)] + [_layer(tbl, l) for tbl in bias_tables],
        out_specs=[_batch_block(s) for s in out_shape],
        out_shape=out_shape,
        scratch_shapes=[pltpu.VMEM((2, stacked_rows, p_cols), F32),
                        pltpu.VMEM((2, stacked_rows, p_cols), MXU_DTYPE)],
        compiler_params=_params(1),
        name="neighbourhood_attention",
    )(*ctx_qkv, *lat_qkv, *bias_tables)
    return (outs[0], outs[1]) if with_ctx_out else (None, outs[0])


def _na_bias_table(rel_bias):
    col = np.arange(GRID_W)
    col_start = np.clip(col - NA_COLS // 2, 0, GRID_W - NA_COLS)
    in_window = (col[None, :] >= col_start[:, None]) & (col[None, :] < col_start[:, None] + NA_COLS)
    dcol = np.clip(col[None, :] - col[:, None], 1 - NA_COLS, NA_COLS - 1) + NA_COLS - 1
    onehot = (dcol[None] == np.arange(2 * NA_COLS - 1)[:, None, None]) & in_window[None]
    rest = rel_bias.astype(F32) * LOG2_E
    cb = None
    for _ in range(3):
        piece = rest.astype(MXU_DTYPE)
        rest = rest - piece.astype(F32)
        part = jnp.einsum("lhdc,cqk->lhqdk", piece, jnp.asarray(onehot, MXU_DTYPE), preferred_element_type=F32)
        cb = part if cb is None else cb + part
    cb = jnp.where(jnp.asarray(in_window)[:, None, :], cb, NEG_INF)
    table = cb.reshape(cb.shape[:3] + ((2 * NA_ROWS - 1) * GRID_W,))
    mid = (NA_ROWS // 2 - 1) * GRID_W
    side = jnp.full(cb.shape[:3] + (GRID_W,), NEG_INF, F32)
    padded = jnp.concatenate([side, table[..., mid:mid + NA_ROWS * GRID_W], side], axis=-1)
    return table, table[..., GRID_W:], padded


def _gqa_tile(q_ref, o_ref, q_rows, kv_refs, chunks, s_ref, p_ref):
    tq = GQA_Q_TILE
    n_kv = GQA_KV_HEADS
    rb = GQA_ROW_BLOCK
    n_blk = GQA_Q_HEADS // n_kv * tq // rb
    per_head = tq // rb
    lo = lax.broadcasted_iota(jnp.int32, (tq, LANES), 1) < HEAD_DIM
    zero = jnp.zeros((), q_ref.dtype)

    def stacked_queries(g):
        parts = []
        for j in range(2 * g, 2 * g + 2):
            q = q_ref[0, q_rows, j * LANES:(j + 1) * LANES]
            parts += [jnp.where(lo, q, zero), jnp.where(lo, zero, q)]
        return jnp.concatenate(parts, axis=0)

    items = [(g, ci) for g in range(n_kv) for ci in range(len(chunks))]
    widths = [sum(c1 - c0 for _, c0, c1 in pieces) for pieces in chunks]
    qs = [stacked_queries(g) for g in range(n_kv)]

    def rows_of(ci, lane_tile):
        pieces = [kv_refs[src][0, c0:c1, lane_tile * LANES:(lane_tile + 1) * LANES] for src, c0, c1 in chunks[ci]]
        return pieces[0] if len(pieces) == 1 else jnp.concatenate(pieces, axis=0)

    def scores(idx):
        g, ci = items[idx]
        s_ref[idx % 2, :, 0:widths[ci]] = _dot_nt(qs[g], rows_of(ci, g))

    scores(0)
    m = acc = None
    for idx, (g, ci) in enumerate(items):
        w, slot = widths[ci], idx % 2
        if idx + 1 < len(items):
            scores(idx + 1)
        if ci == 0:
            m, acc = [None] * n_blk, [None] * n_blk
        alpha = [None] * n_blk
        for i in range(n_blk):
            rows = slice(i * rb, (i + 1) * rb)
            s = s_ref[slot, rows, 0:w]
            s_max = s.max(axis=-1, keepdims=True)
            m_new = s_max if ci == 0 else jnp.maximum(m[i], s_max)
            if ci:
                alpha[i] = jnp.exp2(m[i] - m_new)
            m[i] = m_new
            p_ref[slot, rows, 0:w] = jnp.exp2(s - m_new).astype(p_ref.dtype)
        pv = _dot(p_ref[slot, :, 0:w], rows_of(ci, n_kv + g))
        for i in range(n_blk):
            part = pv[i * rb:(i + 1) * rb]
            acc[i] = part if ci == 0 else alpha[i] * acc[i] + part
        if ci + 1 == len(chunks):
            for j in range(2):
                halves = []
                for half in range(2):
                    h = 2 * j + half
                    a = jnp.concatenate(acc[h * per_head:(h + 1) * per_head], axis=0)
                    a_sw = pltpu.roll(a, HEAD_DIM, 1)
                    halves.append(a / a_sw if half == 0 else a_sw / a)
                pair = 2 * g + j
                o_ref[0, q_rows, pair * LANES:(pair + 1) * LANES] = jnp.where(lo, halves[0], halves[1]).astype(o_ref.dtype)


def _gqa_kernel(*refs, lat_chunks, ctx_chunks):
    if ctx_chunks is None:
        kv_c, q_l, kv_l, o_l, s_ref, p_ref = refs
    else:
        q_c, kv_c, q_l, kv_l, o_c, o_l, s_ref, p_ref = refs
    tq = GQA_Q_TILE

    def tile(j, carry):
        _gqa_tile(q_l, o_l, pl.ds(pl.multiple_of(j * tq, tq), tq), (kv_c, kv_l), lat_chunks, s_ref, p_ref)
        return carry

    lax.fori_loop(0, q_l.shape[1] // tq, tile, 0)
    if ctx_chunks is not None:
        _gqa_tile(q_c, o_c, slice(None), (kv_c,), ctx_chunks, s_ref, p_ref)


def _key_chunks(sizes):
    starts = [0]
    for n in sizes:
        starts.append(starts[-1] + n)
    chunks = []
    for c0 in range(0, starts[-1], GQA_KEY_CHUNK):
        c1 = min(c0 + GQA_KEY_CHUNK, starts[-1])
        pieces = [(src, max(c0, starts[src]) - starts[src], min(c1, starts[src + 1]) - starts[src])
                  for src in range(len(sizes)) if max(c0, starts[src]) < min(c1, starts[src + 1])]
        chunks.append(tuple(pieces))
    return tuple(chunks)


def _gqa(q_lat, kv_ctx, kv_lat, q_ctx=None):
    b, n_lat, _ = q_lat.shape
    n_ctx = kv_ctx.shape[1]
    assert n_lat % GQA_Q_TILE == 0 and n_ctx == GQA_Q_TILE
    with_ctx = q_ctx is not None
    stacked_rows = GQA_Q_HEADS // GQA_KV_HEADS * GQA_Q_TILE
    ins = ([q_ctx] if with_ctx else []) + [kv_ctx, q_lat, kv_lat]
    out_shape = [jax.ShapeDtypeStruct(q_lat.shape, MXU_DTYPE)]
    if with_ctx:
        out_shape.insert(0, jax.ShapeDtypeStruct(q_ctx.shape, MXU_DTYPE))
    outs = pl.pallas_call(
        functools.partial(_gqa_kernel, lat_chunks=_key_chunks((n_ctx, n_lat)),
                          ctx_chunks=_key_chunks((n_ctx,)) if with_ctx else None),
        grid=(b,),
        in_specs=[_batch_block(a) for a in ins],
        out_specs=[_batch_block(s) for s in out_shape],
        out_shape=out_shape,
        scratch_shapes=[pltpu.VMEM((2, stacked_rows, GQA_KEY_CHUNK), F32),
                        pltpu.VMEM((2, stacked_rows, GQA_KEY_CHUNK), MXU_DTYPE)],
        compiler_params=_params(1),
        name="gqa_attention",
    )(*ins)
    return (outs[0], outs[1]) if with_ctx else (None, outs[0])


def _post_kernel(yr_ref, yn_ref, yg_ref, gate_ref, s_ref, gt1_ref, sh2_ref, sc2_ref, gt2_ref,
                 gpost_ref, gpre_ref, gpost2_ref, wr_ref, wn_ref, wg_ref, wout_ref, w1_ref, w2_ref, *rest):
    n_cast = (len(rest) - 1) // 2
    o_ref = rest[n_cast]
    _cast_slabs(rest[:n_cast], rest[n_cast + 1:])
    tm, d = o_ref.shape[1:]

    def mixed(rows):
        y = None
        for i, (br, w_ref) in enumerate(((yr_ref, wr_ref), (yn_ref, wn_ref), (yg_ref, wg_ref))):
            z = _dot(br[0, rows, :], w_ref[...]) * gate_ref[0, rows, i * d:(i + 1) * d].astype(F32)
            y = z if y is None else y + z
        return _dot(y.astype(MXU_DTYPE), wout_ref[...])

    def mlp(rows, y):
        x = s_ref[0, rows, :] + gt1_ref[0] * (_rms(y) * gpost_ref[...])
        h = _rms(x) * gpre_ref[...]
        h = (h * (1.0 + sc2_ref[0]) + sh2_ref[0]).astype(MXU_DTYPE)
        acc = None
        for j in range(D_FF // d):
            u = jnp.maximum(_dot(h, w1_ref[:, j * d:(j + 1) * d]), 0.0)
            part = _dot((u * u).astype(MXU_DTYPE), w2_ref[j * d:(j + 1) * d, :])
            acc = part if acc is None else acc + part
        o_ref[0, rows, :] = x + gt2_ref[0] * (_rms(acc) * gpost2_ref[...])

    groups = [slice(r0, r0 + POST_ROW_GROUP) for r0 in range(0, tm, POST_ROW_GROUP)]
    ys = [mixed(rows) for rows in groups]
    for rows, y in zip(groups, ys):
        mlp(rows, y)


def _post(l, y_ret, y_na, y_gqa, gates, stream, is_ctx, mod3, mod_rows, gains, weights, cast=()):
    tg = _TokenGrid(stream, is_ctx, l, mod_rows)
    d = tg.d
    cast_specs = [tg.cast_specs(a, l + 1) for a in cast]
    outs = pl.pallas_call(
        _post_kernel,
        grid=tg.grid,
        in_specs=[tg.tok(HEAD_W), tg.tok(HEAD_W), tg.tok(HEAD_W), tg.tok(3 * d), tg.tok(d)]
        + [tg.mod(chunk) for chunk in (2, 3, 4, 5)]
        + [_layer(a, l) for a in gains] + [_resident(w.shape) for w in weights] + [s[0] for s in cast_specs],
        out_specs=[tg.tok(d)] + [s[1] for s in cast_specs],
        out_shape=[jax.ShapeDtypeStruct(tg.shape + (d,), F32)] + [s[2] for s in cast_specs],
        compiler_params=_params(2),
        name="merge_mlp_ctx" if is_ctx else "merge_mlp",
    )(tg.view(y_ret), tg.view(y_na), tg.view(y_gqa), tg.view(gates), tg.view(stream),
      mod3, mod3, mod3, mod3, *gains, *weights, *cast)
    return (tg.unview(outs[0]),) + tuple(outs[1:])


def _rope_tables(n_latent):
    t = np.arange(n_latent)
    pos = np.stack([t // GRID_W, t % GRID_W], axis=-1).astype(np.float64)
    n_freq = HEAD_DIM // 4
    inv_freq = ROPE_BASE ** (-np.arange(n_freq, dtype=np.float64) / n_freq)
    lane = np.arange(LANES) % HEAD_DIM
    axis, second, freq = lane // 32, (lane % 32) // 16, lane % 16
    ang = pos[:, axis] * inv_freq[freq][None, :]
    return jnp.asarray(np.cos(ang), F32), jnp.asarray(np.sin(ang) * np.where(second == 1, 1.0, -1.0), F32)


def kernel(x, c, ctx, c_ctx, w_mod, b_mod, g_pre_mix, g_post_mix, g_pre_mlp, g_post_mlp, w_in, ret_decay_logit, na_rel_bias, gqa_q_norm, gqa_k_norm, w_br_ret, w_br_na, w_br_gqa, w_out, w_mlp_in, w_mlp_out):
    b, n, d = x.shape
    depth = w_mod.shape[0]
    cdt = MXU_DTYPE

    rope_tables = _rope_tables(n)
    lane_head = np.arange(2 * LANES) // HEAD_DIM
    gsum = jnp.asarray(lane_head[:, None] == lane_head[None, :], cdt)

    n_rows = -(-(b + 1) // 8) * 8
    cs = jnp.concatenate([c, jnp.zeros((n_rows - b - 1, d), F32), c_ctx[None, :]], axis=0)
    mod, w_in_l = _modulation(cs, w_mod, b_mod, w_in)
    mod3 = mod.reshape(depth * n_rows, 1, 6 * d)
    post_weight_stacks = (w_br_ret, w_br_na, w_br_gqa, w_out, w_mlp_in, w_mlp_out)
    post_gains = tuple(g[:, None, :] for g in (g_post_mix, g_pre_mlp, g_post_mlp))
    pre_gain = g_pre_mix[:, None, :]
    q_gain = jnp.tile(gqa_q_norm, (1, 2))[:, None, :]
    k_gain = jnp.tile(gqa_k_norm, (1, 2))[:, None, :]
    logit_tile = jnp.broadcast_to(ret_decay_logit.astype(F32)[..., None, None], (depth, 2, RET_HEADS, 8, LANES))
    bias_tables = _na_bias_table(na_rel_bias)

    ctx_s, lat_s = ctx, x
    for l in range(depth):
        last = l == depth - 1
        pc = _in_projection(l, ctx_s, True, mod3, n_rows, pre_gain, w_in_l, rope_tables, q_gain, k_gain, gsum,
                            kv_only=last)
        pt = _in_projection(l, lat_s, False, mod3, n_rows, pre_gain, w_in_l, rope_tables, q_gain, k_gain, gsum,
                            cast=post_weight_stacks)
        post_weights = pt[9:]
        ret_c, ret_l = _retention(l, logit_tile, pc[0:2] if last else pc[0:3], pt[0:3], not last)
        na_c, na_l = _neighbourhood(l, pc[4:6] if last else pc[3:6], pt[3:6], bias_tables, not last)
        gqa_c, gqa_l = _gqa(pt[6], pc[7], pt[7], None if last else pc[6])
        post_l = _post(l, ret_l, na_l, gqa_l, pt[8], lat_s, False, mod3, n_rows, post_gains, post_weights,
                       cast=() if last else (w_in,))
        lat_s = post_l[0]
        if not last:
            w_in_l = post_l[1]
            ctx_s, = _post(l, ret_c, na_c, gqa_c, pc[8], ctx_s, True, mod3, n_rows, post_gains, post_weights)
    return lat_s
```

```python
import functools

import jax
import jax.numpy as jnp
import numpy as np
from jax import lax
from jax.experimental import pallas as pl
from jax.experimental.pallas import tpu as pltpu

F32 = jnp.float32
MXU_DTYPE = jnp.bfloat16

D_MODEL = 1024
GRID_W = 64
CTX_LEN = 256
HEAD_DIM = 64
RET_HEADS = 4
RET_CHUNK = 128
NA_HEADS = 8
NA_ROWS = 8
NA_COLS = 16
NA_BLOCK_ROWS = 4
NA_UNION_ROWS = NA_BLOCK_ROWS + NA_ROWS
GQA_Q_HEADS = 8
GQA_KV_HEADS = 2
GQA_Q_TILE = 256
GQA_KEY_CHUNK = 768
GQA_ROW_BLOCK = 64
D_FF = 4 * D_MODEL
ROPE_BASE = 10000.0
NORM_EPS = 1e-6
NEG_INF = -1e30
LOG2_E = 1.4426950408889634

LANES = 128
HEAD_W = 512
C_RQ, C_RK, C_RV, C_RG = 0, 256, 512, 1024
C_NQ, C_NK, C_NV = 1536, 2048, 2560
C_GQ, C_GK, C_GV = 3072, 3584, 3712
C_GATE = 3840
IN_W = 6912

TOKEN_TILE = 512
POST_ROW_GROUP = 256
VMEM_LIMIT = 56 * 1024 * 1024


def _rms(x):
    return x * lax.rsqrt(jnp.mean(x * x, axis=-1, keepdims=True) + NORM_EPS)


def _dot(a, b):
    return jnp.dot(a, b, preferred_element_type=F32)


def _dot_nt(a, b):
    return lax.dot_general(a, b, (((1,), (1,)), ((), ())), preferred_element_type=F32)


def _resident(shape):
    return pl.BlockSpec(shape, lambda *_: (0,) * len(shape), pipeline_mode=pl.Buffered(1))


def _layer(arr, l):
    return pl.BlockSpec((None,) + arr.shape[1:], lambda *_: (l,) + (0,) * (arr.ndim - 1),
                        pipeline_mode=pl.Buffered(1))


def _batch_block(arr):
    return pl.BlockSpec((1,) + arr.shape[1:], lambda i, *_: (i, 0, 0))


def _params(n_axes, fuse_inputs=None):
    return pltpu.CompilerParams(dimension_semantics=("arbitrary",) * n_axes,
                                vmem_limit_bytes=VMEM_LIMIT, allow_input_fusion=fuse_inputs)


def _mod_kernel(c_ref, w_ref, b_ref, win_ref, o_ref, wout_ref):
    c = c_ref[...]
    s = (c * jax.nn.sigmoid(c)).astype(MXU_DTYPE)
    o_ref[0] = _dot(s, w_ref[0].astype(MXU_DTYPE)) + b_ref[0]
    wout_ref[...] = win_ref[...].astype(wout_ref.dtype)


def _modulation(cs, w_mod, b_mod, w_in):
    depth, d, n = w_mod.shape
    r = cs.shape[0]
    tn = 1536
    per_layer = n // tn
    rows, cols = w_in.shape[1:]
    slab = rows // (depth * per_layer)
    assert slab * depth * per_layer == rows and slab % 16 == 0
    return pl.pallas_call(
        _mod_kernel,
        grid=(depth, per_layer),
        in_specs=[pl.BlockSpec((r, d), lambda l, j: (0, 0)),
                  pl.BlockSpec((1, d, tn), lambda l, j: (l, 0, j)),
                  pl.BlockSpec((1, 1, tn), lambda l, j: (l, 0, j)),
                  pl.BlockSpec((None, slab, cols), lambda l, j: (0, l * per_layer + j, 0))],
        out_specs=[pl.BlockSpec((1, r, tn), lambda l, j: (l, 0, j)),
                   pl.BlockSpec((slab, cols), lambda l, j: (l * per_layer + j, 0))],
        out_shape=[jax.ShapeDtypeStruct((depth, r, n), F32), jax.ShapeDtypeStruct((rows, cols), MXU_DTYPE)],
        compiler_params=_params(2),
        name="adaln_modulation",
    )(cs, w_mod, b_mod.reshape(depth, 1, n), w_in)


class _TokenGrid:
    def __init__(self, stream, is_ctx, layer, mod_rows):
        b, n, self.d = stream.shape
        self.is_ctx = is_ctx
        self.shape = (1, b * n) if is_ctx else (b, n)
        self.tm = min(TOKEN_TILE, self.shape[1])
        self.grid = (self.shape[0], self.shape[1] // self.tm)
        self.orig = (b, n)
        self.mod_base = layer * mod_rows + (mod_rows - 1 if is_ctx else 0)

    def view(self, arr):
        return arr.reshape(self.shape + arr.shape[2:])

    def unview(self, arr):
        return arr.reshape(self.orig + arr.shape[2:])

    def tok(self, width):
        return pl.BlockSpec((1, self.tm, width), lambda i, j: (i, j, 0))

    def mod(self, chunk):
        base, per_batch = self.mod_base, 0 if self.is_ctx else 1
        return pl.BlockSpec((1, 1, self.d), lambda i, j: (base + per_batch * i, 0, chunk))

    def cast_specs(self, stacked, layer):
        steps = self.grid[0] * self.grid[1]
        rows, cols = stacked.shape[1:]
        slab = rows // steps
        assert slab * steps == rows and slab % 16 == 0, (rows, steps)
        per_row = self.grid[1]
        return (pl.BlockSpec((None, slab, cols), lambda i, j: (layer, i * per_row + j, 0)),
                pl.BlockSpec((slab, cols), lambda i, j: (i * per_row + j, 0)),
                jax.ShapeDtypeStruct((rows, cols), MXU_DTYPE))


def _cast_slabs(src_refs, dst_refs):
    for src, dst in zip(src_refs, dst_refs):
        dst[...] = src[...].astype(dst.dtype)


def _inproj_kernel(*refs, with_rope, n_cast, kv_only):
    x_ref, sh_ref, sc_ref, g_ref, w_ref = refs[:5]
    if with_rope:
        cos_ref, sin_ref = refs[5:7]
        refs = refs[7:]
    else:
        refs = refs[5:]
    qg_ref, kg_ref, gsum_ref = refs[:3]
    cast_src, refs = refs[3:3 + n_cast], refs[3 + n_cast:]
    if kv_only:
        rqk_ref, rv_ref, nk_ref, nv_ref, gkv_ref = refs[:5]
        n_out = 5
    else:
        rqk_ref, rv_ref, rg_ref, nq_ref, nk_ref, nv_ref, gq_ref, gkv_ref, gate_ref = refs[:9]
        n_out = 9
    _cast_slabs(cast_src, refs[n_out:])

    x = x_ref[0]
    h = _rms(x) * g_ref[...]
    h = h * (1.0 + sc_ref[0]) + sh_ref[0]
    hb = h.astype(MXU_DTYPE)
    lane = lax.broadcasted_iota(jnp.int32, (x.shape[0], LANES), 1)
    gsum = gsum_ref[...]

    def proj(col, width):
        return _dot(hb, w_ref[:, col:col + width])

    if with_rope:
        cos, sin = cos_ref[...], sin_ref[...]
        first_half = (lane % 32) < 16

        def rope(v):
            partner = jnp.where(first_half, pltpu.roll(v, LANES - 16, 1), pltpu.roll(v, 16, 1))
            return v * cos + partner * sin
    else:
        def rope(v):
            return v

    def head_rms(v):
        ms = _dot((v * v).astype(MXU_DTYPE), gsum) * (1.0 / HEAD_DIM)
        return v * lax.rsqrt(ms + NORM_EPS)

    odt = rqk_ref.dtype
    ret_scale = HEAD_DIM ** -0.5
    att_scale = HEAD_DIM ** -0.5 * LOG2_E

    rk = proj(C_RK, 256)
    if kv_only:
        rqk_ref[0, :, 0:256] = jnp.zeros((x.shape[0], 256), odt)
    else:
        rq = proj(C_RQ, 256)
    for j in range(2):
        sl = slice(j * LANES, (j + 1) * LANES)
        if not kv_only:
            rqk_ref[0, :, j * LANES:(j + 1) * LANES] = rope(rq[:, sl]).astype(odt)
        rqk_ref[0, :, 256 + j * LANES:256 + (j + 1) * LANES] = (rope(rk[:, sl]) * ret_scale).astype(odt)
    rv_ref[0] = proj(C_RV, HEAD_W).astype(odt)
    if not kv_only:
        rg = proj(C_RG, HEAD_W)
        rg_ref[0] = (rg * jax.nn.sigmoid(rg)).astype(odt)
        nq_ref[0] = (proj(C_NQ, HEAD_W) * att_scale).astype(odt)
    nk_ref[0] = proj(C_NK, HEAD_W).astype(odt)
    nv = proj(C_NV, HEAD_W).astype(odt)
    for j in range(NA_HEADS // 2):
        nv_ref[0, :, 2 * j * LANES:(2 * j + 1) * LANES] = nv[:, j * LANES:(j + 1) * LANES]
        nv_ref[0, :, (2 * j + 1) * LANES:(2 * j + 2) * LANES] = jnp.ones((nv.shape[0], LANES), odt)

    if not kv_only:
        gq = proj(C_GQ, HEAD_W)
        qg = qg_ref[...]
        for j in range(2):
            normed = head_rms(gq[:, 2 * j * LANES:(2 * j + 2) * LANES])
            for i in range(2):
                sl = slice((2 * j + i) * LANES, (2 * j + i + 1) * LANES)
                gq_ref[0, :, sl] = (rope(normed[:, i * LANES:(i + 1) * LANES] * qg) * att_scale).astype(odt)
    gkv = proj(C_GK, 2 * LANES)
    gk = rope(head_rms(gkv)[:, 0:LANES] * kg_ref[...])
    gv = gkv[:, LANES:]
    low_half = lane < HEAD_DIM
    gk_sw, gv_sw = pltpu.roll(gk, HEAD_DIM, 1), pltpu.roll(gv, HEAD_DIM, 1)
    gkv_ref[0, :, 0:LANES] = jnp.where(low_half, gk, gk_sw).astype(odt)
    gkv_ref[0, :, LANES:2 * LANES] = jnp.where(low_half, gk_sw, gk).astype(odt)
    gkv_ref[0, :, 2 * LANES:3 * LANES] = jnp.where(low_half, gv, 1.0).astype(odt)
    gkv_ref[0, :, 3 * LANES:4 * LANES] = jnp.where(low_half, gv_sw, 1.0).astype(odt)

    if not kv_only:
        for j in range(6):
            sl = slice(j * HEAD_W, (j + 1) * HEAD_W)
            gate_ref[0, :, sl] = jax.nn.sigmoid(proj(C_GATE + j * HEAD_W, HEAD_W)).astype(odt)


def _in_projection(l, stream, is_ctx, mod3, mod_rows, layer_gain, w_in, rope_tables, q_gain, k_gain, gsum, cast=(),
                   kv_only=False):
    tg = _TokenGrid(stream, is_ctx, l, mod_rows)
    d = tg.d
    all_widths = [HEAD_W] * 5 + [2 * HEAD_W] + [HEAD_W] * 2 + [3 * d]
    made = [0, 1, 4, 5, 7] if kv_only else list(range(len(all_widths)))
    out_widths = [all_widths[k] for k in made]
    rope_specs = [] if is_ctx else [pl.BlockSpec((tg.tm, LANES), lambda i, j: (j, 0))] * 2
    rope_args = () if is_ctx else rope_tables
    cast_specs = [tg.cast_specs(a, l) for a in cast]
    outs = pl.pallas_call(
        functools.partial(_inproj_kernel, with_rope=not is_ctx, n_cast=len(cast), kv_only=kv_only),
        grid=tg.grid,
        in_specs=[tg.tok(d), tg.mod(0), tg.mod(1), _layer(layer_gain, l), _resident(w_in.shape)] + rope_specs
        + [_layer(q_gain, l), _layer(k_gain, l), _resident(gsum.shape)] + [s[0] for s in cast_specs],
        out_specs=[tg.tok(w) for w in out_widths] + [s[1] for s in cast_specs],
        out_shape=[jax.ShapeDtypeStruct(tg.shape + (w,), MXU_DTYPE) for w in out_widths] + [s[2] for s in cast_specs],
        compiler_params=_params(2),
        name="in_projection_ctx" if is_ctx else "in_projection",
    )(tg.view(stream), mod3, mod3, layer_gain, w_in, *rope_args, q_gain, k_gain, gsum, *cast)
    projections = [None] * len(all_widths)
    for k, o in zip(made, outs):
        projections[k] = tg.unview(o)
    return tuple(projections) + tuple(outs[len(out_widths):])


def _retention_kernel(*refs, with_ctx_out):
    if with_ctx_out:
        logit_ref, qk_c, v_c, g_c, qk_l, v_l, g_l, o_c, o_l, dec_ref, st_ref = refs
    else:
        logit_ref, qk_c, v_c, qk_l, v_l, g_l, o_l, dec_ref, st_ref = refs
        g_c = o_c = None
    c_len = RET_CHUNK
    n_ctx_chunks = qk_c.shape[1] // c_len
    n_lat_chunks = qk_l.shape[1] // c_len
    n_chunks = n_ctx_chunks + n_lat_chunks
    heads = RET_HEADS
    pairs = heads // 2
    t_mask, t_dkf, t_dkb, t_dsf, t_dsb = (k * heads for k in range(5))
    t_dqf, t_dqb = 5 * heads, 5 * heads + pairs
    cdt = qk_l.dtype

    row = lax.broadcasted_iota(jnp.int32, (c_len, LANES), 0)
    lane = lax.broadcasted_iota(jnp.int32, (c_len, LANES), 1)
    lane_lo = lane < HEAD_DIM
    row_lo = row < HEAD_DIM

    @pl.when(pl.program_id(0) == 0)
    def _build_tables():
        rowf = row.astype(F32)
        lanef = lane.astype(F32)
        rel = rowf - lanef
        lg = [[jnp.broadcast_to(jax.nn.log_sigmoid(logit_ref[dd, hh])[0:1, :], (c_len, LANES))
               for hh in range(heads)] for dd in range(2)]
        for hh in range(heads):
            lf, lb = lg[0][hh], lg[1][hh]
            fwd = jnp.where(rel >= 0, jnp.exp(lf * jnp.maximum(rel, 0.0)), 0.0)
            bwd = jnp.where(rel < 0, jnp.exp(lb * jnp.maximum(-rel, 0.0)), 0.0)
            dec_ref[t_mask + hh] = fwd + bwd
            dec_ref[t_dkf + hh] = jnp.exp(lf * (c_len - 1.0 - lanef))
            dec_ref[t_dkb + hh] = jnp.exp(lb * lanef)
            dec_ref[t_dsf + hh] = jnp.exp(lf * float(c_len))
            dec_ref[t_dsb + hh] = jnp.exp(lb * float(c_len))
        for p in range(pairs):
            lf = jnp.where(lane_lo, lg[0][2 * p], lg[0][2 * p + 1])
            lb = jnp.where(lane_lo, lg[1][2 * p], lg[1][2 * p + 1])
            dec_ref[t_dqf + p] = jnp.exp(lf * (rowf + 1.0))
            dec_ref[t_dqb + p] = jnp.exp(lb * (float(c_len) - rowf))

    def summaries(qk_ref, v_ref, rows, c):
        for p in range(pairs):
            kt = qk_ref[0, rows, 256 + p * LANES:256 + (p + 1) * LANES].astype(F32).T
            for half in range(2):
                hh = 2 * p + half
                ktm = jnp.where(row_lo if half == 0 else jnp.logical_not(row_lo), kt, 0.0)
                lhs = jnp.concatenate([ktm * dec_ref[t_dkf + hh], ktm * dec_ref[t_dkb + hh]], axis=0)
                st_ref[c, hh] = _dot(lhs.astype(cdt), v_ref[0, rows, hh * LANES:(hh + 1) * LANES])

    def outputs(qk_ref, v_ref, g_ref, o_ref, rows, c):
        zero = jnp.zeros((), cdt)
        for p in range(pairs):
            q = qk_ref[0, rows, p * LANES:(p + 1) * LANES]
            k = qk_ref[0, rows, 256 + p * LANES:256 + (p + 1) * LANES]
            qs = jnp.concatenate([jnp.where(lane_lo, q, zero), jnp.where(lane_lo, zero, q)], axis=0)
            a = _dot_nt(qs, k)
            q32 = q.astype(F32)
            qf = (q32 * dec_ref[t_dqf + p]).astype(cdt)
            qb = (q32 * dec_ref[t_dqb + p]).astype(cdt)
            for half in range(2):
                hh = 2 * p + half
                sl = slice(hh * LANES, (hh + 1) * LANES)
                pm = (a[half * c_len:(half + 1) * c_len] * dec_ref[t_mask + hh]).astype(cdt)
                lhs = jnp.concatenate([pm, qf, qb], axis=1)
                rhs = jnp.concatenate([v_ref[0, rows, sl], st_ref[c, hh].astype(cdt)], axis=0)
                y = _rms(_dot(lhs, rhs)) * g_ref[0, rows, sl].astype(F32)
                o_ref[0, rows, sl] = y.astype(o_ref.dtype)

    def lat_rows(i):
        return pl.ds(pl.multiple_of(i * c_len, c_len), c_len)

    for ci in range(n_ctx_chunks):
        summaries(qk_c, v_c, slice(ci * c_len, (ci + 1) * c_len), ci)

    def lat_summaries(i, carry):
        summaries(qk_l, v_l, lat_rows(i), n_ctx_chunks + i)
        return carry

    lax.fori_loop(0, n_lat_chunks, lat_summaries, 0, unroll=True)

    fwd_order = list(range(n_chunks))
    bwd_order = list(range(n_ctx_chunks - 1, -1, -1)) + list(range(n_chunks - 1, n_ctx_chunks - 1, -1))
    for hh in range(heads):
        for order, lo, t_ds in ((fwd_order, 0, t_dsf), (bwd_order, c_len, t_dsb)):
            decay = dec_ref[t_ds + hh]
            state = jnp.zeros((c_len, LANES), F32)
            for c in order:
                z = st_ref[c, hh, lo:lo + c_len, :]
                st_ref[c, hh, lo:lo + c_len, :] = state
                state = state * decay + z

    if with_ctx_out:
        for ci in range(n_ctx_chunks):
            outputs(qk_c, v_c, g_c, o_c, slice(ci * c_len, (ci + 1) * c_len), ci)

    def lat_outputs(i, carry):
        outputs(qk_l, v_l, g_l, o_l, lat_rows(i), n_ctx_chunks + i)
        return carry

    lax.fori_loop(0, n_lat_chunks, lat_outputs, 0, unroll=True)


def _retention(l, logit_tile, ctx_qvg, lat_qvg, with_ctx_out):
    b, n_ctx, _ = ctx_qvg[0].shape
    n_lat = lat_qvg[0].shape[1]
    n_tables = 5 * RET_HEADS + RET_HEADS
    out_shape = [jax.ShapeDtypeStruct((b, n_lat, HEAD_W), MXU_DTYPE)]
    if with_ctx_out:
        out_shape.insert(0, jax.ShapeDtypeStruct((b, n_ctx, HEAD_W), MXU_DTYPE))
    outs = pl.pallas_call(
        functools.partial(_retention_kernel, with_ctx_out=with_ctx_out),
        grid=(b,),
        in_specs=[_layer(logit_tile, l)] + [_batch_block(a) for a in (*ctx_qvg, *lat_qvg)],
        out_specs=[_batch_block(s) for s in out_shape],
        out_shape=out_shape,
        scratch_shapes=[pltpu.VMEM((n_tables, RET_CHUNK, LANES), F32),
                        pltpu.VMEM(((n_ctx + n_lat) // RET_CHUNK, RET_HEADS, 2 * RET_CHUNK, LANES), F32)],
        compiler_params=_params(1),
        name="retention",
    )(logit_tile, *ctx_qvg, *lat_qvg)
    return (outs[0], outs[1]) if with_ctx_out else (None, outs[0])


def _na_kernel(*refs, with_ctx_out):
    if with_ctx_out:
        q_c, k_c, v_c, q_l, k_l, v_l, bias_ref, bias_shift_ref, bias_pad_ref, o_c, o_l, s_ref, p_ref = refs
    else:
        k_c, v_c, q_l, k_l, v_l, bias_ref, bias_shift_ref, bias_pad_ref, o_l, s_ref, p_ref = refs
    n_ctx = k_c.shape[1]
    n_rows = q_l.shape[1] // GRID_W
    n_blocks = n_rows // NA_BLOCK_ROWS
    qb = NA_BLOCK_ROWS * GRID_W
    un = NA_UNION_ROWS * GRID_W
    win = NA_ROWS * GRID_W
    half_win = NA_ROWS // 2
    cdt = p_ref.dtype

    def stacked(q):
        lo = lax.broadcasted_iota(jnp.int32, q.shape, 1) < HEAD_DIM
        zero = jnp.zeros((), q.dtype)
        return jnp.concatenate([jnp.where(lo, q, zero), jnp.where(lo, zero, q)], axis=0)

    def unstack(o):
        m = o.shape[0] // 2
        lo = lax.broadcasted_iota(jnp.int32, (m, LANES), 1) < HEAD_DIM
        o = o[:, 0:LANES] / o[:, LANES:]
        return jnp.where(lo, o[0:m], o[m:])

    def softmax(parts):
        m = parts[0].max(axis=-1, keepdims=True)
        for s in parts[1:]:
            m = jnp.maximum(m, s.max(axis=-1, keepdims=True))
        return [jnp.exp2(s - m) for s in parts]

    def values(v_ref, rows, p):
        return v_ref[0, rows, 2 * p * LANES:(2 * p + 2) * LANES]

    def block(g, config):
        if config == "first":
            u0, ku = 0, win
        elif config == "last":
            u0, ku = n_rows - NA_ROWS, win
        else:
            u0, ku = g * NA_BLOCK_ROWS - half_win, un
        q_rows = pl.ds(pl.multiple_of(g * qb, qb), qb)
        k_rows = pl.ds(pl.multiple_of(u0 * GRID_W, GRID_W), ku)

        def scores(p):
            sl = slice(p * LANES, (p + 1) * LANES)
            qs = stacked(q_l[0, q_rows, sl])
            s_ref[p % 2, :, 0:ku] = _dot_nt(qs, k_l[0, k_rows, sl])
            s_ref[p % 2, :, ku:ku + n_ctx] = _dot_nt(qs, k_c[0, :, sl])

        scores(0)
        for p in range(NA_HEADS // 2):
            sl = slice(p * LANES, (p + 1) * LANES)
            pbuf, sbuf = p_ref.at[p % 2], s_ref.at[p % 2]
            if p + 1 < NA_HEADS // 2:
                scores(p + 1)
            for half in range(2):
                hh = 2 * p + half
                for a in range(NA_BLOCK_ROWS):
                    rows = slice(half * qb + a * GRID_W, half * qb + (a + 1) * GRID_W)
                    if config == "first":
                        w, off = 0, NA_ROWS - 1 - a
                    elif config == "last":
                        w, off = 0, half_win - 1 - a
                    else:
                        w, off = a * GRID_W, half_win - 1
                    if w % LANES:
                        lo, width, bias = w - GRID_W, win + 2 * GRID_W, bias_pad_ref[hh]
                    elif off % 2:
                        lo, width, bias = w, win, bias_shift_ref[hh, :, (off - 1) * GRID_W:(off - 1) * GRID_W + win]
                    else:
                        lo, width, bias = w, win, bias_ref[hh, :, off * GRID_W:off * GRID_W + win]
                    p_loc, p_cx = softmax([sbuf[rows, lo:lo + width] + bias, sbuf[rows, ku:ku + n_ctx]])
                    pieces = [p_loc, jnp.zeros((GRID_W, ku - lo - width), F32), p_cx]
                    if lo:
                        pieces = [jnp.zeros((GRID_W, lo), F32)] + pieces
                    pbuf[rows, 0:ku + n_ctx] = jnp.concatenate([x for x in pieces if x.shape[1]], axis=1).astype(cdt)
            o = (_dot(pbuf[:, 0:ku], values(v_l, k_rows, p))
                 + _dot(pbuf[:, ku:ku + n_ctx], values(v_c, slice(None), p)))
            o_l[0, q_rows, sl] = unstack(o).astype(o_l.dtype)

    block(0, "first")

    def middle(g, carry):
        block(g, "middle")
        return carry

    lax.fori_loop(1, n_blocks - 1, middle, 0, unroll=True)
    block(n_blocks - 1, "last")

    if with_ctx_out:
        for p in range(NA_HEADS // 2):
            sl = slice(p * LANES, (p + 1) * LANES)
            (pn,) = softmax([_dot_nt(stacked(q_c[0, :, sl]), k_c[0, :, sl])])
            o = _dot(pn.astype(cdt), values(v_c, slice(None), p))
            o_c[0, :, sl] = unstack(o).astype(o_c.dtype)


def _neighbourhood(l, ctx_qkv, lat_qkv, bias_tables, with_ctx_out):
    b, n_ctx, _ = ctx_qkv[0].shape
    n_lat = lat_qkv[0].shape[1]
    p_cols = NA_UNION_ROWS * GRID_W + n_ctx
    stacked_rows = 2 * NA_BLOCK_ROWS * GRID_W
    out_shape = [jax.ShapeDtypeStruct((b, n_lat, HEAD_W), MXU_DTYPE)]
    if with_ctx_out:
        out_shape.insert(0, jax.ShapeDtypeStruct((b, n_ctx, HEAD_W), MXU_DTYPE))
    outs = pl.pallas_call(
        functools.partial(_na_kernel, with_ctx_out=with_ctx_out),
        grid=(b,),
        in_specs=[_batch_block(a) for a in (*ctx_qkv, *lat_qkv)] + [_layer(tbl, l) for tbl in bias_tables],
        out_specs=[_batch_block(s) for s in out_shape],
        out_shape=out_shape,
        scratch_shapes=[pltpu.VMEM((2, stacked_rows, p_cols), F32),
                        pltpu.VMEM((2, stacked_rows, p_cols), MXU_DTYPE)],
        compiler_params=_params(1, [False] * (len(ctx_qkv) + len(lat_qkv)) + [True] * len(bias_tables)),
        name="neighbourhood_attention",
    )(*ctx_qkv, *lat_qkv, *bias_tables)
    return (outs[0], outs[1]) if with_ctx_out else (None, outs[0])


def _na_bias_table(rel_bias):
    col = np.arange(GRID_W)
    col_start = np.clip(col - NA_COLS // 2, 0, GRID_W - NA_COLS)
    in_window = (col[None, :] >= col_start[:, None]) & (col[None, :] < col_start[:, None] + NA_COLS)
    dcol = np.clip(col[None, :] - col[:, None], 1 - NA_COLS, NA_COLS - 1) + NA_COLS - 1
    onehot = (dcol[None] == np.arange(2 * NA_COLS - 1)[:, None, None]) & in_window[None]
    rest = rel_bias.astype(F32) * LOG2_E
    cb = None
    for _ in range(3):
        piece = rest.astype(MXU_DTYPE)
        rest = rest - piece.astype(F32)
        part = jnp.einsum("lhdc,cqk->lhqdk", piece, jnp.asarray(onehot, MXU_DTYPE), preferred_element_type=F32)
        cb = part if cb is None else cb + part
    cb = jnp.where(jnp.asarray(in_window)[:, None, :], cb, NEG_INF)
    table = cb.reshape(cb.shape[:3] + ((2 * NA_ROWS - 1) * GRID_W,))
    mid = (NA_ROWS // 2 - 1) * GRID_W
    side = jnp.full(cb.shape[:3] + (GRID_W,), NEG_INF, F32)
    padded = jnp.concatenate([side, table[..., mid:mid + NA_ROWS * GRID_W], side], axis=-1)
    return table, table[..., GRID_W:], padded


def _gqa_tile(q_ref, o_ref, q_rows, kv_refs, chunks, s_ref, p_ref):
    tq = GQA_Q_TILE
    n_kv = GQA_KV_HEADS
    rb = GQA_ROW_BLOCK
    n_blk = GQA_Q_HEADS // n_kv * tq // rb
    per_head = tq // rb
    lo = lax.broadcasted_iota(jnp.int32, (tq, LANES), 1) < HEAD_DIM
    zero = jnp.zeros((), q_ref.dtype)

    def stacked_queries(g):
        parts = []
        for j in range(2 * g, 2 * g + 2):
            q = q_ref[0, q_rows, j * LANES:(j + 1) * LANES]
            parts += [jnp.where(lo, q, zero), jnp.where(lo, zero, q)]
        return jnp.concatenate(parts, axis=0)

    items = [(g, ci) for g in range(n_kv) for ci in range(len(chunks))]
    widths = [sum(c1 - c0 for _, c0, c1 in pieces) for pieces in chunks]
    qs = [stacked_queries(g) for g in range(n_kv)]

    def rows_of(ci, lane_tile):
        pieces = [kv_refs[src][0, c0:c1, lane_tile * LANES:(lane_tile + 1) * LANES] for src, c0, c1 in chunks[ci]]
        return pieces[0] if len(pieces) == 1 else jnp.concatenate(pieces, axis=0)

    def scores(idx):
        g, ci = items[idx]
        s_ref[idx % 2, :, 0:widths[ci]] = _dot_nt(qs[g], rows_of(ci, g))

    scores(0)
    m = acc = None
    for idx, (g, ci) in enumerate(items):
        w, slot = widths[ci], idx % 2
        if idx + 1 < len(items):
            scores(idx + 1)
        if ci == 0:
            m, acc = [None] * n_blk, [None] * n_blk
        alpha = [None] * n_blk
        for i in range(n_blk):
            rows = slice(i * rb, (i + 1) * rb)
            s = s_ref[slot, rows, 0:w]
            s_max = s.max(axis=-1, keepdims=True)
            m_new = s_max if ci == 0 else jnp.maximum(m[i], s_max)
            if ci:
                alpha[i] = jnp.exp2(m[i] - m_new)
            m[i] = m_new
            p_ref[slot, rows, 0:w] = jnp.exp2(s - m_new).astype(p_ref.dtype)
        pv = _dot(p_ref[slot, :, 0:w], rows_of(ci, n_kv + g))
        for i in range(n_blk):
            part = pv[i * rb:(i + 1) * rb]
            acc[i] = part if ci == 0 else alpha[i] * acc[i] + part
        if ci + 1 == len(chunks):
            for j in range(2):
                halves = []
                for half in range(2):
                    h = 2 * j + half
                    a = jnp.concatenate(acc[h * per_head:(h + 1) * per_head], axis=0)
                    a_sw = pltpu.roll(a, HEAD_DIM, 1)
                    halves.append(a / a_sw if half == 0 else a_sw / a)
                pair = 2 * g + j
                o_ref[0, q_rows, pair * LANES:(pair + 1) * LANES] = jnp.where(lo, halves[0], halves[1]).astype(o_ref.dtype)


def _gqa_kernel(*refs, lat_chunks, ctx_chunks):
    if ctx_chunks is None:
        kv_c, q_l, kv_l, o_l, s_ref, p_ref = refs
    else:
        q_c, kv_c, q_l, kv_l, o_c, o_l, s_ref, p_ref = refs
    tq = GQA_Q_TILE

    def tile(j, carry):
        _gqa_tile(q_l, o_l, pl.ds(pl.multiple_of(j * tq, tq), tq), (kv_c, kv_l), lat_chunks, s_ref, p_ref)
        return carry

    lax.fori_loop(0, q_l.shape[1] // tq, tile, 0)
    if ctx_chunks is not None:
        _gqa_tile(q_c, o_c, slice(None), (kv_c,), ctx_chunks, s_ref, p_ref)


def _key_chunks(sizes):
    starts = [0]
    for n in sizes:
        starts.append(starts[-1] + n)
    chunks = []
    for c0 in range(0, starts[-1], GQA_KEY_CHUNK):
        c1 = min(c0 + GQA_KEY_CHUNK, starts[-1])
        pieces = [(src, max(c0, starts[src]) - starts[src], min(c1, starts[src + 1]) - starts[src])
                  for src in range(len(sizes)) if max(c0, starts[src]) < min(c1, starts[src + 1])]
        chunks.append(tuple(pieces))
    return tuple(chunks)


def _gqa(q_lat, kv_ctx, kv_lat, q_ctx=None):
    b, n_lat, _ = q_lat.shape
    n_ctx = kv_ctx.shape[1]
    assert n_lat % GQA_Q_TILE == 0 and n_ctx == GQA_Q_TILE
    with_ctx = q_ctx is not None
    stacked_rows = GQA_Q_HEADS // GQA_KV_HEADS * GQA_Q_TILE
    ins = ([q_ctx] if with_ctx else []) + [kv_ctx, q_lat, kv_lat]
    out_shape = [jax.ShapeDtypeStruct(q_lat.shape, MXU_DTYPE)]
    if with_ctx:
        out_shape.insert(0, jax.ShapeDtypeStruct(q_ctx.shape, MXU_DTYPE))
    outs = pl.pallas_call(
        functools.partial(_gqa_kernel, lat_chunks=_key_chunks((n_ctx, n_lat)),
                          ctx_chunks=_key_chunks((n_ctx,)) if with_ctx else None),
        grid=(b,),
        in_specs=[_batch_block(a) for a in ins],
        out_specs=[_batch_block(s) for s in out_shape],
        out_shape=out_shape,
        scratch_shapes=[pltpu.VMEM((2, stacked_rows, GQA_KEY_CHUNK), F32),
                        pltpu.VMEM((2, stacked_rows, GQA_KEY_CHUNK), MXU_DTYPE)],
        compiler_params=_params(1),
        name="gqa_attention",
    )(*ins)
    return (outs[0], outs[1]) if with_ctx else (None, outs[0])


def _post_kernel(yr_ref, yn_ref, yg_ref, gate_ref, s_ref, gt1_ref, sh2_ref, sc2_ref, gt2_ref,
                 gpost_ref, gpre_ref, gpost2_ref, wr_ref, wn_ref, wg_ref, wout_ref, w1_ref, w2_ref, *rest):
    n_cast = (len(rest) - 1) // 2
    o_ref = rest[n_cast]
    _cast_slabs(rest[:n_cast], rest[n_cast + 1:])
    tm, d = o_ref.shape[1:]

    def mixed(rows):
        y = None
        for i, (br, w_ref) in enumerate(((yr_ref, wr_ref), (yn_ref, wn_ref), (yg_ref, wg_ref))):
            z = _dot(br[0, rows, :], w_ref[...]) * gate_ref[0, rows, i * d:(i + 1) * d].astype(F32)
            y = z if y is None else y + z
        return _dot(y.astype(MXU_DTYPE), wout_ref[...])

    def mlp(rows, y):
        x = s_ref[0, rows, :] + gt1_ref[0] * (_rms(y) * gpost_ref[...])
        h = _rms(x) * gpre_ref[...]
        h = (h * (1.0 + sc2_ref[0]) + sh2_ref[0]).astype(MXU_DTYPE)
        acc = None
        for j in range(D_FF // d):
            u = jnp.maximum(_dot(h, w1_ref[:, j * d:(j + 1) * d]), 0.0)
            part = _dot((u * u).astype(MXU_DTYPE), w2_ref[j * d:(j + 1) * d, :])
            acc = part if acc is None else acc + part
        o_ref[0, rows, :] = x + gt2_ref[0] * (_rms(acc) * gpost2_ref[...])

    groups = [slice(r0, r0 + POST_ROW_GROUP) for r0 in range(0, tm, POST_ROW_GROUP)]
    ys = [mixed(rows) for rows in groups]
    for rows, y in zip(groups, ys):
        mlp(rows, y)


def _post(l, y_ret, y_na, y_gqa, gates, stream, is_ctx, mod3, mod_rows, gains, weights, cast=()):
    tg = _TokenGrid(stream, is_ctx, l, mod_rows)
    d = tg.d
    cast_specs = [tg.cast_specs(a, l + 1) for a in cast]
    outs = pl.pallas_call(
        _post_kernel,
        grid=tg.grid,
        in_specs=[tg.tok(HEAD_W), tg.tok(HEAD_W), tg.tok(HEAD_W), tg.tok(3 * d), tg.tok(d)]
        + [tg.mod(chunk) for chunk in (2, 3, 4, 5)]
        + [_layer(a, l) for a in gains] + [_resident(w.shape) for w in weights] + [s[0] for s in cast_specs],
        out_specs=[tg.tok(d)] + [s[1] for s in cast_specs],
        out_shape=[jax.ShapeDtypeStruct(tg.shape + (d,), F32)] + [s[2] for s in cast_specs],
        compiler_params=_params(2),
        name="merge_mlp_ctx" if is_ctx else "merge_mlp",
    )(tg.view(y_ret), tg.view(y_na), tg.view(y_gqa), tg.view(gates), tg.view(stream),
      mod3, mod3, mod3, mod3, *gains, *weights, *cast)
    return (tg.unview(outs[0]),) + tuple(outs[1:])


def _rope_tables(n_latent):
    t = np.arange(n_latent)
    pos = np.stack([t // GRID_W, t % GRID_W], axis=-1).astype(np.float64)
    n_freq = HEAD_DIM // 4
    inv_freq = ROPE_BASE ** (-np.arange(n_freq, dtype=np.float64) / n_freq)
    lane = np.arange(LANES) % HEAD_DIM
    axis, second, freq = lane // 32, (lane % 32) // 16, lane % 16
    ang = pos[:, axis] * inv_freq[freq][None, :]
    return jnp.asarray(np.cos(ang), F32), jnp.asarray(np.sin(ang) * np.where(second == 1, 1.0, -1.0), F32)


def kernel(x, c, ctx, c_ctx, w_mod, b_mod, g_pre_mix, g_post_mix, g_pre_mlp, g_post_mlp, w_in, ret_decay_logit, na_rel_bias, gqa_q_norm, gqa_k_norm, w_br_ret, w_br_na, w_br_gqa, w_out, w_mlp_in, w_mlp_out):
    b, n, d = x.shape
    depth = w_mod.shape[0]
    cdt = MXU_DTYPE

    rope_tables = _rope_tables(n)
    lane_head = np.arange(2 * LANES) // HEAD_DIM
    gsum = jnp.asarray(lane_head[:, None] == lane_head[None, :], cdt)

    n_rows = -(-(b + 1) // 8) * 8
    cs = jnp.concatenate([c, jnp.zeros((n_rows - b - 1, d), F32), c_ctx[None, :]], axis=0)
    mod, w_in_l = _modulation(cs, w_mod, b_mod, w_in)
    mod3 = mod.reshape(depth * n_rows, 1, 6 * d)
    post_weight_stacks = (w_br_ret, w_br_na, w_br_gqa, w_out, w_mlp_in, w_mlp_out)
    post_gains = tuple(g[:, None, :] for g in (g_post_mix, g_pre_mlp, g_post_mlp))
    pre_gain = g_pre_mix[:, None, :]
    q_gain = jnp.tile(gqa_q_norm, (1, 2))[:, None, :]
    k_gain = jnp.tile(gqa_k_norm, (1, 2))[:, None, :]
    logit_tile = jnp.broadcast_to(ret_decay_logit.astype(F32)[..., None, None], (depth, 2, RET_HEADS, 8, LANES))
    bias_tables = _na_bias_table(na_rel_bias)

    ctx_s, lat_s = ctx, x
    for l in range(depth):
        last = l == depth - 1
        pc = _in_projection(l, ctx_s, True, mod3, n_rows, pre_gain, w_in_l, rope_tables, q_gain, k_gain, gsum,
                            kv_only=last)
        pt = _in_projection(l, lat_s, False, mod3, n_rows, pre_gain, w_in_l, rope_tables, q_gain, k_gain, gsum,
                            cast=post_weight_stacks)
        post_weights = pt[9:]
        ret_c, ret_l = _retention(l, logit_tile, pc[0:2] if last else pc[0:3], pt[0:3], not last)
        na_c, na_l = _neighbourhood(l, pc[4:6] if last else pc[3:6], pt[3:6], bias_tables, not last)
        gqa_c, gqa_l = _gqa(pt[6], pc[7], pt[7], None if last else pc[6])
        post_l = _post(l, ret_l, na_l, gqa_l, pt[8], lat_s, False, mod3, n_rows, post_gains, post_weights,
                       cast=() if last else (w_in,))
        lat_s = post_l[0]
        if not last:
            w_in_l = post_l[1]
            ctx_s, = _post(l, ret_c, na_c, gqa_c, pc[8], ctx_s, True, mod3, n_rows, post_gains, post_weights)
    return lat_s
```
